```python
import math
import jax, jax.numpy as jnp
from jax import lax
import numpy as np

D_MODEL = 2048
BATCH = 16
SEQ = 2048
DEPTH = 1
DEC_BATCH = 32
DEC_SEQ = 16
PAST_LEN = 4096

CHUNK = 64
Q_BLOCK = 128
NORM_EPS = 1e-6

MLA_HEADS = 16
Q_LORA = 512
KV_LORA = 512
NOPE_DIM = 128
ROPE_DIM = 64
V_DIM = D_MODEL // MLA_HEADS
ROPE_BASE = 10000.0
MLA_SCALE = 1.0 / math.sqrt(NOPE_DIM + ROPE_DIM)
NEG_INF = -1e30

HGRN_HEADS = 16
HGRN_DK = 128
HGRN_DV = D_MODEL // HGRN_HEADS
HGRN_F = HGRN_HEADS * HGRN_DK

N_EXPERTS = 32
TOP_K = 4
EXPERT_FF = D_MODEL
SWIGLU_ALPHA = 1.702
SWIGLU_LIMIT = 7.0
EXPERT_BLOCK = 128

IN_SPLITS = (Q_LORA, KV_LORA, ROPE_DIM, HGRN_F, HGRN_F, D_MODEL, D_MODEL, D_MODEL, D_MODEL)
IN_COLS = sum(IN_SPLITS)
IN_OFFSETS = tuple(int(v) for v in np.cumsum(IN_SPLITS)[:-1])

kernel_name = "hybrid_mla_hgrn2_moe_stream"


def rmsnorm(x, g):
    xf = x.astype(jnp.float32)
    y = xf * lax.rsqrt(jnp.mean(xf * xf, axis=-1, keepdims=True) + NORM_EPS)
    return (y * g.astype(jnp.float32)).astype(x.dtype)


def rope(x, pos):
    half = x.shape[-1] // 2
    inv = ROPE_BASE ** (-jnp.arange(half, dtype=jnp.float32) / half)
    ang = pos.astype(jnp.float32)[:, None] * inv[None, :]
    if x.ndim == 4:
        ang = ang[:, None, :]
    cos = jnp.cos(ang).astype(x.dtype)
    sin = jnp.sin(ang).astype(x.dtype)
    x1, x2 = x[..., :half], x[..., half:]
    return jnp.concatenate([x1 * cos - x2 * sin, x1 * sin + x2 * cos], axis=-1)


def mla_queries(c_q, q_norm_g, w_uq, pos):
    q = jnp.einsum('bsc,chd->bshd', rmsnorm(c_q, q_norm_g), w_uq)
    return q[..., :NOPE_DIM], rope(q[..., NOPE_DIM:], pos)


def mla_prompt(q_nope, q_rope, lat, k_rope, w_uk, w_uv):
    B, S = q_nope.shape[:2]
    k_nope = jnp.einsum('bsc,chn->bshn', lat, w_uk)
    v = jnp.einsum('bsc,chv->bshv', lat, w_uv)
    nb = S // Q_BLOCK
    key_chunk = jnp.arange(S) // CHUNK
    qn = q_nope.reshape(B, nb, Q_BLOCK, MLA_HEADS, NOPE_DIM).swapaxes(0, 1)
    qr = q_rope.reshape(B, nb, Q_BLOCK, MLA_HEADS, ROPE_DIM).swapaxes(0, 1)

    def one_block(args):
        j, qn_b, qr_b = args
        s = (jnp.einsum('bqhn,bkhn->bhqk', qn_b, k_nope)
             + jnp.einsum('bqhr,bkr->bhqk', qr_b, k_rope)).astype(jnp.float32) * MLA_SCALE
        q_chunk = (j * Q_BLOCK + jnp.arange(Q_BLOCK)) // CHUNK
        s = jnp.where(key_chunk[None, :] <= q_chunk[:, None], s, NEG_INF)
        p = jax.nn.softmax(s, axis=-1).astype(v.dtype)
        return jnp.einsum('bhqk,bkhv->bqhv', p, v)

    o = lax.map(one_block, (jnp.arange(nb), qn, qr))
    return o.swapaxes(0, 1).reshape(B, S, MLA_HEADS * V_DIM)


def mla_sample(q_nope, q_rope, cache_lat, cache_kr, lat_new, kr_new, w_uk, w_uv):
    Bd, Sd = q_nope.shape[:2]
    P = cache_lat.shape[1]
    q_lat = jnp.einsum('bshn,chn->bshc', q_nope, w_uk)
    s_past = jnp.einsum('bshc,btc->bhst', q_lat, cache_lat) + jnp.einsum('bshr,btr->bhst', q_rope, cache_kr)
    s_new = jnp.einsum('bshc,btc->bhst', q_lat, lat_new) + jnp.einsum('bshr,btr->bhst', q_rope, kr_new)
    s = jnp.concatenate([s_past, s_new], axis=-1).astype(jnp.float32) * MLA_SCALE
    p = jax.nn.softmax(s, axis=-1).astype(q_nope.dtype)
    o_lat = (jnp.einsum('bhst,btc->bshc', p[..., :P], cache_lat)
             + jnp.einsum('bhst,btc->bshc', p[..., P:], lat_new))
    o = jnp.einsum('bshc,chv->bshv', o_lat, w_uv)
    return o.reshape(Bd, Sd, MLA_HEADS * V_DIM)


def hgrn_inputs(hq, hf, hi, lb):
    b, s = hq.shape[:2]
    q = hq.reshape(b, s, HGRN_HEADS, HGRN_DK).astype(jnp.float32)
    f = lb + (1.0 - lb) * jax.nn.sigmoid(hf.reshape(b, s, HGRN_HEADS, HGRN_DK).astype(jnp.float32))
    v = hi.reshape(b, s, HGRN_HEADS, HGRN_DV).astype(jnp.float32)
    return q, jnp.log(f), 1.0 - f, v


def hgrn_block(S0, q, logf, k, v):
    L = q.shape[1]
    cum = jnp.cumsum(logf, axis=1)
    q_dec = q * jnp.exp(cum)
    k_dec = k * jnp.exp(-cum)
    causal = jnp.tril(jnp.ones((L, L), dtype=bool))
    att = jnp.where(causal, jnp.einsum('blhk,bmhk->bhlm', q_dec, k_dec), 0.0)
    o = jnp.einsum('blhk,bhkv->blhv', q_dec, S0) + jnp.einsum('bhlm,bmhv->blhv', att, v)
    tot = cum[:, -1]
    S_new = (jnp.exp(tot)[..., None] * S0
             + jnp.einsum('blhk,blhv->bhkv', k * jnp.exp(tot[:, None] - cum), v))
    return S_new, o


def hgrn_chunks(q, logf, k, v, S0):
    B, S = q.shape[:2]
    nb = S // CHUNK
    blk = lambda t: t.reshape(B, nb, CHUNK, *t.shape[2:]).swapaxes(0, 1)
    S_fin, o = lax.scan(lambda st, xs: hgrn_block(st, *xs), S0, (blk(q), blk(logf), blk(k), blk(v)))
    return S_fin, o.swapaxes(0, 1).reshape(B, S, HGRN_HEADS, HGRN_DV)


def hgrn_readout(o, hg, g):
    b, s = o.shape[:2]
    on = o * lax.rsqrt(jnp.mean(o * o, axis=-1, keepdims=True) + NORM_EPS) * g.astype(jnp.float32)
    return (on.reshape(b, s, D_MODEL) * jax.nn.silu(hg.astype(jnp.float32))).astype(hg.dtype)


def mixer_layer(h, pos, past, norm_g, w_in_l, q_norm_g_l, w_uq_l, kv_norm_g_l, w_uk_l, w_uv_l,
                lb, hgrn_g_l, w_o_l):
    u = rmsnorm(h, norm_g)
    c_q, c_kv, k_pe, hq, hf, hi, hg, ga, gb = jnp.split(u @ w_in_l, IN_OFFSETS, axis=-1)
    q_nope, q_rope = mla_queries(c_q, q_norm_g_l, w_uq_l, pos)
    lat = rmsnorm(c_kv, kv_norm_g_l)
    kr = rope(k_pe, pos)
    q, logf, k, v = hgrn_inputs(hq, hf, hi, lb)
    if past is None:
        o_a = mla_prompt(q_nope, q_rope, lat, kr, w_uk_l, w_uv_l)
        S0 = jnp.zeros((h.shape[0], HGRN_HEADS, HGRN_DK, HGRN_DV), jnp.float32)
        S_new, o_h = hgrn_chunks(q, logf, k, v, S0)
    else:
        cache_lat, cache_kr, state = past
        o_a = mla_sample(q_nope, q_rope, cache_lat, cache_kr, lat, kr, w_uk_l, w_uv_l)
        S_new, o_h = hgrn_block(state.astype(jnp.float32), q, logf, k, v)
    o_b = hgrn_readout(o_h, hg, hgrn_g_l)
    merged = jax.nn.sigmoid(ga) * o_a + jax.nn.sigmoid(gb) * o_b
    return h + merged @ w_o_l, lat, kr, S_new.astype(h.dtype)


def moe(x2d, w_router, b_router, w_gu, b_gu, w_dn, b_dn):
    T, D = x2d.shape
    logits = (x2d @ w_router + b_router).astype(jnp.float32)
    top_val, top_idx = lax.top_k(logits, TOP_K)
    gate_w = jax.nn.softmax(top_val, axis=-1)
    A = T * TOP_K
    flat_e = top_idx.reshape(A)
    order = jnp.argsort(flat_e)
    e_sorted = flat_e[order]
    tok_sorted = (order // TOP_K).astype(jnp.int32)
    w_sorted = gate_w.reshape(A)[order]
    counts = jnp.bincount(flat_e, length=N_EXPERTS)
    padded = (counts + EXPERT_BLOCK - 1) // EXPERT_BLOCK * EXPERT_BLOCK
    pad_end = jnp.cumsum(padded)
    pad_start = pad_end - padded
    start = jnp.cumsum(counts) - counts
    dest = pad_start[e_sorted] + jnp.arange(A) - start[e_sorted]
    n_blocks = -(-A // EXPERT_BLOCK) + N_EXPERTS
    P = n_blocks * EXPERT_BLOCK
    buf_tok = jnp.full((P,), T, jnp.int32).at[dest].set(tok_sorted)
    buf_w = jnp.zeros((P,), jnp.float32).at[dest].set(w_sorted)
    block_e = jnp.minimum(jnp.searchsorted(pad_end, jnp.arange(n_blocks) * EXPERT_BLOCK, side='right'),
                          N_EXPERTS - 1)
    xp = jnp.concatenate([x2d, jnp.zeros((1, D), x2d.dtype)], axis=0)

    def expert_block(args):
        tok, e = args
        hid = xp[tok] @ w_gu[e] + b_gu[e]
        gate = jnp.minimum(hid[:, :EXPERT_FF], SWIGLU_LIMIT)
        up = jnp.clip(hid[:, EXPERT_FF:], -SWIGLU_LIMIT, SWIGLU_LIMIT)
        act = gate * jax.nn.sigmoid(SWIGLU_ALPHA * gate) * (up + 1.0)
        return act @ w_dn[e] + b_dn[e]

    out = lax.map(expert_block, (buf_tok.reshape(n_blocks, EXPERT_BLOCK), block_e))
    out = out.reshape(P, D) * buf_w[:, None].astype(out.dtype)
    return jnp.zeros((T + 1, D), out.dtype).at[buf_tok].add(out)[:T]


def moe_layer(h, norm_g, w_router_l, b_router_l, w_gu_l, b_gu_l, w_dn_l, b_dn_l):
    u = rmsnorm(h, norm_g)
    y = moe(u.reshape(-1, D_MODEL), w_router_l, b_router_l, w_gu_l, b_gu_l, w_dn_l, b_dn_l)
    return h + y.reshape(h.shape)


def setup_inputs(seed: int = 0) -> dict:
    key = jax.random.key(seed)
    ks = jax.random.split(key, 24)
    f32 = jnp.float32
    nrm = lambda k, shape, scale: jax.random.normal(k, shape, f32) * scale
    gain = lambda k, shape: 1.0 + 0.02 * jax.random.normal(k, shape, f32)
    return {
        'x_prompt': nrm(ks[0], (BATCH, SEQ, D_MODEL), 1.0),
        'x_sample': nrm(ks[1], (DEC_BATCH, DEC_SEQ, D_MODEL), 1.0),
        'cache_mla_latent': nrm(ks[2], (DEPTH, DEC_BATCH, PAST_LEN, KV_LORA), 1.0),
        'cache_mla_krope': nrm(ks[3], (DEPTH, DEC_BATCH, PAST_LEN, ROPE_DIM), 1.0),
        'state_hgrn': nrm(ks[4], (DEPTH, DEC_BATCH, HGRN_HEADS, HGRN_DK, HGRN_DV), 0.5),
        'norm_mix_g': gain(ks[5], (DEPTH, D_MODEL)),
        'w_in': nrm(ks[6], (DEPTH, D_MODEL, IN_COLS), D_MODEL ** -0.5),
        'q_norm_g': gain(ks[7], (DEPTH, Q_LORA)),
        'w_uq': nrm(ks[8], (DEPTH, Q_LORA, MLA_HEADS, NOPE_DIM + ROPE_DIM), Q_LORA ** -0.5),
        'kv_norm_g': gain(ks[9], (DEPTH, KV_LORA)),
        'w_uk': nrm(ks[10], (DEPTH, KV_LORA, MLA_HEADS, NOPE_DIM), KV_LORA ** -0.5),
        'w_uv': nrm(ks[11], (DEPTH, KV_LORA, MLA_HEADS, V_DIM), KV_LORA ** -0.5),
        'hgrn_lb_raw': nrm(ks[12], (DEPTH + 1, HGRN_F), 0.1),
        'hgrn_norm_g': gain(ks[13], (DEPTH, HGRN_DV)),
        'w_o': nrm(ks[14], (DEPTH, D_MODEL, D_MODEL), D_MODEL ** -0.5),
        'norm_ffn_g': gain(ks[15], (DEPTH, D_MODEL)),
        'w_router': nrm(ks[16], (DEPTH, D_MODEL, N_EXPERTS), D_MODEL ** -0.5),
        'b_router': nrm(ks[17], (DEPTH, N_EXPERTS), 0.01),
        'w_gate_up': nrm(ks[18], (DEPTH, N_EXPERTS, D_MODEL, 2 * EXPERT_FF), D_MODEL ** -0.5),
        'b_gate_up': nrm(ks[19], (DEPTH, N_EXPERTS, 2 * EXPERT_FF), 0.01),
        'w_down': nrm(ks[20], (DEPTH, N_EXPERTS, EXPERT_FF, D_MODEL), EXPERT_FF ** -0.5),
        'b_down': nrm(ks[21], (DEPTH, N_EXPERTS, D_MODEL), 0.01),
        'norm_final_g': gain(ks[22], (D_MODEL,)),
    }


def reference(x_prompt, x_sample, cache_mla_latent, cache_mla_krope, state_hgrn,
              norm_mix_g, w_in, q_norm_g, w_uq, kv_norm_g, w_uk, w_uv, hgrn_lb_raw, hgrn_norm_g, w_o,
              norm_ffn_g, w_router, b_router, w_gate_up, b_gate_up, w_down, b_down, norm_final_g):
    pos_p = jnp.arange(x_prompt.shape[1])
    pos_s = cache_mla_latent.shape[2] + jnp.arange(x_sample.shape[1])
    lb_all = jnp.cumsum(jax.nn.softmax(hgrn_lb_raw.astype(jnp.float32), axis=0), axis=0)
    hp, hs = x_prompt, x_sample
    lat_p, kr_p, st_p, lat_s, kr_s, st_s = [], [], [], [], [], []
    for l in range(DEPTH):
        lb = lb_all[l].reshape(HGRN_HEADS, HGRN_DK)
        mix_w = (norm_mix_g[l], w_in[l], q_norm_g[l], w_uq[l], kv_norm_g[l], w_uk[l], w_uv[l],
                 lb, hgrn_norm_g[l], w_o[l])
        ffn_w = (norm_ffn_g[l], w_router[l], b_router[l], w_gate_up[l], b_gate_up[l], w_down[l], b_down[l])
        hp, lat, kr, st = mixer_layer(hp, pos_p, None, *mix_w)
        hp = moe_layer(hp, *ffn_w)
        lat_p.append(lat); kr_p.append(kr); st_p.append(st)
        past = (cache_mla_latent[l], cache_mla_krope[l], state_hgrn[l])
        hs, lat, kr, st = mixer_layer(hs, pos_s, past, *mix_w)
        hs = moe_layer(hs, *ffn_w)
        lat_s.append(lat); kr_s.append(kr); st_s.append(st)
    y_prompt = rmsnorm(hp, norm_final_g)
    y_sample = rmsnorm(hs, norm_final_g)
    return (y_prompt, y_sample, jnp.stack(lat_p), jnp.stack(kr_p), jnp.stack(st_p),
            jnp.stack(lat_s), jnp.stack(kr_s), jnp.stack(st_s))
```

```python
import functools
import math

import jax
import jax.numpy as jnp
from jax import lax
from jax.experimental import pallas as pl
from jax.experimental.pallas import tpu as pltpu

F32 = jnp.float32
BF16 = jnp.bfloat16

CHUNK = 64
NORM_EPS = 1e-6
ROPE_BASE = 10000.0
NEG_INF = -1e30
TOP_K = 4
SWIGLU_ALPHA = 1.702
SWIGLU_LIMIT = 7.0

LANE = 128
VMEM_LIMIT = 56 * 1024 * 1024


def _cparams(*sem):
    return pltpu.CompilerParams(dimension_semantics=sem, vmem_limit_bytes=VMEM_LIMIT)


def _dot(a, b):
    return jnp.dot(a, b, preferred_element_type=F32)


def _dot_nt(a, b):
    return lax.dot_general(a, b, (((1,), (1,)), ((), ())), preferred_element_type=F32)


def _rms(x, g):
    return x * lax.rsqrt(jnp.mean(x * x, axis=-1, keepdims=True) + NORM_EPS) * g


def _sigmoid(x):
    return 1.0 / (1.0 + jnp.exp(-x))


def _small_proj_kernel(x_ref, g_ref, w_ref, qg_ref, kvg_ref, cos_ref, sin_ref,
                       u_ref, cqn_ref, lat_ref, kr_ref, krb_ref, *, ql, kl, rope):
    u = _rms(x_ref[...], g_ref[...]).astype(BF16)
    u_ref[...] = u
    y = _dot(u, w_ref[...])
    cqn_ref[...] = _rms(y[:, :ql], qg_ref[...]).astype(BF16)
    lat_ref[...] = _rms(y[:, ql:ql + kl], kvg_ref[...])
    o = ql + kl
    kr = y[:, o:o + LANE] * cos_ref[...] + y[:, o + LANE:o + 2 * LANE] * sin_ref[...]
    kr_ref[...] = kr[:, :rope]
    krb_ref[...] = kr.astype(BF16)


def _small_proj(x2, g, w_small, qg, kvg, cos_t, sin_t, *, tm, ql, kl, rope):
    T, D = x2.shape
    n = w_small.shape[1]
    nblk = cos_t.shape[0] // tm
    row = lambda i: (i, 0)
    fix = lambda i: (0, 0)
    tab = lambda i: (i % nblk, 0)
    return pl.pallas_call(
        functools.partial(_small_proj_kernel, ql=ql, kl=kl, rope=rope),
        grid=(T // tm,),
        in_specs=[pl.BlockSpec((tm, D), row), pl.BlockSpec((1, D), fix), pl.BlockSpec((D, n), fix),
                  pl.BlockSpec((1, ql), fix), pl.BlockSpec((1, kl), fix),
                  pl.BlockSpec((tm, LANE), tab), pl.BlockSpec((tm, LANE), tab)],
        out_specs=[pl.BlockSpec((tm, D), row), pl.BlockSpec((tm, ql), row), pl.BlockSpec((tm, kl), row),
                   pl.BlockSpec((tm, rope), row), pl.BlockSpec((tm, LANE), row)],
        out_shape=[jax.ShapeDtypeStruct((T, D), BF16), jax.ShapeDtypeStruct((T, ql), BF16),
                   jax.ShapeDtypeStruct((T, kl), F32), jax.ShapeDtypeStruct((T, rope), F32),
                   jax.ShapeDtypeStruct((T, LANE), BF16)],
        compiler_params=_cparams("parallel"),
        name="small_proj",
    )(x2, g, w_small, qg, kvg, cos_t, sin_t)


def _big_proj_kernel(u_ref, *refs, nseg):
    u = u_ref[...]
    for s in range(nseg):
        refs[nseg + s][...] = _dot(u, refs[s][...]).astype(refs[nseg + s].dtype)


def _big_proj(u, ws, out_dtypes, *, tm, tn):
    T, D = u.shape
    n = ws[0].shape[1]
    nseg = len(ws)
    return pl.pallas_call(
        functools.partial(_big_proj_kernel, nseg=nseg),
        grid=(T // tm, n // tn),
        in_specs=[pl.BlockSpec((tm, D), lambda i, j: (i, 0))]
                 + [pl.BlockSpec((D, tn), lambda i, j: (0, j)) for _ in ws],
        out_specs=[pl.BlockSpec((tm, tn), lambda i, j: (i, j)) for _ in ws],
        out_shape=[jax.ShapeDtypeStruct((T, n), dt) for dt in out_dtypes],
        compiler_params=_cparams("parallel", "arbitrary"),
        name="big_proj",
    )(u, *ws)


def _q_proj_kernel(c_ref, wr_ref, ws_ref, cos_ref, sin_ref, q_ref, *, heads):
    c = c_ref[...]
    raw = _dot(c, wr_ref[...])
    sw = _dot(c, ws_ref[...])
    cos = cos_ref[...]
    sin = sin_ref[...]
    for h in range(heads):
        b = 2 * LANE * h
        q_ref[:, b:b + LANE] = raw[:, b:b + LANE].astype(BF16)
        r = raw[:, b + LANE:b + 2 * LANE] * cos + sw[:, LANE * h:LANE * (h + 1)] * sin
        q_ref[:, b + LANE:b + 2 * LANE] = r.astype(BF16)


def _q_proj(cqn, wq_raw, wq_sw, cos_t, sin_t, *, tm, heads):
    T, ql = cqn.shape
    nblk = cos_t.shape[0] // tm
    row = lambda i: (i, 0)
    fix = lambda i: (0, 0)
    tab = lambda i: (i % nblk, 0)
    n = wq_raw.shape[1]
    return pl.pallas_call(
        functools.partial(_q_proj_kernel, heads=heads),
        grid=(T // tm,),
        in_specs=[pl.BlockSpec((tm, ql), row), pl.BlockSpec(wq_raw.shape, fix), pl.BlockSpec(wq_sw.shape, fix),
                  pl.BlockSpec((tm, LANE), tab), pl.BlockSpec((tm, LANE), tab)],
        out_specs=pl.BlockSpec((tm, n), row),
        out_shape=jax.ShapeDtypeStruct((T, n), BF16),
        compiler_params=_cparams("parallel"),
        name="q_proj",
    )(cqn, wq_raw, wq_sw, cos_t, sin_t)


def _kv_proj_kernel(lat_ref, krb_ref, wk_ref, wv_ref, k_ref, v_ref, *, heads):
    lat = lat_ref[...].astype(BF16)
    kn = _dot(lat, wk_ref[...]).astype(BF16)
    v_ref[...] = _dot(lat, wv_ref[...]).astype(BF16)
    krb = krb_ref[...]
    for h in range(heads):
        b = 2 * LANE * h
        k_ref[:, b:b + LANE] = kn[:, LANE * h:LANE * (h + 1)]
        k_ref[:, b + LANE:b + 2 * LANE] = krb


def _kv_proj(lat, krb, wk, wv, *, tm, heads):
    T, kl = lat.shape
    row = lambda i: (i, 0)
    fix = lambda i: (0, 0)
    return pl.pallas_call(
        functools.partial(_kv_proj_kernel, heads=heads),
        grid=(T // tm,),
        in_specs=[pl.BlockSpec((tm, kl), row), pl.BlockSpec((tm, LANE), row),
                  pl.BlockSpec(wk.shape, fix), pl.BlockSpec(wv.shape, fix)],
        out_specs=[pl.BlockSpec((tm, 2 * LANE * heads), row), pl.BlockSpec((tm, wv.shape[1]), row)],
        out_shape=[jax.ShapeDtypeStruct((T, 2 * LANE * heads), BF16), jax.ShapeDtypeStruct((T, wv.shape[1]), BF16)],
        compiler_params=_cparams("parallel"),
        name="kv_proj",
    )(lat, krb, wk, wv)


def _attn_kernel(q_ref, k_ref, v_ref, o_ref, *, tq):
    qi = pl.program_id(2)
    q = q_ref[0]

    def update(carry, s, v):
        m, l, acc = carry
        m_new = jnp.maximum(m, jnp.max(s, axis=-1, keepdims=True))
        alpha = jnp.exp(m - m_new)
        p = jnp.exp(s - m_new)
        l = alpha * l + jnp.sum(p, axis=-1, keepdims=True)
        acc = alpha * acc + _dot(p.astype(BF16), v)
        return m_new, l, acc

    def body(j, carry):
        start = pl.multiple_of(j * tq, tq)
        k = k_ref[0, pl.ds(start, tq), :]
        v = v_ref[0, pl.ds(start, tq), :]
        return update(carry, _dot_nt(q, k), v)

    init = (jnp.full((tq, 1), NEG_INF, F32), jnp.zeros((tq, 1), F32), jnp.zeros((tq, v_ref.shape[2]), F32))
    carry = lax.fori_loop(0, qi, body, init)
    start = pl.multiple_of(qi * tq, tq)
    k = k_ref[0, pl.ds(start, tq), :]
    v = v_ref[0, pl.ds(start, tq), :]
    qc = lax.broadcasted_iota(jnp.int32, (tq, tq), 0) // CHUNK
    kc = lax.broadcasted_iota(jnp.int32, (tq, tq), 1) // CHUNK
    s = jnp.where(kc <= qc, _dot_nt(q, k), NEG_INF)
    m, l, acc = update(carry, s, v)
    o_ref[0] = (acc / l).astype(BF16)


def _attention(qf, kf, v, *, heads, tq):
    B, S, _ = qf.shape
    vd = v.shape[2] // heads
    return pl.pallas_call(
        functools.partial(_attn_kernel, tq=tq),
        grid=(B, heads, S // tq),
        in_specs=[pl.BlockSpec((1, tq, 2 * LANE), lambda b, h, i: (b, i, h)),
                  pl.BlockSpec((1, S, 2 * LANE), lambda b, h, i: (b, 0, h)),
                  pl.BlockSpec((1, S, vd), lambda b, h, i: (b, 0, h))],
        out_specs=pl.BlockSpec((1, tq, vd), lambda b, h, i: (b, i, h)),
        out_shape=jax.ShapeDtypeStruct((B, S, heads * vd), BF16),
        compiler_params=_cparams("parallel", "parallel", "arbitrary"),
        name="mla_prompt_attn",
    )(qf, kf, v)


def _attn_sample_kernel(q_ref, cl_ref, ck_ref, ln_ref, kn_ref, wuk_ref, wuv_ref, o_ref,
                        ql_ref, qr_ref, ol_ref, *, heads, rope):
    sd = q_ref.shape[1]
    for h in range(heads):
        b = 2 * LANE * h
        ql_ref[h * sd:(h + 1) * sd, :] = _dot(q_ref[0, :, b:b + LANE], wuk_ref[h]).astype(BF16)
        qr_ref[h * sd:(h + 1) * sd, :] = q_ref[0, :, b + LANE:b + 2 * LANE]
    ql = ql_ref[...]
    qr = qr_ref[...][:, :rope]
    cl = cl_ref[0].astype(BF16)
    ck = ck_ref[0].astype(BF16)
    ln = ln_ref[0].astype(BF16)
    kn = kn_ref[0].astype(BF16)
    s_past = _dot_nt(ql, cl) + _dot_nt(qr, ck)
    s_new = _dot_nt(ql, ln) + _dot_nt(qr, kn)
    m = jnp.maximum(jnp.max(s_past, axis=-1, keepdims=True), jnp.max(s_new, axis=-1, keepdims=True))
    p_past = jnp.exp(s_past - m)
    p_new = jnp.exp(s_new - m)
    l = jnp.sum(p_past, axis=-1, keepdims=True) + jnp.sum(p_new, axis=-1, keepdims=True)
    o_lat = (_dot(p_past.astype(BF16), cl) + _dot(p_new.astype(BF16), ln)) / l
    ol_ref[...] = o_lat.astype(BF16)
    vd = wuv_ref.shape[2]
    for h in range(heads):
        o_ref[0, :, h * vd:(h + 1) * vd] = _dot(ol_ref[h * sd:(h + 1) * sd, :], wuv_ref[h]).astype(BF16)


def _attention_sample(qf, cache_lat, cache_kr, lat_new, kr_new, wuk_t, wuv, *, heads):
    B, sd, _ = qf.shape
    P, kl = cache_lat.shape[1:]
    rope = cache_kr.shape[2]
    vd = wuv.shape[2]
    b3 = lambda b: (b, 0, 0)
    fix = lambda b: (0, 0, 0)
    return pl.pallas_call(
        functools.partial(_attn_sample_kernel, heads=heads, rope=rope),
        grid=(B,),
        in_specs=[pl.BlockSpec((1, sd, qf.shape[2]), b3), pl.BlockSpec((1, P, kl), b3), pl.BlockSpec((1, P, rope), b3),
                  pl.BlockSpec((1, sd, kl), b3), pl.BlockSpec((1, sd, rope), b3),
                  pl.BlockSpec(wuk_t.shape, fix), pl.BlockSpec(wuv.shape, fix)],
        out_specs=pl.BlockSpec((1, sd, heads * vd), b3),
        out_shape=jax.ShapeDtypeStruct((B, sd, heads * vd), BF16),
        scratch_shapes=[pltpu.VMEM((heads * sd, kl), BF16), pltpu.VMEM((heads * sd, LANE), BF16),
                        pltpu.VMEM((heads * sd, kl), BF16)],
        compiler_params=_cparams("parallel"),
        name="mla_sample_attn",
    )(qf, cache_lat, cache_kr, lat_new, kr_new, wuk_t, wuv)


def _hgrn_kernel(*refs, heads, has_state):
    if has_state:
        hq_ref, hf_ref, hi_ref, hg_ref, lb_ref, g_ref, s0_ref, o_ref, sn_ref, st_ref = refs
    else:
        hq_ref, hf_ref, hi_ref, hg_ref, lb_ref, g_ref, o_ref, sn_ref, st_ref = refs
    c = pl.program_id(1)
    L = hq_ref.shape[1]
    dk = st_ref.shape[2]

    @pl.when(c == 0)
    def _():
        if has_state:
            for h in range(heads):
                st_ref[h] = s0_ref[0, h].T
        else:
            st_ref[...] = jnp.zeros_like(st_ref)

    row = lax.broadcasted_iota(jnp.int32, (L, L), 0)
    col = lax.broadcasted_iota(jnp.int32, (L, L), 1)
    causal = col <= row
    tri = causal.astype(F32)
    g = g_ref[...]
    for h in range(heads):
        sl = slice(h * dk, (h + 1) * dk)
        lb = lb_ref[:, sl]
        q = hq_ref[0, :, sl].astype(F32)
        f = lb + (1.0 - lb) * _sigmoid(hf_ref[0, :, sl])
        kk = 1.0 - f
        v = hi_ref[0, :, sl].astype(F32)
        cum = jnp.dot(tri, jnp.log(f), precision=lax.Precision.HIGHEST, preferred_element_type=F32)
        q_dec = (q * jnp.exp(cum)).astype(BF16)
        k_dec = (kk * jnp.exp(-cum)).astype(BF16)
        att = jnp.where(causal, _dot_nt(q_dec, k_dec), 0.0)
        st = st_ref[h]
        vb = v.astype(BF16)
        o = _dot_nt(q_dec, st.astype(BF16)) + _dot(att.astype(BF16), vb)
        tot = cum[L - 1:L, :]
        k2 = (kk * jnp.exp(tot - cum)).astype(BF16)
        st_ref[h] = jnp.exp(tot) * st + _dot(v.T.astype(BF16), k2)
        on = o * lax.rsqrt(jnp.mean(o * o, axis=-1, keepdims=True) + NORM_EPS) * g
        hg = hg_ref[0, :, sl].astype(F32)
        o_ref[0, :, sl] = (on * (hg * _sigmoid(hg))).astype(BF16)

    @pl.when(c == pl.num_programs(1) - 1)
    def _():
        for h in range(heads):
            sn_ref[0, h] = st_ref[h].T


def _hgrn(hq, hf, hi, hg, lb, g, s0, *, heads, L):
    B, S, F = hq.shape
    dk = F // heads
    blk = lambda b, c: (b, c, 0)
    fix2 = lambda b, c: (0, 0)
    st4 = lambda b, c: (b, 0, 0, 0)
    in_specs = [pl.BlockSpec((1, L, F), blk)] * 4 + [pl.BlockSpec((1, F), fix2), pl.BlockSpec((1, dk), fix2)]
    args = [hq, hf, hi, hg, lb, g]
    if s0 is not None:
        in_specs.append(pl.BlockSpec((1, heads, dk, dk), st4))
        args.append(s0)
    return pl.pallas_call(
        functools.partial(_hgrn_kernel, heads=heads, has_state=s0 is not None),
        grid=(B, S // L),
        in_specs=in_specs,
        out_specs=[pl.BlockSpec((1, L, F), blk), pl.BlockSpec((1, heads, dk, dk), st4)],
        out_shape=[jax.ShapeDtypeStruct((B, S, F), BF16), jax.ShapeDtypeStruct((B, heads, dk, dk), F32)],
        scratch_shapes=[pltpu.VMEM((heads, dk, dk), F32)],
        compiler_params=_cparams("parallel", "arbitrary"),
        name="hgrn",
    )(*args)


def _out_proj_kernel(oa_ref, ob_ref, ga_ref, gb_ref, x_ref, w_ref, o_ref, m_ref):
    @pl.when(pl.program_id(1) == 0)
    def _():
        ga = _sigmoid(ga_ref[...].astype(F32))
        gb = _sigmoid(gb_ref[...].astype(F32))
        m_ref[...] = (ga * oa_ref[...].astype(F32) + gb * ob_ref[...].astype(F32)).astype(BF16)

    o_ref[...] = x_ref[...] + _dot(m_ref[...], w_ref[...])


def _out_proj(oa, ob, ga, gb, x2, w_o, *, tm, tn):
    T, D = x2.shape
    row = lambda i, j: (i, 0)
    return pl.pallas_call(
        _out_proj_kernel,
        grid=(T // tm, D // tn),
        in_specs=[pl.BlockSpec((tm, D), row)] * 4
                 + [pl.BlockSpec((tm, tn), lambda i, j: (i, j)), pl.BlockSpec((D, tn), lambda i, j: (0, j))],
        out_specs=pl.BlockSpec((tm, tn), lambda i, j: (i, j)),
        out_shape=jax.ShapeDtypeStruct((T, D), F32),
        scratch_shapes=[pltpu.VMEM((tm, D), BF16)],
        compiler_params=_cparams("parallel", "arbitrary"),
        name="out_proj",
    )(oa, ob, ga, gb, x2, w_o)


def _router_kernel(*refs, n_exp, aliased):
    if aliased:
        h_ref, g_ref, w_ref, b_ref, _, u_ref, idx_ref, gate_ref = refs
    else:
        h_ref, g_ref, w_ref, b_ref, u_ref, idx_ref, gate_ref = refs
    u = _rms(h_ref[...], g_ref[...])
    u_ref[...] = u
    logits = jnp.dot(u, w_ref[...], precision=lax.Precision.HIGHEST, preferred_element_type=F32) + b_ref[...]
    tm = logits.shape[0]
    lane_e = lax.broadcasted_iota(jnp.int32, (tm, n_exp), 1)
    lane_o = lax.broadcasted_iota(jnp.int32, (tm, LANE), 1)
    idx_out = jnp.zeros((tm, LANE), jnp.int32)
    val_out = jnp.zeros((tm, LANE), F32)
    vals = []
    for k in range(TOP_K):
        m = jnp.max(logits, axis=-1, keepdims=True)
        i = jnp.min(jnp.where(logits == m, lane_e, n_exp), axis=-1, keepdims=True)
        logits = jnp.where(lane_e == i, -jnp.inf, logits)
        vals.append(m)
        idx_out = jnp.where(lane_o == k, i, idx_out)
    es = [jnp.exp(v - vals[0]) for v in vals]
    den = es[0]
    for e in es[1:]:
        den = den + e
    for k in range(TOP_K):
        val_out = jnp.where(lane_o == k, es[k] / den, val_out)
    idx_ref[...] = idx_out
    gate_ref[...] = val_out


def _router(h2, g, w_r, b_r, u_all, *, tm, row_off, t_all):
    T, D = h2.shape
    n_exp = w_r.shape[1]
    row = lambda i: (i, 0)
    fix = lambda i: (0, 0)
    off = row_off // tm
    in_specs = [pl.BlockSpec((tm, D), row), pl.BlockSpec((1, D), fix), pl.BlockSpec((D, n_exp), fix),
                pl.BlockSpec((1, n_exp), fix)]
    args = [h2, g, w_r, b_r]
    aliases = {}
    if u_all is not None:
        in_specs.append(pl.BlockSpec(memory_space=pl.ANY))
        args.append(u_all)
        aliases = {4: 0}
    return pl.pallas_call(
        functools.partial(_router_kernel, n_exp=n_exp, aliased=u_all is not None),
        grid=(T // tm,),
        in_specs=in_specs,
        out_specs=[pl.BlockSpec((tm, D), lambda i: (i + off, 0)), pl.BlockSpec((tm, LANE), row),
                   pl.BlockSpec((tm, LANE), row)],
        out_shape=[jax.ShapeDtypeStruct((t_all, D), F32), jax.ShapeDtypeStruct((T, LANE), jnp.int32),
                   jax.ShapeDtypeStruct((T, LANE), F32)],
        input_output_aliases=aliases,
        compiler_params=_cparams("arbitrary"),
        name="router",
    )(*args)


def _row_copy(src_ref, dst_ref, src_row, dst_row, sem):
    return pltpu.make_async_copy(src_ref.at[pl.ds(src_row, 1)], dst_ref.at[pl.ds(dst_row, 1)], sem)


def _gather_kernel(idx_hbm, src_hbm, o_ref, idx_smem, sem_idx, sem):
    i = pl.program_id(0)
    R = o_ref.shape[0]
    cp = pltpu.make_async_copy(idx_hbm.at[i], idx_smem, sem_idx)
    cp.start()
    cp.wait()

    def issue(r, c):
        _row_copy(src_hbm, o_ref, idx_smem[r], r, sem).start()
        return c

    lax.fori_loop(0, R, issue, 0)

    def drain(r, c):
        _row_copy(src_hbm, o_ref, 0, r, sem).wait()
        return c

    lax.fori_loop(0, R, drain, 0)


def _gather_rows(idx2, src, *, R):
    n = idx2.shape[0]
    D = src.shape[1]
    return pl.pallas_call(
        _gather_kernel,
        grid=(n,),
        in_specs=[pl.BlockSpec(memory_space=pl.ANY), pl.BlockSpec(memory_space=pl.ANY)],
        out_specs=pl.BlockSpec((R, D), lambda i: (i, 0)),
        out_shape=jax.ShapeDtypeStruct((n * R, D), src.dtype),
        scratch_shapes=[pltpu.SMEM((R,), jnp.int32), pltpu.SemaphoreType.DMA, pltpu.SemaphoreType.DMA],
        compiler_params=_cparams("arbitrary"),
        name="moe_dispatch_gather",
    )(idx2, src)


def _expert_kernel(te_ref, tv_ref, x_ref, wg_ref, wu_ref, bg_ref, bu_ref, wd_ref, bd_ref, y_ref, xb_ref, acc_ref):
    i = pl.program_id(0)
    c = pl.program_id(1)

    @pl.when(c == 0)
    def _():
        acc_ref[...] = jnp.zeros_like(acc_ref)
        xb_ref[...] = x_ref[...].astype(BF16)

    @pl.when(tv_ref[i] > 0)
    def _():
        xb = xb_ref[...]
        gate = jnp.minimum(_dot(xb, wg_ref[0]) + bg_ref[0], SWIGLU_LIMIT)
        up = jnp.clip(_dot(xb, wu_ref[0]) + bu_ref[0], -SWIGLU_LIMIT, SWIGLU_LIMIT)
        act = gate * _sigmoid(SWIGLU_ALPHA * gate) * (up + 1.0)
        acc_ref[...] += _dot(act.astype(BF16), wd_ref[0])

    @pl.when(c == pl.num_programs(1) - 1)
    def _():
        y_ref[...] = acc_ref[...] + bd_ref[0]


def _experts(tile_e, tile_v, xs, w_gu, b_gu, w_dn, b_dn, *, tm, fc):
    P, D = xs.shape
    E, _, ff2 = w_gu.shape
    ff = ff2 // 2
    nc = ff // fc
    n_tiles = P // tm

    def cc(i, c, tv):
        return jnp.where(tv[i] > 0, c, nc - 1)

    grid_spec = pltpu.PrefetchScalarGridSpec(
        num_scalar_prefetch=2,
        grid=(n_tiles, nc),
        in_specs=[
            pl.BlockSpec((tm, D), lambda i, c, te, tv: (i, 0)),
            pl.BlockSpec((1, D, fc), lambda i, c, te, tv: (te[i], 0, cc(i, c, tv))),
            pl.BlockSpec((1, D, fc), lambda i, c, te, tv: (te[i], 0, nc + cc(i, c, tv))),
            pl.BlockSpec((1, 1, fc), lambda i, c, te, tv: (te[i], 0, cc(i, c, tv))),
            pl.BlockSpec((1, 1, fc), lambda i, c, te, tv: (te[i], 0, nc + cc(i, c, tv))),
            pl.BlockSpec((1, fc, D), lambda i, c, te, tv: (te[i], cc(i, c, tv), 0)),
            pl.BlockSpec((1, 1, D), lambda i, c, te, tv: (te[i], 0, 0)),
        ],
        out_specs=pl.BlockSpec((tm, D), lambda i, c, te, tv: (i, 0)),
        scratch_shapes=[pltpu.VMEM((tm, D), BF16), pltpu.VMEM((tm, D), F32)],
    )
    return pl.pallas_call(
        _expert_kernel,
        grid_spec=grid_spec,
        out_shape=jax.ShapeDtypeStruct((P, D), F32),
        compiler_params=_cparams("arbitrary", "arbitrary"),
        name="moe_experts",
    )(tile_e, tile_v, xs, w_gu, w_gu, b_gu, b_gu, w_dn, b_dn)


def _combine_kernel(dest_hbm, gate_ref, h_ref, ys_hbm, g_ref, y_ref, buf_ref, idx_smem, sem_idx, sem):
    i = pl.program_id(0)
    R = h_ref.shape[0]
    cp = pltpu.make_async_copy(dest_hbm.at[i], idx_smem, sem_idx)
    cp.start()
    cp.wait()

    def issue(r, c):
        for k in range(TOP_K):
            _row_copy(ys_hbm, buf_ref.at[k], idx_smem[r * TOP_K + k], r, sem).start()
        return c

    lax.fori_loop(0, R, issue, 0)

    def drain(r, c):
        for k in range(TOP_K):
            _row_copy(ys_hbm, buf_ref.at[k], 0, r, sem).wait()
        return c

    lax.fori_loop(0, R, drain, 0)
    h = h_ref[...]
    gate = gate_ref[...]
    for k in range(TOP_K):
        h = h + gate[:, k:k + 1] * buf_ref[k]
    y_ref[...] = _rms(h, g_ref[...])


def _combine(dest2, gate, h2, ys, g, *, R):
    T, D = h2.shape
    row = lambda i: (i, 0)
    return pl.pallas_call(
        _combine_kernel,
        grid=(T // R,),
        in_specs=[pl.BlockSpec(memory_space=pl.ANY), pl.BlockSpec((R, LANE), row), pl.BlockSpec((R, D), row),
                  pl.BlockSpec(memory_space=pl.ANY), pl.BlockSpec((1, D), lambda i: (0, 0))],
        out_specs=pl.BlockSpec((R, D), row),
        out_shape=jax.ShapeDtypeStruct((T, D), F32),
        scratch_shapes=[pltpu.VMEM((TOP_K, R, D), F32), pltpu.SMEM((R * TOP_K,), jnp.int32),
                        pltpu.SemaphoreType.DMA, pltpu.SemaphoreType.DMA],
        compiler_params=_cparams("arbitrary"),
        name="moe_combine",
    )(dest2, gate, h2, ys, g)


def _rope_tables(pos, rope, rows):
    half = rope // 2
    inv = ROPE_BASE ** (-jnp.arange(half, dtype=F32) / half)
    ang = pos.astype(F32)[:, None] * inv[None, :]
    cos, sin = jnp.cos(ang), jnp.sin(ang)
    z = jnp.zeros((pos.shape[0], LANE - rope), F32)
    cos_t = jnp.concatenate([cos, cos, z], axis=1)
    sin_t = jnp.concatenate([-sin, sin, z], axis=1)
    rep = max(1, rows // pos.shape[0])
    return jnp.tile(cos_t, (rep, 1)), jnp.tile(sin_t, (rep, 1))


def _swap_halves(w):
    half = w.shape[-1] // 2
    return jnp.concatenate([w[..., half:], w[..., :half]], axis=-1)


def _mixer(x, pos, past, wts, *, tm):
    B, S, D = x.shape
    T = B * S
    heads, ql, kl, rope = wts["heads"], wts["ql"], wts["kl"], wts["rope"]
    x2 = x.reshape(T, D)
    cos_t, sin_t = _rope_tables(pos, rope, tm)
    u, cqn, lat, kr, krb = _small_proj(x2, wts["norm_mix_g"], wts["w_small"], wts["q_norm_g"], wts["kv_norm_g"],
                                       cos_t, sin_t, tm=tm, ql=ql, kl=kl, rope=rope)
    hq, hf, hi, hg, ga, gb = _big_proj(u, wts["w_big"], (BF16, F32, BF16, BF16, BF16, BF16), tm=tm, tn=256)
    qf = _q_proj(cqn, wts["wq_raw"], wts["wq_sw"], cos_t, sin_t, tm=tm, heads=heads)
    F = hq.shape[1]
    r3 = lambda a: a.reshape(B, S, a.shape[1])
    if past is None:
        kf, v = _kv_proj(lat, krb, wts["w_uk"], wts["w_uv"], tm=tm, heads=heads)
        o_a = _attention(r3(qf), r3(kf), r3(v), heads=heads, tq=min(256, S))
        o_b, st = _hgrn(r3(hq), r3(hf), r3(hi), r3(hg), wts["lb"], wts["hgrn_g"], None, heads=wts["hg_heads"],
                        L=CHUNK)
    else:
        cache_lat, cache_kr, state = past
        o_a = _attention_sample(r3(qf), cache_lat, cache_kr, r3(lat), r3(kr), wts["w_uk_t"], wts["w_uv_h"],
                                heads=heads)
        o_b, st = _hgrn(r3(hq), r3(hf), r3(hi), r3(hg), wts["lb"], wts["hgrn_g"], state, heads=wts["hg_heads"], L=S)
    h2 = _out_proj(o_a.reshape(T, D), o_b.reshape(T, F), ga, gb, x2, wts["w_o"], tm=tm, tn=512)
    return h2, lat.reshape(B, S, kl), kr.reshape(B, S, rope), st


def kernel(x_prompt, x_sample, cache_mla_latent, cache_mla_krope, state_hgrn, norm_mix_g, w_in, q_norm_g, w_uq,
           kv_norm_g, w_uk, w_uv, hgrn_lb_raw, hgrn_norm_g, w_o, norm_ffn_g, w_router, b_router, w_gate_up,
           b_gate_up, w_down, b_down, norm_final_g):
    depth = w_in.shape[0]
    assert depth == 1, "single-layer kernel"
    B, S, D = x_prompt.shape
    Bd, Sd, _ = x_sample.shape
    past_len = cache_mla_latent.shape[2]
    ql, heads, qk = w_uq.shape[1:]
    kl, _, nope = w_uk.shape[1:]
    vd = w_uv.shape[3]
    rope = qk - nope
    hg_heads, dk, dv = state_hgrn.shape[2:]
    n_exp, _, ff2 = w_gate_up.shape[1:]
    assert nope == LANE and vd == LANE and dk == LANE and dv == LANE and rope <= LANE
    hf_dim = hg_heads * dk
    scale = 1.0 / math.sqrt(nope + rope)
    l = 0

    wi = w_in[l]
    o_cq, o_kv, o_pe, o_hq = 0, ql, ql + kl, ql + kl + rope
    k_pe = wi[:, o_pe:o_pe + rope]
    zpad = jnp.zeros((D, LANE - rope), F32)
    w_small = jnp.concatenate([wi[:, o_cq:o_pe], k_pe, zpad, _swap_halves(k_pe), zpad], axis=1).astype(BF16)
    segs = [hf_dim, hf_dim, D, D, D, D]
    offs = [o_hq]
    for s_ in segs[:-1]:
        offs.append(offs[-1] + s_)
    w_big = [wi[:, o:o + s_].astype(BF16) for o, s_ in zip(offs, segs)]
    wq = w_uq[l] * scale
    zq = jnp.zeros((ql, heads, LANE - rope), F32)
    wq_raw = jnp.concatenate([wq[..., :nope], wq[..., nope:], zq], axis=-1).reshape(ql, heads * 2 * LANE).astype(BF16)
    wq_sw = jnp.concatenate([_swap_halves(wq[..., nope:]), zq], axis=-1).reshape(ql, heads * LANE).astype(BF16)
    lb_all = jnp.cumsum(jax.nn.softmax(hgrn_lb_raw.astype(F32), axis=0), axis=0)
    wts = dict(
        heads=heads, ql=ql, kl=kl, rope=rope, hg_heads=hg_heads,
        norm_mix_g=norm_mix_g[l][None], w_small=w_small, q_norm_g=q_norm_g[l][None], kv_norm_g=kv_norm_g[l][None],
        w_big=w_big, wq_raw=wq_raw, wq_sw=wq_sw,
        w_uk=w_uk[l].reshape(kl, heads * nope).astype(BF16), w_uv=w_uv[l].reshape(kl, heads * vd).astype(BF16),
        w_uk_t=jnp.transpose(w_uk[l], (1, 2, 0)).astype(BF16), w_uv_h=jnp.transpose(w_uv[l], (1, 0, 2)).astype(BF16),
        lb=lb_all[l][None], hgrn_g=hgrn_norm_g[l][None], w_o=w_o[l].astype(BF16),
    )

    Tp, Ts = B * S, Bd * Sd
    tm = 512
    assert S % tm == 0 or tm % S == 0
    assert Tp % tm == 0 and Ts % tm == 0 and tm % Sd == 0
    h2p, lat_p, kr_p, st_p = _mixer(x_prompt, jnp.arange(S), None, wts, tm=tm)
    past = (cache_mla_latent[l], cache_mla_krope[l], state_hgrn[l])
    h2s, lat_s, kr_s, st_s = _mixer(x_sample, past_len + jnp.arange(Sd), past, wts, tm=tm)

    T_all = Tp + Ts
    gffn = norm_ffn_g[l][None]
    u_all, idx_p, gate_p = _router(h2p, gffn, w_router[l], b_router[l][None], None, tm=tm, row_off=0, t_all=T_all)
    u_all, idx_s, gate_s = _router(h2s, gffn, w_router[l], b_router[l][None], u_all, tm=tm, row_off=Tp, t_all=T_all)

    te_rows = 512
    idx = jnp.concatenate([idx_p[:, :TOP_K], idx_s[:, :TOP_K]], axis=0)
    A = T_all * TOP_K
    flat_e = idx.reshape(A)
    onehot = (flat_e[:, None] == jnp.arange(n_exp, dtype=jnp.int32)[None, :]).astype(jnp.int32)
    csum = jnp.cumsum(onehot, axis=0)
    counts = csum[-1]
    rank = jnp.sum(csum * onehot, axis=1) - 1
    padded = (counts + te_rows - 1) // te_rows * te_rows
    pad_end = jnp.cumsum(padded)
    pad_start = pad_end - padded
    dest = pad_start[flat_e] + rank
    n_tiles = -(-A // te_rows) + n_exp
    P = n_tiles * te_rows
    row_tok = jnp.zeros((P,), jnp.int32).at[dest].set(jnp.arange(A, dtype=jnp.int32) // TOP_K)
    tile_start = jnp.arange(n_tiles, dtype=jnp.int32) * te_rows
    tile_e = jnp.minimum(jnp.searchsorted(pad_end, tile_start, side="right"), n_exp - 1).astype(jnp.int32)
    tile_v = (tile_start < pad_end[-1]).astype(jnp.int32)
    last_e = tile_e[jnp.maximum(jnp.sum(tile_v) - 1, 0)]
    tile_e = jnp.where(tile_v > 0, tile_e, last_e)

    gR = 256
    xs = _gather_rows(row_tok.reshape(P // gR, gR), u_all, R=gR)
    ys = _experts(tile_e, tile_v, xs, w_gate_up[l].astype(BF16), b_gate_up[l][:, None, :],
                  w_down[l].astype(BF16), b_down[l][:, None, :], tm=te_rows, fc=512)

    cR = 128
    gfin = norm_final_g[None]
    dest2 = dest.reshape(T_all // cR, cR * TOP_K)
    y_p = _combine(dest2[:Tp // cR], gate_p, h2p, ys, gfin, R=cR)
    y_s = _combine(dest2[Tp // cR:], gate_s, h2s, ys, gfin, R=cR)

    return (y_p.reshape(B, S, D), y_s.reshape(Bd, Sd, D), lat_p[None], kr_p[None], st_p[None],
            lat_s[None], kr_s[None], st_s[None])
```

```python
import functools
import math

import jax
import jax.numpy as jnp
from jax import lax
from jax.experimental import pallas as pl
from jax.experimental.pallas import tpu as pltpu

F32 = jnp.float32
BF16 = jnp.bfloat16

CHUNK = 64
NORM_EPS = 1e-6
ROPE_BASE = 10000.0
NEG_INF = -1e30
TOP_K = 4
SWIGLU_ALPHA = 1.702
SWIGLU_LIMIT = 7.0

LANE = 128
VMEM_LIMIT = 56 * 1024 * 1024


def _cparams(*sem):
    return pltpu.CompilerParams(dimension_semantics=sem, vmem_limit_bytes=VMEM_LIMIT)


def _dot(a, b):
    return jnp.dot(a, b, preferred_element_type=F32)


def _dot_nt(a, b):
    return lax.dot_general(a, b, (((1,), (1,)), ((), ())), preferred_element_type=F32)


def _rms(x, g):
    return x * lax.rsqrt(jnp.mean(x * x, axis=-1, keepdims=True) + NORM_EPS) * g


def _sigmoid(x):
    return 1.0 / (1.0 + jnp.exp(-x))


def _small_proj_kernel(x_ref, g_ref, w_ref, qg_ref, kvg_ref, cos_ref, sin_ref,
                       u_ref, cqn_ref, lat_ref, kr_ref, krb_ref, *, ql, kl, rope):
    u = _rms(x_ref[...], g_ref[...]).astype(BF16)
    u_ref[...] = u
    y = _dot(u, w_ref[...])
    cqn_ref[...] = _rms(y[:, :ql], qg_ref[...]).astype(BF16)
    lat_ref[...] = _rms(y[:, ql:ql + kl], kvg_ref[...])
    o = ql + kl
    kr = y[:, o:o + LANE] * cos_ref[...] + y[:, o + LANE:o + 2 * LANE] * sin_ref[...]
    kr_ref[...] = kr[:, :rope]
    krb_ref[...] = kr.astype(BF16)


def _small_proj(x2, g, w_small, qg, kvg, cos_t, sin_t, *, tm, ql, kl, rope):
    T, D = x2.shape
    n = w_small.shape[1]
    nblk = cos_t.shape[0] // tm
    row = lambda i: (i, 0)
    fix = lambda i: (0, 0)
    tab = lambda i: (i % nblk, 0)
    return pl.pallas_call(
        functools.partial(_small_proj_kernel, ql=ql, kl=kl, rope=rope),
        grid=(T // tm,),
        in_specs=[pl.BlockSpec((tm, D), row), pl.BlockSpec((1, D), fix), pl.BlockSpec((D, n), fix),
                  pl.BlockSpec((1, ql), fix), pl.BlockSpec((1, kl), fix),
                  pl.BlockSpec((tm, LANE), tab), pl.BlockSpec((tm, LANE), tab)],
        out_specs=[pl.BlockSpec((tm, D), row), pl.BlockSpec((tm, ql), row), pl.BlockSpec((tm, kl), row),
                   pl.BlockSpec((tm, rope), row), pl.BlockSpec((tm, LANE), row)],
        out_shape=[jax.ShapeDtypeStruct((T, D), BF16), jax.ShapeDtypeStruct((T, ql), BF16),
                   jax.ShapeDtypeStruct((T, kl), F32), jax.ShapeDtypeStruct((T, rope), F32),
                   jax.ShapeDtypeStruct((T, LANE), BF16)],
        compiler_params=_cparams("parallel"),
        name="small_proj",
    )(x2, g, w_small, qg, kvg, cos_t, sin_t)


def _big_proj_kernel(u_ref, *refs, nseg):
    u = u_ref[...]
    for s in range(nseg):
        refs[nseg + s][...] = _dot(u, refs[s][...]).astype(refs[nseg + s].dtype)


def _big_proj(u, ws, out_dtypes, *, tm, tn):
    T, D = u.shape
    n = ws[0].shape[1]
    nseg = len(ws)
    return pl.pallas_call(
        functools.partial(_big_proj_kernel, nseg=nseg),
        grid=(T // tm, n // tn),
        in_specs=[pl.BlockSpec((tm, D), lambda i, j: (i, 0))]
                 + [pl.BlockSpec((D, tn), lambda i, j: (0, j)) for _ in ws],
        out_specs=[pl.BlockSpec((tm, tn), lambda i, j: (i, j)) for _ in ws],
        out_shape=[jax.ShapeDtypeStruct((T, n), dt) for dt in out_dtypes],
        compiler_params=_cparams("parallel", "arbitrary"),
        name="big_proj",
    )(u, *ws)


def _q_proj_kernel(c_ref, wr_ref, ws_ref, cos_ref, sin_ref, q_ref, *, heads):
    c = c_ref[...]
    raw = _dot(c, wr_ref[...])
    sw = _dot(c, ws_ref[...])
    cos = cos_ref[...]
    sin = sin_ref[...]
    for h in range(heads):
        b = 2 * LANE * h
        q_ref[:, b:b + LANE] = raw[:, b:b + LANE].astype(BF16)
        r = raw[:, b + LANE:b + 2 * LANE] * cos + sw[:, LANE * h:LANE * (h + 1)] * sin
        q_ref[:, b + LANE:b + 2 * LANE] = r.astype(BF16)


def _q_proj(cqn, wq_raw, wq_sw, cos_t, sin_t, *, tm, heads):
    T, ql = cqn.shape
    nblk = cos_t.shape[0] // tm
    row = lambda i: (i, 0)
    fix = lambda i: (0, 0)
    tab = lambda i: (i % nblk, 0)
    n = wq_raw.shape[1]
    return pl.pallas_call(
        functools.partial(_q_proj_kernel, heads=heads),
        grid=(T // tm,),
        in_specs=[pl.BlockSpec((tm, ql), row), pl.BlockSpec(wq_raw.shape, fix), pl.BlockSpec(wq_sw.shape, fix),
                  pl.BlockSpec((tm, LANE), tab), pl.BlockSpec((tm, LANE), tab)],
        out_specs=pl.BlockSpec((tm, n), row),
        out_shape=jax.ShapeDtypeStruct((T, n), BF16),
        compiler_params=_cparams("parallel"),
        name="q_proj",
    )(cqn, wq_raw, wq_sw, cos_t, sin_t)


def _kv_proj_kernel(lat_ref, krb_ref, wk_ref, wv_ref, k_ref, v_ref, *, heads):
    lat = lat_ref[...].astype(BF16)
    kn = _dot(lat, wk_ref[...]).astype(BF16)
    v_ref[...] = _dot(lat, wv_ref[...]).astype(BF16)
    krb = krb_ref[...]
    for h in range(heads):
        b = 2 * LANE * h
        k_ref[:, b:b + LANE] = kn[:, LANE * h:LANE * (h + 1)]
        k_ref[:, b + LANE:b + 2 * LANE] = krb


def _kv_proj(lat, krb, wk, wv, *, tm, heads):
    T, kl = lat.shape
    row = lambda i: (i, 0)
    fix = lambda i: (0, 0)
    return pl.pallas_call(
        functools.partial(_kv_proj_kernel, heads=heads),
        grid=(T // tm,),
        in_specs=[pl.BlockSpec((tm, kl), row), pl.BlockSpec((tm, LANE), row),
                  pl.BlockSpec(wk.shape, fix), pl.BlockSpec(wv.shape, fix)],
        out_specs=[pl.BlockSpec((tm, 2 * LANE * heads), row), pl.BlockSpec((tm, wv.shape[1]), row)],
        out_shape=[jax.ShapeDtypeStruct((T, 2 * LANE * heads), BF16), jax.ShapeDtypeStruct((T, wv.shape[1]), BF16)],
        compiler_params=_cparams("parallel"),
        name="kv_proj",
    )(lat, krb, wk, wv)


def _attn_kernel(q_ref, k_ref, v_ref, o_ref, *, tq, tk, hpb):
    qi = pl.program_id(2)
    kd = 2 * LANE
    vd = v_ref.shape[2] // hpb
    qs = [q_ref[0, :, g * kd:(g + 1) * kd] for g in range(hpb)]

    def update(carry, s, v):
        m, l, acc = carry
        m_new = jnp.maximum(m, jnp.max(s, axis=-1, keepdims=True))
        alpha = jnp.exp(m - m_new)
        p = jnp.exp(s - m_new)
        l = alpha * l + jnp.sum(p, axis=-1, keepdims=True)
        acc = alpha * acc + _dot(p.astype(BF16), v)
        return m_new, l, acc

    def block(carry, start, mask):
        new = []
        for g in range(hpb):
            k = k_ref[0, pl.ds(start, tk), g * kd:(g + 1) * kd]
            v = v_ref[0, pl.ds(start, tk), g * vd:(g + 1) * vd]
            s = _dot_nt(qs[g], k)
            if mask is not None:
                s = jnp.where(mask, s, NEG_INF)
            new.append(update(carry[g], s, v))
        return tuple(new)

    one = (jnp.full((tq, 1), NEG_INF, F32), jnp.zeros((tq, 1), F32), jnp.zeros((tq, vd), F32))
    nfull = qi * (tq // tk)
    carry = lax.fori_loop(0, nfull, lambda j, c: block(c, pl.multiple_of(j * tk, tk), None), (one,) * hpb)
    qc = lax.broadcasted_iota(jnp.int32, (tq, tk), 0) // CHUNK
    kc = lax.broadcasted_iota(jnp.int32, (tq, tk), 1) // CHUNK
    for d in range(tq // tk):
        carry = block(carry, pl.multiple_of((nfull + d) * tk, tk), kc + d * (tk // CHUNK) <= qc)
    for g in range(hpb):
        m, l, acc = carry[g]
        o_ref[0, :, g * vd:(g + 1) * vd] = (acc / l).astype(BF16)


def _attention(qf, kf, v, *, heads, tq, tk, hpb):
    B, S, _ = qf.shape
    vd = v.shape[2] // heads
    return pl.pallas_call(
        functools.partial(_attn_kernel, tq=tq, tk=tk, hpb=hpb),
        grid=(B, heads // hpb, S // tq),
        in_specs=[pl.BlockSpec((1, tq, 2 * LANE * hpb), lambda b, h, i: (b, i, h)),
                  pl.BlockSpec((1, S, 2 * LANE * hpb), lambda b, h, i: (b, 0, h)),
                  pl.BlockSpec((1, S, vd * hpb), lambda b, h, i: (b, 0, h))],
        out_specs=pl.BlockSpec((1, tq, vd * hpb), lambda b, h, i: (b, i, h)),
        out_shape=jax.ShapeDtypeStruct((B, S, heads * vd), BF16),
        compiler_params=_cparams("parallel", "parallel", "arbitrary"),
        name="mla_prompt_attn",
    )(qf, kf, v)


def _attn_sample_kernel(q_ref, cl_ref, ck_ref, ln_ref, kn_ref, wuk_ref, wuv_ref, o_ref,
                        ql_ref, qr_ref, ol_ref, *, heads, rope):
    sd = q_ref.shape[1]
    for h in range(heads):
        b = 2 * LANE * h
        ql_ref[h * sd:(h + 1) * sd, :] = _dot(q_ref[0, :, b:b + LANE], wuk_ref[h]).astype(BF16)
        qr_ref[h * sd:(h + 1) * sd, :] = q_ref[0, :, b + LANE:b + 2 * LANE]
    ql = ql_ref[...]
    qr = qr_ref[...][:, :rope]
    cl = cl_ref[0].astype(BF16)
    ck = ck_ref[0].astype(BF16)
    ln = ln_ref[0].astype(BF16)
    kn = kn_ref[0].astype(BF16)
    s_past = _dot_nt(ql, cl) + _dot_nt(qr, ck)
    s_new = _dot_nt(ql, ln) + _dot_nt(qr, kn)
    m = jnp.maximum(jnp.max(s_past, axis=-1, keepdims=True), jnp.max(s_new, axis=-1, keepdims=True))
    p_past = jnp.exp(s_past - m)
    p_new = jnp.exp(s_new - m)
    l = jnp.sum(p_past, axis=-1, keepdims=True) + jnp.sum(p_new, axis=-1, keepdims=True)
    o_lat = (_dot(p_past.astype(BF16), cl) + _dot(p_new.astype(BF16), ln)) / l
    ol_ref[...] = o_lat.astype(BF16)
    vd = wuv_ref.shape[2]
    for h in range(heads):
        o_ref[0, :, h * vd:(h + 1) * vd] = _dot(ol_ref[h * sd:(h + 1) * sd, :], wuv_ref[h]).astype(BF16)


def _attention_sample(qf, cache_lat, cache_kr, lat_new, kr_new, wuk_t, wuv, *, heads):
    B, sd, _ = qf.shape
    P, kl = cache_lat.shape[1:]
    rope = cache_kr.shape[2]
    vd = wuv.shape[2]
    b3 = lambda b: (b, 0, 0)
    fix = lambda b: (0, 0, 0)
    return pl.pallas_call(
        functools.partial(_attn_sample_kernel, heads=heads, rope=rope),
        grid=(B,),
        in_specs=[pl.BlockSpec((1, sd, qf.shape[2]), b3), pl.BlockSpec((1, P, kl), b3), pl.BlockSpec((1, P, rope), b3),
                  pl.BlockSpec((1, sd, kl), b3), pl.BlockSpec((1, sd, rope), b3),
                  pl.BlockSpec(wuk_t.shape, fix), pl.BlockSpec(wuv.shape, fix)],
        out_specs=pl.BlockSpec((1, sd, heads * vd), b3),
        out_shape=jax.ShapeDtypeStruct((B, sd, heads * vd), BF16),
        scratch_shapes=[pltpu.VMEM((heads * sd, kl), BF16), pltpu.VMEM((heads * sd, LANE), BF16),
                        pltpu.VMEM((heads * sd, kl), BF16)],
        compiler_params=_cparams("parallel"),
        name="mla_sample_attn",
    )(qf, cache_lat, cache_kr, lat_new, kr_new, wuk_t, wuv)


def _hgrn_kernel(*refs, heads, has_state):
    if has_state:
        hq_ref, hf_ref, hi_ref, hg_ref, lb_ref, g_ref, s0_ref, o_ref, sn_ref, st_ref = refs
    else:
        hq_ref, hf_ref, hi_ref, hg_ref, lb_ref, g_ref, o_ref, sn_ref, st_ref = refs
    c = pl.program_id(1)
    L = hq_ref.shape[1]
    dk = st_ref.shape[2]

    @pl.when(c == 0)
    def _():
        if has_state:
            for h in range(heads):
                st_ref[h] = s0_ref[0, h].T
        else:
            st_ref[...] = jnp.zeros_like(st_ref)

    row = lax.broadcasted_iota(jnp.int32, (L, L), 0)
    col = lax.broadcasted_iota(jnp.int32, (L, L), 1)
    causal = col <= row
    tri = jnp.where(causal, 1.0, 0.0).astype(BF16)
    lb = lb_ref[...]
    f = lb + (1.0 - lb) * _sigmoid(hf_ref[0])
    kk = 1.0 - f
    logf = jnp.log(f)
    hi = logf.astype(BF16)
    r1 = logf - hi.astype(F32)
    mid = r1.astype(BF16)
    lo = (r1 - mid.astype(F32)).astype(BF16)
    cum = _dot(tri, hi) + _dot(tri, mid) + _dot(tri, lo)
    tot = cum[L - 1:L, :]
    q_dec = (hq_ref[0].astype(F32) * jnp.exp(cum)).astype(BF16)
    k_dec = (kk * jnp.exp(-cum)).astype(BF16)
    k2 = (kk * jnp.exp(tot - cum)).astype(BF16)
    decay = jnp.exp(tot)
    vb = hi_ref[0]
    v32 = vb.astype(F32)
    sls = [slice(h * dk, (h + 1) * dk) for h in range(heads)]
    atts = [jnp.where(causal, _dot_nt(q_dec[:, sl], k_dec[:, sl]), 0.0).astype(BF16) for sl in sls]
    sts = [st_ref[h] for h in range(heads)]
    outs = [_dot_nt(q_dec[:, sl], sts[h].astype(BF16)) + _dot(atts[h], vb[:, sl]) for h, sl in enumerate(sls)]
    for h, sl in enumerate(sls):
        st_ref[h] = decay[:, sl] * sts[h] + _dot(v32[:, sl].T.astype(BF16), k2[:, sl])
    g = g_ref[...]
    for h, sl in enumerate(sls):
        o = outs[h]
        on = o * lax.rsqrt(jnp.mean(o * o, axis=-1, keepdims=True) + NORM_EPS) * g
        hg = hg_ref[0, :, sl].astype(F32)
        o_ref[0, :, sl] = (on * (hg * _sigmoid(hg))).astype(BF16)

    @pl.when(c == pl.num_programs(1) - 1)
    def _():
        for h in range(heads):
            sn_ref[0, h] = st_ref[h].T


def _hgrn(hq, hf, hi, hg, lb, g, s0, *, heads, L):
    B, S, F = hq.shape
    dk = F // heads
    blk = lambda b, c: (b, c, 0)
    fix2 = lambda b, c: (0, 0)
    st4 = lambda b, c: (b, 0, 0, 0)
    in_specs = [pl.BlockSpec((1, L, F), blk)] * 4 + [pl.BlockSpec((1, F), fix2), pl.BlockSpec((1, dk), fix2)]
    args = [hq, hf, hi, hg, lb, g]
    if s0 is not None:
        in_specs.append(pl.BlockSpec((1, heads, dk, dk), st4))
        args.append(s0)
    return pl.pallas_call(
        functools.partial(_hgrn_kernel, heads=heads, has_state=s0 is not None),
        grid=(B, S // L),
        in_specs=in_specs,
        out_specs=[pl.BlockSpec((1, L, F), blk), pl.BlockSpec((1, heads, dk, dk), st4)],
        out_shape=[jax.ShapeDtypeStruct((B, S, F), BF16), jax.ShapeDtypeStruct((B, heads, dk, dk), F32)],
        scratch_shapes=[pltpu.VMEM((heads, dk, dk), F32)],
        compiler_params=_cparams("parallel", "arbitrary"),
        name="hgrn",
    )(*args)


def _out_proj_kernel(oa_ref, ob_ref, ga_ref, gb_ref, x_ref, w_ref, o_ref, *, kc):
    acc = x_ref[...]
    for k in range(w_ref.shape[0] // kc):
        sl = slice(k * kc, (k + 1) * kc)
        ga = _sigmoid(ga_ref[:, sl].astype(F32))
        gb = _sigmoid(gb_ref[:, sl].astype(F32))
        m = (ga * oa_ref[:, sl].astype(F32) + gb * ob_ref[:, sl].astype(F32)).astype(BF16)
        acc = acc + _dot(m, w_ref[sl, :])
    o_ref[...] = acc


def _out_proj(oa, ob, ga, gb, x2, w_o, *, tm):
    T, D = x2.shape
    row = lambda i: (i, 0)
    return pl.pallas_call(
        functools.partial(_out_proj_kernel, kc=512),
        grid=(T // tm,),
        in_specs=[pl.BlockSpec((tm, D), row)] * 5 + [pl.BlockSpec((D, D), lambda i: (0, 0))],
        out_specs=pl.BlockSpec((tm, D), row),
        out_shape=jax.ShapeDtypeStruct((T, D), F32),
        compiler_params=_cparams("parallel"),
        name="out_proj",
    )(oa, ob, ga, gb, x2, w_o)


def _router_kernel(hp_ref, hs_ref, g_ref, w_ref, b_ref, u_ref, idx_ref, gate_ref, *, n_exp, n_p):
    h = jnp.where(pl.program_id(0) < n_p, hp_ref[...], hs_ref[...])
    u = _rms(h, g_ref[...])
    u_ref[...] = u
    logits = jnp.dot(u, w_ref[...], precision=lax.Precision.HIGHEST, preferred_element_type=F32) + b_ref[...]
    tm = logits.shape[0]
    lane_e = lax.broadcasted_iota(jnp.int32, (tm, n_exp), 1)
    lane_o = lax.broadcasted_iota(jnp.int32, (tm, LANE), 1)
    idx_out = jnp.zeros((tm, LANE), jnp.int32)
    val_out = jnp.zeros((tm, LANE), F32)
    vals = []
    for k in range(TOP_K):
        m = jnp.max(logits, axis=-1, keepdims=True)
        i = jnp.min(jnp.where(logits == m, lane_e, n_exp), axis=-1, keepdims=True)
        logits = jnp.where(lane_e == i, -jnp.inf, logits)
        vals.append(m)
        idx_out = jnp.where(lane_o == k, i, idx_out)
    es = [jnp.exp(v - vals[0]) for v in vals]
    den = es[0]
    for e in es[1:]:
        den = den + e
    for k in range(TOP_K):
        val_out = jnp.where(lane_o == k, es[k] / den, val_out)
    idx_ref[...] = idx_out
    gate_ref[...] = val_out


def _two_group_specs(shape, n_p):
    return [pl.BlockSpec(shape, lambda i: (jnp.minimum(i, n_p - 1), 0)),
            pl.BlockSpec(shape, lambda i: (jnp.maximum(i - n_p, 0), 0))]


def _router(h2p, h2s, g, w_r, b_r, *, tm):
    (Tp, D), Ts = h2p.shape, h2s.shape[0]
    T = Tp + Ts
    n_exp = w_r.shape[1]
    row = lambda i: (i, 0)
    fix = lambda i: (0, 0)
    return pl.pallas_call(
        functools.partial(_router_kernel, n_exp=n_exp, n_p=Tp // tm),
        grid=(T // tm,),
        in_specs=_two_group_specs((tm, D), Tp // tm)
                 + [pl.BlockSpec((1, D), fix), pl.BlockSpec((D, n_exp), fix), pl.BlockSpec((1, n_exp), fix)],
        out_specs=[pl.BlockSpec((tm, D), row), pl.BlockSpec((tm, LANE), row), pl.BlockSpec((tm, LANE), row)],
        out_shape=[jax.ShapeDtypeStruct((T, D), F32), jax.ShapeDtypeStruct((T, LANE), jnp.int32),
                   jax.ShapeDtypeStruct((T, LANE), F32)],
        compiler_params=_cparams("arbitrary"),
        name="router",
    )(h2p, h2s, g, w_r, b_r)


def _row_copy(src_ref, dst_ref, src_row, dst_row, sem):
    return pltpu.make_async_copy(src_ref.at[pl.ds(src_row, 1)], dst_ref.at[pl.ds(dst_row, 1)], sem)


def _dispatch_kernel(zs_ref, ze_ref, dest_hbm, u_hbm, xs_hbm, idx_smem, zrow_ref, sem_idx, sem, sem_z, *, R, n_exp):
    i = pl.program_id(0)
    n = pl.num_programs(0)
    slot = i % 2

    def idx_copy(step, s):
        return pltpu.make_async_copy(dest_hbm.at[step], idx_smem.at[s], sem_idx.at[s])

    def start_rows(s, base):
        def body(r, c):
            for k in range(TOP_K):
                _row_copy(u_hbm, xs_hbm, base + r, idx_smem[s, r * TOP_K + k], sem.at[s]).start()
            return c
        lax.fori_loop(0, R, body, 0, unroll=8)

    def wait_rows(s):
        def body(r, c):
            for k in range(TOP_K):
                _row_copy(u_hbm, xs_hbm, 0, 0, sem.at[s]).wait()
            return c
        lax.fori_loop(0, R, body, 0, unroll=8)

    def zero_rows(start):
        def per_expert(e, c):
            def body(r, c2):
                cp = _row_copy(zrow_ref, xs_hbm, 0, r, sem_z)
                cp.start() if start else cp.wait()
                return c2
            return lax.fori_loop(zs_ref[e], ze_ref[e], body, c)
        lax.fori_loop(0, n_exp, per_expert, 0)

    @pl.when(i == 0)
    def _():
        zrow_ref[...] = jnp.zeros_like(zrow_ref)
        idx_copy(0, 0).start()

    idx_copy(i, slot).wait()

    @pl.when(i + 1 < n)
    def _():
        idx_copy(i + 1, 1 - slot).start()

    start_rows(slot, i * R)

    @pl.when(i > 0)
    def _():
        wait_rows(1 - slot)

    @pl.when(i == n - 1)
    def _():
        zero_rows(True)
        wait_rows(slot)
        zero_rows(False)


def _dispatch(zs, ze, dest2, u, *, P, R):
    n = dest2.shape[0]
    D = u.shape[1]
    grid_spec = pltpu.PrefetchScalarGridSpec(
        num_scalar_prefetch=2,
        grid=(n,),
        in_specs=[pl.BlockSpec(memory_space=pl.ANY), pl.BlockSpec(memory_space=pl.ANY)],
        out_specs=pl.BlockSpec(memory_space=pl.ANY),
        scratch_shapes=[pltpu.SMEM((2, R * TOP_K), jnp.int32), pltpu.VMEM((8, D), u.dtype),
                        pltpu.SemaphoreType.DMA((2,)), pltpu.SemaphoreType.DMA((2,)), pltpu.SemaphoreType.DMA],
    )
    return pl.pallas_call(
        functools.partial(_dispatch_kernel, R=R, n_exp=zs.shape[0]),
        grid_spec=grid_spec,
        out_shape=jax.ShapeDtypeStruct((P, D), u.dtype),
        compiler_params=_cparams("arbitrary"),
        name="moe_dispatch",
    )(zs, ze, dest2, u)


def _expert_kernel(te_ref, tv_ref, x_ref, wg_ref, wu_ref, bg_ref, bu_ref, wd_ref, bd_ref, y_ref, xb_ref, acc_ref):
    i = pl.program_id(0)
    c = pl.program_id(1)

    @pl.when(c == 0)
    def _():
        acc_ref[...] = jnp.zeros_like(acc_ref)
        xb_ref[...] = x_ref[...].astype(BF16)

    @pl.when(tv_ref[i] > 0)
    def _():
        xb = xb_ref[...]
        gate = jnp.minimum(_dot(xb, wg_ref[0]) + bg_ref[0], SWIGLU_LIMIT)
        up = jnp.clip(_dot(xb, wu_ref[0]) + bu_ref[0], -SWIGLU_LIMIT, SWIGLU_LIMIT)
        act = gate * _sigmoid(SWIGLU_ALPHA * gate) * (up + 1.0)
        acc_ref[...] += _dot(act.astype(BF16), wd_ref[0])

    @pl.when(c == pl.num_programs(1) - 1)
    def _():
        y_ref[...] = acc_ref[...] + bd_ref[0]


def _experts(tile_e, tile_v, xs, w_gu, b_gu, w_dn, b_dn, *, tm, fc):
    P, D = xs.shape
    E, _, ff2 = w_gu.shape
    ff = ff2 // 2
    nc = ff // fc
    n_tiles = P // tm

    def cc(i, c, tv):
        return jnp.where(tv[i] > 0, c, nc - 1)

    grid_spec = pltpu.PrefetchScalarGridSpec(
        num_scalar_prefetch=2,
        grid=(n_tiles, nc),
        in_specs=[
            pl.BlockSpec((tm, D), lambda i, c, te, tv: (jnp.where(tv[i] > 0, i, 0), 0)),
            pl.BlockSpec((1, D, fc), lambda i, c, te, tv: (te[i], 0, cc(i, c, tv))),
            pl.BlockSpec((1, D, fc), lambda i, c, te, tv: (te[i], 0, nc + cc(i, c, tv))),
            pl.BlockSpec((1, 1, fc), lambda i, c, te, tv: (te[i], 0, cc(i, c, tv))),
            pl.BlockSpec((1, 1, fc), lambda i, c, te, tv: (te[i], 0, nc + cc(i, c, tv))),
            pl.BlockSpec((1, fc, D), lambda i, c, te, tv: (te[i], cc(i, c, tv), 0)),
            pl.BlockSpec((1, 1, D), lambda i, c, te, tv: (te[i], 0, 0)),
        ],
        out_specs=pl.BlockSpec((tm, D), lambda i, c, te, tv: (i, 0)),
        scratch_shapes=[pltpu.VMEM((tm, D), BF16), pltpu.VMEM((tm, D), F32)],
    )
    return pl.pallas_call(
        _expert_kernel,
        grid_spec=grid_spec,
        out_shape=jax.ShapeDtypeStruct((P, D), F32),
        compiler_params=_cparams("arbitrary", "arbitrary"),
        name="moe_experts",
    )(tile_e, tile_v, xs, w_gu, w_gu, b_gu, b_gu, w_dn, b_dn)


def _combine_kernel(dest_hbm, gate_ref, hp_ref, hs_ref, ys_hbm, g_ref, yp_ref, ysm_ref, buf_ref, idx_smem, sem_idx,
                    sem, *, n_p):
    i = pl.program_id(0)
    n = pl.num_programs(0)
    R = hp_ref.shape[0]
    slot = i % 2

    def idx_copy(step, s):
        return pltpu.make_async_copy(dest_hbm.at[step], idx_smem.at[s], sem_idx.at[s])

    def start_rows(s):
        def body(r, c):
            for k in range(TOP_K):
                _row_copy(ys_hbm, buf_ref.at[s, k], idx_smem[s, r * TOP_K + k], r, sem.at[s]).start()
            return c
        lax.fori_loop(0, R, body, 0, unroll=8)

    def wait_rows(s):
        def body(r, c):
            for k in range(TOP_K):
                _row_copy(ys_hbm, buf_ref.at[s, k], 0, r, sem.at[s]).wait()
            return c
        lax.fori_loop(0, R, body, 0, unroll=8)

    @pl.when(i == 0)
    def _():
        cp = idx_copy(0, 0)
        cp.start()
        cp.wait()
        start_rows(0)

        @pl.when(n > 1)
        def _():
            idx_copy(1, 1).start()

    @pl.when(i + 1 < n)
    def _():
        idx_copy(i + 1, 1 - slot).wait()
        start_rows(1 - slot)

    @pl.when(i + 2 < n)
    def _():
        idx_copy(i + 2, slot).start()

    wait_rows(slot)
    is_p = i < n_p
    h = jnp.where(is_p, hp_ref[...], hs_ref[...])
    gate = gate_ref[...]
    for k in range(TOP_K):
        h = h + gate[:, k:k + 1] * buf_ref[slot, k]
    y = _rms(h, g_ref[...])

    @pl.when(is_p)
    def _():
        yp_ref[...] = y

    @pl.when(jnp.logical_not(is_p))
    def _():
        ysm_ref[...] = y


def _combine(dest2, gate, h2p, h2s, ys, g, *, R):
    (Tp, D), Ts = h2p.shape, h2s.shape[0]
    n_p = Tp // R
    row = lambda i: (i, 0)
    return pl.pallas_call(
        functools.partial(_combine_kernel, n_p=n_p),
        grid=((Tp + Ts) // R,),
        in_specs=[pl.BlockSpec(memory_space=pl.ANY), pl.BlockSpec((R, LANE), row)] + _two_group_specs((R, D), n_p)
                 + [pl.BlockSpec(memory_space=pl.ANY), pl.BlockSpec((1, D), lambda i: (0, 0))],
        out_specs=_two_group_specs((R, D), n_p),
        out_shape=[jax.ShapeDtypeStruct((Tp, D), F32), jax.ShapeDtypeStruct((Ts, D), F32)],
        scratch_shapes=[pltpu.VMEM((2, TOP_K, R, D), F32), pltpu.SMEM((2, R * TOP_K), jnp.int32),
                        pltpu.SemaphoreType.DMA((2,)), pltpu.SemaphoreType.DMA((2,))],
        compiler_params=_cparams("arbitrary"),
        name="moe_combine",
    )(dest2, gate, h2p, h2s, ys, g)


def _rope_tables(pos, rope, rows):
    half = rope // 2
    inv = ROPE_BASE ** (-jnp.arange(half, dtype=F32) / half)
    ang = pos.astype(F32)[:, None] * inv[None, :]
    cos, sin = jnp.cos(ang), jnp.sin(ang)
    z = jnp.zeros((pos.shape[0], LANE - rope), F32)
    cos_t = jnp.concatenate([cos, cos, z], axis=1)
    sin_t = jnp.concatenate([-sin, sin, z], axis=1)
    rep = max(1, rows // pos.shape[0])
    return jnp.tile(cos_t, (rep, 1)), jnp.tile(sin_t, (rep, 1))


def _swap_halves(w):
    half = w.shape[-1] // 2
    return jnp.concatenate([w[..., half:], w[..., :half]], axis=-1)


def _mixer(x, pos, past, wts, *, tm):
    B, S, D = x.shape
    T = B * S
    heads, ql, kl, rope = wts["heads"], wts["ql"], wts["kl"], wts["rope"]
    x2 = x.reshape(T, D)
    cos_t, sin_t = _rope_tables(pos, rope, tm)
    u, cqn, lat, kr, krb = _small_proj(x2, wts["norm_mix_g"], wts["w_small"], wts["q_norm_g"], wts["kv_norm_g"],
                                       cos_t, sin_t, tm=tm, ql=ql, kl=kl, rope=rope)
    hq, hf, hi, hg, ga, gb = _big_proj(u, wts["w_big"], (BF16, F32, BF16, BF16, BF16, BF16), tm=tm, tn=256)
    qf = _q_proj(cqn, wts["wq_raw"], wts["wq_sw"], cos_t, sin_t, tm=tm, heads=heads)
    F = hq.shape[1]
    r3 = lambda a: a.reshape(B, S, a.shape[1])
    if past is None:
        kf, v = _kv_proj(lat, krb, wts["w_uk"], wts["w_uv"], tm=tm, heads=heads)
        o_a = _attention(r3(qf), r3(kf), r3(v), heads=heads, tq=min(512, S), tk=min(256, S), hpb=2)
        o_b, st = _hgrn(r3(hq), r3(hf), r3(hi), r3(hg), wts["lb"], wts["hgrn_g"], None, heads=wts["hg_heads"],
                        L=CHUNK)
    else:
        cache_lat, cache_kr, state = past
        o_a = _attention_sample(r3(qf), cache_lat, cache_kr, r3(lat), r3(kr), wts["w_uk_t"], wts["w_uv_h"],
                                heads=heads)
        o_b, st = _hgrn(r3(hq), r3(hf), r3(hi), r3(hg), wts["lb"], wts["hgrn_g"], state, heads=wts["hg_heads"], L=S)
    h2 = _out_proj(o_a.reshape(T, D), o_b.reshape(T, F), ga, gb, x2, wts["w_o"], tm=256)
    return h2, lat.reshape(B, S, kl), kr.reshape(B, S, rope), st


def kernel(x_prompt, x_sample, cache_mla_latent, cache_mla_krope, state_hgrn, norm_mix_g, w_in, q_norm_g, w_uq,
           kv_norm_g, w_uk, w_uv, hgrn_lb_raw, hgrn_norm_g, w_o, norm_ffn_g, w_router, b_router, w_gate_up,
           b_gate_up, w_down, b_down, norm_final_g):
    depth = w_in.shape[0]
    assert depth == 1, "single-layer kernel"
    B, S, D = x_prompt.shape
    Bd, Sd, _ = x_sample.shape
    past_len = cache_mla_latent.shape[2]
    ql, heads, qk = w_uq.shape[1:]
    kl, _, nope = w_uk.shape[1:]
    vd = w_uv.shape[3]
    rope = qk - nope
    hg_heads, dk, dv = state_hgrn.shape[2:]
    n_exp, _, ff2 = w_gate_up.shape[1:]
    assert nope == LANE and vd == LANE and dk == LANE and dv == LANE and rope <= LANE
    hf_dim = hg_heads * dk
    scale = 1.0 / math.sqrt(nope + rope)
    l = 0

    wi = w_in[l]
    o_cq, o_kv, o_pe, o_hq = 0, ql, ql + kl, ql + kl + rope
    k_pe = wi[:, o_pe:o_pe + rope]
    zpad = jnp.zeros((D, LANE - rope), F32)
    w_small = jnp.concatenate([wi[:, o_cq:o_pe], k_pe, zpad, _swap_halves(k_pe), zpad], axis=1).astype(BF16)
    segs = [hf_dim, hf_dim, D, D, D, D]
    offs = [o_hq]
    for s_ in segs[:-1]:
        offs.append(offs[-1] + s_)
    w_big = [wi[:, o:o + s_].astype(BF16) for o, s_ in zip(offs, segs)]
    wq = w_uq[l] * scale
    zq = jnp.zeros((ql, heads, LANE - rope), F32)
    wq_raw = jnp.concatenate([wq[..., :nope], wq[..., nope:], zq], axis=-1).reshape(ql, heads * 2 * LANE).astype(BF16)
    wq_sw = jnp.concatenate([_swap_halves(wq[..., nope:]), zq], axis=-1).reshape(ql, heads * LANE).astype(BF16)
    lb_all = jnp.cumsum(jax.nn.softmax(hgrn_lb_raw.astype(F32), axis=0), axis=0)
    wts = dict(
        heads=heads, ql=ql, kl=kl, rope=rope, hg_heads=hg_heads,
        norm_mix_g=norm_mix_g[l][None], w_small=w_small, q_norm_g=q_norm_g[l][None], kv_norm_g=kv_norm_g[l][None],
        w_big=w_big, wq_raw=wq_raw, wq_sw=wq_sw,
        w_uk=w_uk[l].reshape(kl, heads * nope).astype(BF16), w_uv=w_uv[l].reshape(kl, heads * vd).astype(BF16),
        w_uk_t=jnp.transpose(w_uk[l], (1, 2, 0)).astype(BF16), w_uv_h=jnp.transpose(w_uv[l], (1, 0, 2)).astype(BF16),
        lb=lb_all[l][None], hgrn_g=hgrn_norm_g[l][None], w_o=w_o[l].astype(BF16),
    )

    Tp, Ts = B * S, Bd * Sd
    tm = 512
    assert S % tm == 0 or tm % S == 0
    assert Tp % tm == 0 and Ts % tm == 0 and tm % Sd == 0
    h2p, lat_p, kr_p, st_p = _mixer(x_prompt, jnp.arange(S), None, wts, tm=tm)
    past = (cache_mla_latent[l], cache_mla_krope[l], state_hgrn[l])
    h2s, lat_s, kr_s, st_s = _mixer(x_sample, past_len + jnp.arange(Sd), past, wts, tm=tm)

    T_all = Tp + Ts
    u_all, idx, gate = _router(h2p, h2s, norm_ffn_g[l][None], w_router[l], b_router[l][None], tm=tm)

    te_rows = 512
    A = T_all * TOP_K
    flat_e = idx[:, :TOP_K].reshape(A)
    onehot = (flat_e[:, None] == jnp.arange(n_exp, dtype=jnp.int32)[None, :]).astype(jnp.int32)
    csum = jnp.cumsum(onehot, axis=0)
    counts = csum[-1]
    padded = (counts + te_rows - 1) // te_rows * te_rows
    pad_end = jnp.cumsum(padded)
    pad_start = pad_end - padded
    dest = jnp.sum((csum - 1 + pad_start[None, :]) * onehot, axis=1)
    n_tiles = -(-A // te_rows) + n_exp
    P = n_tiles * te_rows
    tile_start = jnp.arange(n_tiles, dtype=jnp.int32) * te_rows
    tile_e = jnp.minimum(jnp.searchsorted(pad_end, tile_start, side="right"), n_exp - 1).astype(jnp.int32)
    tile_v = (tile_start < pad_end[-1]).astype(jnp.int32)
    last_e = tile_e[jnp.maximum(jnp.sum(tile_v) - 1, 0)]
    tile_e = jnp.where(tile_v > 0, tile_e, last_e)

    dR = 512
    xs = _dispatch((pad_start + counts).astype(jnp.int32), pad_end.astype(jnp.int32),
                   dest.reshape(T_all // dR, dR * TOP_K), u_all, P=P, R=dR)
    ys = _experts(tile_e, tile_v, xs, w_gate_up[l].astype(BF16), b_gate_up[l][:, None, :],
                  w_down[l].astype(BF16), b_down[l][:, None, :], tm=te_rows, fc=512)

    cR = 128
    y_p, y_s = _combine(dest.reshape(T_all // cR, cR * TOP_K), gate, h2p, h2s, ys, norm_final_g[None], R=cR)

    return (y_p.reshape(B, S, D), y_s.reshape(Bd, Sd, D), lat_p[None], kr_p[None], st_p[None],
            lat_s[None], kr_s[None], st_s[None])
```

```python
import functools
import math

import jax
import jax.numpy as jnp
from jax import lax
from jax.experimental import pallas as pl
from jax.experimental.pallas import tpu as pltpu

F32 = jnp.float32
BF16 = jnp.bfloat16

CHUNK = 64
NORM_EPS = 1e-6
ROPE_BASE = 10000.0
NEG_INF = -1e30
TOP_K = 4
SWIGLU_ALPHA = 1.702
SWIGLU_LIMIT = 7.0

LANE = 128
VMEM_LIMIT = 56 * 1024 * 1024


def _cparams(*sem):
    return pltpu.CompilerParams(dimension_semantics=sem, vmem_limit_bytes=VMEM_LIMIT)


def _dot(a, b):
    return jnp.dot(a, b, preferred_element_type=F32)


def _dot_nt(a, b):
    return lax.dot_general(a, b, (((1,), (1,)), ((), ())), preferred_element_type=F32)


def _rms(x, g):
    return x * lax.rsqrt(jnp.mean(x * x, axis=-1, keepdims=True) + NORM_EPS) * g


def _sigmoid(x):
    return 1.0 / (1.0 + jnp.exp(-x))


def _pack_bf16_pairs(x):
    half = x.shape[1] // 2
    bits = lax.bitcast_convert_type(x.astype(BF16).astype(F32), jnp.uint32)
    return (bits[:, :half] & jnp.uint32(0xFFFF0000)) | (bits[:, half:] >> 16)


def _unpack_bf16_pairs(p):
    hi = lax.bitcast_convert_type(p & jnp.uint32(0xFFFF0000), F32).astype(BF16)
    lo = lax.bitcast_convert_type(p << 16, F32).astype(BF16)
    return hi, lo


def _small_proj_kernel(x_ref, g_ref, w_ref, qg_ref, kvg_ref, cos_ref, sin_ref,
                       u_ref, cqn_ref, lat_ref, kr_ref, krb_ref, *, ql, kl, rope):
    u = _rms(x_ref[...], g_ref[...]).astype(BF16)
    u_ref[...] = u
    y = _dot(u, w_ref[...])
    cqn_ref[...] = _rms(y[:, :ql], qg_ref[...]).astype(BF16)
    lat_ref[...] = _rms(y[:, ql:ql + kl], kvg_ref[...])
    o = ql + kl
    kr = y[:, o:o + LANE] * cos_ref[...] + y[:, o + LANE:o + 2 * LANE] * sin_ref[...]
    kr_ref[...] = kr[:, :rope]
    krb_ref[...] = kr.astype(BF16)


def _small_proj(x2, g, w_small, qg, kvg, cos_t, sin_t, *, tm, ql, kl, rope):
    T, D = x2.shape
    n = w_small.shape[1]
    nblk = cos_t.shape[0] // tm
    row = lambda i: (i, 0)
    fix = lambda i: (0, 0)
    tab = lambda i: (i % nblk, 0)
    return pl.pallas_call(
        functools.partial(_small_proj_kernel, ql=ql, kl=kl, rope=rope),
        grid=(T // tm,),
        in_specs=[pl.BlockSpec((tm, D), row), pl.BlockSpec((1, D), fix), pl.BlockSpec((D, n), fix),
                  pl.BlockSpec((1, ql), fix), pl.BlockSpec((1, kl), fix),
                  pl.BlockSpec((tm, LANE), tab), pl.BlockSpec((tm, LANE), tab)],
        out_specs=[pl.BlockSpec((tm, D), row), pl.BlockSpec((tm, ql), row), pl.BlockSpec((tm, kl), row),
                   pl.BlockSpec((tm, rope), row), pl.BlockSpec((tm, LANE), row)],
        out_shape=[jax.ShapeDtypeStruct((T, D), BF16), jax.ShapeDtypeStruct((T, ql), BF16),
                   jax.ShapeDtypeStruct((T, kl), F32), jax.ShapeDtypeStruct((T, rope), F32),
                   jax.ShapeDtypeStruct((T, LANE), BF16)],
        compiler_params=_cparams("parallel"),
        name="small_proj",
    )(x2, g, w_small, qg, kvg, cos_t, sin_t)


def _big_proj_kernel(u_ref, *refs, nseg):
    u = u_ref[...]
    for s in range(nseg):
        refs[nseg + s][...] = _dot(u, refs[s][...]).astype(refs[nseg + s].dtype)


def _big_proj(u, ws, out_dtypes, *, tm, tn):
    T, D = u.shape
    n = ws[0].shape[1]
    nseg = len(ws)
    return pl.pallas_call(
        functools.partial(_big_proj_kernel, nseg=nseg),
        grid=(T // tm, n // tn),
        in_specs=[pl.BlockSpec((tm, D), lambda i, j: (i, 0))]
                 + [pl.BlockSpec((D, tn), lambda i, j: (0, j)) for _ in ws],
        out_specs=[pl.BlockSpec((tm, tn), lambda i, j: (i, j)) for _ in ws],
        out_shape=[jax.ShapeDtypeStruct((T, n), dt) for dt in out_dtypes],
        compiler_params=_cparams("parallel", "arbitrary"),
        name="big_proj",
    )(u, *ws)


def _q_proj_kernel(c_ref, wr_ref, ws_ref, cos_ref, sin_ref, q_ref, *, heads):
    c = c_ref[...]
    raw = _dot(c, wr_ref[...])
    sw = _dot(c, ws_ref[...])
    cos = cos_ref[...]
    sin = sin_ref[...]
    for h in range(heads):
        b = 2 * LANE * h
        q_ref[:, b:b + LANE] = raw[:, b:b + LANE].astype(BF16)
        r = raw[:, b + LANE:b + 2 * LANE] * cos + sw[:, LANE * h:LANE * (h + 1)] * sin
        q_ref[:, b + LANE:b + 2 * LANE] = r.astype(BF16)


def _q_proj(cqn, wq_raw, wq_sw, cos_t, sin_t, *, tm, heads):
    T, ql = cqn.shape
    nblk = cos_t.shape[0] // tm
    row = lambda i: (i, 0)
    fix = lambda i: (0, 0)
    tab = lambda i: (i % nblk, 0)
    n = wq_raw.shape[1]
    return pl.pallas_call(
        functools.partial(_q_proj_kernel, heads=heads),
        grid=(T // tm,),
        in_specs=[pl.BlockSpec((tm, ql), row), pl.BlockSpec(wq_raw.shape, fix), pl.BlockSpec(wq_sw.shape, fix),
                  pl.BlockSpec((tm, LANE), tab), pl.BlockSpec((tm, LANE), tab)],
        out_specs=pl.BlockSpec((tm, n), row),
        out_shape=jax.ShapeDtypeStruct((T, n), BF16),
        compiler_params=_cparams("parallel"),
        name="q_proj",
    )(cqn, wq_raw, wq_sw, cos_t, sin_t)


def _kv_proj_kernel(lat_ref, krb_ref, wk_ref, wv_ref, k_ref, v_ref, *, heads):
    lat = lat_ref[...].astype(BF16)
    kn = _dot(lat, wk_ref[...]).astype(BF16)
    v_ref[...] = _dot(lat, wv_ref[...]).astype(BF16)
    krb = krb_ref[...]
    for h in range(heads):
        b = 2 * LANE * h
        k_ref[:, b:b + LANE] = kn[:, LANE * h:LANE * (h + 1)]
        k_ref[:, b + LANE:b + 2 * LANE] = krb


def _kv_proj(lat, krb, wk, wv, *, tm, heads):
    T, kl = lat.shape
    row = lambda i: (i, 0)
    fix = lambda i: (0, 0)
    return pl.pallas_call(
        functools.partial(_kv_proj_kernel, heads=heads),
        grid=(T // tm,),
        in_specs=[pl.BlockSpec((tm, kl), row), pl.BlockSpec((tm, LANE), row),
                  pl.BlockSpec(wk.shape, fix), pl.BlockSpec(wv.shape, fix)],
        out_specs=[pl.BlockSpec((tm, 2 * LANE * heads), row), pl.BlockSpec((tm, wv.shape[1]), row)],
        out_shape=[jax.ShapeDtypeStruct((T, 2 * LANE * heads), BF16), jax.ShapeDtypeStruct((T, wv.shape[1]), BF16)],
        compiler_params=_cparams("parallel"),
        name="kv_proj",
    )(lat, krb, wk, wv)


def _attn_kernel(q_ref, k_ref, v_ref, o_ref, *, tq, tk, hpb):
    qi = pl.program_id(2)
    kd = 2 * LANE
    vd = v_ref.shape[2] // hpb
    qs = [q_ref[0, :, g * kd:(g + 1) * kd] for g in range(hpb)]

    def update(carry, s, v):
        m, l, acc = carry
        m_new = jnp.maximum(m, jnp.max(s, axis=-1, keepdims=True))
        alpha = jnp.exp(m - m_new)
        p = jnp.exp(s - m_new)
        l = alpha * l + jnp.sum(p, axis=-1, keepdims=True)
        acc = alpha * acc + _dot(p.astype(BF16), v)
        return m_new, l, acc

    def block(carry, start, mask):
        new = []
        for g in range(hpb):
            k = k_ref[0, pl.ds(start, tk), g * kd:(g + 1) * kd]
            v = v_ref[0, pl.ds(start, tk), g * vd:(g + 1) * vd]
            s = _dot_nt(qs[g], k)
            if mask is not None:
                s = jnp.where(mask, s, NEG_INF)
            new.append(update(carry[g], s, v))
        return tuple(new)

    one = (jnp.full((tq, 1), NEG_INF, F32), jnp.zeros((tq, 1), F32), jnp.zeros((tq, vd), F32))
    nfull = qi * (tq // tk)
    carry = lax.fori_loop(0, nfull, lambda j, c: block(c, pl.multiple_of(j * tk, tk), None), (one,) * hpb)
    qc = lax.broadcasted_iota(jnp.int32, (tq, tk), 0) // CHUNK
    kc = lax.broadcasted_iota(jnp.int32, (tq, tk), 1) // CHUNK
    for d in range(tq // tk):
        carry = block(carry, pl.multiple_of((nfull + d) * tk, tk), kc + d * (tk // CHUNK) <= qc)
    for g in range(hpb):
        m, l, acc = carry[g]
        o_ref[0, :, g * vd:(g + 1) * vd] = (acc / l).astype(BF16)


def _attention(qf, kf, v, *, heads, tq, tk, hpb):
    B, S, _ = qf.shape
    vd = v.shape[2] // heads
    return pl.pallas_call(
        functools.partial(_attn_kernel, tq=tq, tk=tk, hpb=hpb),
        grid=(B, heads // hpb, S // tq),
        in_specs=[pl.BlockSpec((1, tq, 2 * LANE * hpb), lambda b, h, i: (b, i, h)),
                  pl.BlockSpec((1, S, 2 * LANE * hpb), lambda b, h, i: (b, 0, h)),
                  pl.BlockSpec((1, S, vd * hpb), lambda b, h, i: (b, 0, h))],
        out_specs=pl.BlockSpec((1, tq, vd * hpb), lambda b, h, i: (b, i, h)),
        out_shape=jax.ShapeDtypeStruct((B, S, heads * vd), BF16),
        compiler_params=_cparams("parallel", "parallel", "arbitrary"),
        name="mla_prompt_attn",
    )(qf, kf, v)


def _attn_sample_kernel(q_ref, cl_ref, ck_ref, ln_ref, kn_ref, wuk_ref, wuv_ref, o_ref,
                        ql_ref, qr_ref, ol_ref, *, heads, rope):
    sd = q_ref.shape[1]
    for h in range(heads):
        b = 2 * LANE * h
        ql_ref[h * sd:(h + 1) * sd, :] = _dot(q_ref[0, :, b:b + LANE], wuk_ref[h]).astype(BF16)
        qr_ref[h * sd:(h + 1) * sd, :] = q_ref[0, :, b + LANE:b + 2 * LANE]
    ql = ql_ref[...]
    qr = qr_ref[...][:, :rope]
    cl = cl_ref[0].astype(BF16)
    ck = ck_ref[0].astype(BF16)
    ln = ln_ref[0].astype(BF16)
    kn = kn_ref[0].astype(BF16)
    s_past = _dot_nt(ql, cl) + _dot_nt(qr, ck)
    s_new = _dot_nt(ql, ln) + _dot_nt(qr, kn)
    m = jnp.maximum(jnp.max(s_past, axis=-1, keepdims=True), jnp.max(s_new, axis=-1, keepdims=True))
    p_past = jnp.exp(s_past - m)
    p_new = jnp.exp(s_new - m)
    l = jnp.sum(p_past, axis=-1, keepdims=True) + jnp.sum(p_new, axis=-1, keepdims=True)
    o_lat = (_dot(p_past.astype(BF16), cl) + _dot(p_new.astype(BF16), ln)) / l
    ol_ref[...] = o_lat.astype(BF16)
    vd = wuv_ref.shape[2]
    for h in range(heads):
        o_ref[0, :, h * vd:(h + 1) * vd] = _dot(ol_ref[h * sd:(h + 1) * sd, :], wuv_ref[h]).astype(BF16)


def _attention_sample(qf, cache_lat, cache_kr, lat_new, kr_new, wuk_t, wuv, *, heads):
    B, sd, _ = qf.shape
    P, kl = cache_lat.shape[1:]
    rope = cache_kr.shape[2]
    vd = wuv.shape[2]
    b3 = lambda b: (b, 0, 0)
    fix = lambda b: (0, 0, 0)
    return pl.pallas_call(
        functools.partial(_attn_sample_kernel, heads=heads, rope=rope),
        grid=(B,),
        in_specs=[pl.BlockSpec((1, sd, qf.shape[2]), b3), pl.BlockSpec((1, P, kl), b3), pl.BlockSpec((1, P, rope), b3),
                  pl.BlockSpec((1, sd, kl), b3), pl.BlockSpec((1, sd, rope), b3),
                  pl.BlockSpec(wuk_t.shape, fix), pl.BlockSpec(wuv.shape, fix)],
        out_specs=pl.BlockSpec((1, sd, heads * vd), b3),
        out_shape=jax.ShapeDtypeStruct((B, sd, heads * vd), BF16),
        scratch_shapes=[pltpu.VMEM((heads * sd, kl), BF16), pltpu.VMEM((heads * sd, LANE), BF16),
                        pltpu.VMEM((heads * sd, kl), BF16)],
        compiler_params=_cparams("parallel"),
        name="mla_sample_attn",
    )(qf, cache_lat, cache_kr, lat_new, kr_new, wuk_t, wuv)


def _hgrn_kernel(*refs, heads, has_state):
    if has_state:
        hq_ref, hf_ref, hi_ref, hg_ref, lb_ref, g_ref, s0_ref, o_ref, sn_ref, st_ref = refs
    else:
        hq_ref, hf_ref, hi_ref, hg_ref, lb_ref, g_ref, o_ref, sn_ref, st_ref = refs
    c = pl.program_id(1)
    L = hq_ref.shape[1]
    dk = st_ref.shape[2]

    @pl.when(c == 0)
    def _():
        if has_state:
            for h in range(heads):
                st_ref[h] = s0_ref[0, h].T
        else:
            st_ref[...] = jnp.zeros_like(st_ref)

    row = lax.broadcasted_iota(jnp.int32, (L, L), 0)
    col = lax.broadcasted_iota(jnp.int32, (L, L), 1)
    causal = col <= row
    tri = jnp.where(causal, 1.0, 0.0).astype(BF16)
    lb = lb_ref[...]
    f = lb + (1.0 - lb) * _sigmoid(hf_ref[0])
    kk = 1.0 - f
    logf = jnp.log(f)
    hi = logf.astype(BF16)
    r1 = logf - hi.astype(F32)
    mid = r1.astype(BF16)
    lo = (r1 - mid.astype(F32)).astype(BF16)
    cum = _dot(tri, hi) + _dot(tri, mid) + _dot(tri, lo)
    tot = cum[L - 1:L, :]
    q_dec = (hq_ref[0].astype(F32) * jnp.exp(cum)).astype(BF16)
    k_dec = (kk * jnp.exp(-cum)).astype(BF16)
    k2 = (kk * jnp.exp(tot - cum)).astype(BF16)
    decay = jnp.exp(tot)
    vb = hi_ref[0]
    v32 = vb.astype(F32)
    sls = [slice(h * dk, (h + 1) * dk) for h in range(heads)]
    atts = [jnp.where(causal, _dot_nt(q_dec[:, sl], k_dec[:, sl]), 0.0).astype(BF16) for sl in sls]
    sts = [st_ref[h] for h in range(heads)]
    outs = [_dot_nt(q_dec[:, sl], sts[h].astype(BF16)) + _dot(atts[h], vb[:, sl]) for h, sl in enumerate(sls)]
    for h, sl in enumerate(sls):
        st_ref[h] = decay[:, sl] * sts[h] + _dot(v32[:, sl].T.astype(BF16), k2[:, sl])
    g = g_ref[...]
    for h, sl in enumerate(sls):
        o = outs[h]
        on = o * lax.rsqrt(jnp.mean(o * o, axis=-1, keepdims=True) + NORM_EPS) * g
        hg = hg_ref[0, :, sl].astype(F32)
        o_ref[0, :, sl] = (on * (hg * _sigmoid(hg))).astype(BF16)

    @pl.when(c == pl.num_programs(1) - 1)
    def _():
        for h in range(heads):
            sn_ref[0, h] = st_ref[h].T


def _hgrn(hq, hf, hi, hg, lb, g, s0, *, heads, L):
    B, S, F = hq.shape
    dk = F // heads
    blk = lambda b, c: (b, c, 0)
    fix2 = lambda b, c: (0, 0)
    st4 = lambda b, c: (b, 0, 0, 0)
    in_specs = [pl.BlockSpec((1, L, F), blk)] * 4 + [pl.BlockSpec((1, F), fix2), pl.BlockSpec((1, dk), fix2)]
    args = [hq, hf, hi, hg, lb, g]
    if s0 is not None:
        in_specs.append(pl.BlockSpec((1, heads, dk, dk), st4))
        args.append(s0)
    return pl.pallas_call(
        functools.partial(_hgrn_kernel, heads=heads, has_state=s0 is not None),
        grid=(B, S // L),
        in_specs=in_specs,
        out_specs=[pl.BlockSpec((1, L, F), blk), pl.BlockSpec((1, heads, dk, dk), st4)],
        out_shape=[jax.ShapeDtypeStruct((B, S, F), BF16), jax.ShapeDtypeStruct((B, heads, dk, dk), F32)],
        scratch_shapes=[pltpu.VMEM((heads, dk, dk), F32)],
        compiler_params=_cparams("parallel", "arbitrary"),
        name="hgrn",
    )(*args)


def _out_proj_kernel(oa_ref, ob_ref, ga_ref, gb_ref, x_ref, w_ref, o_ref, *, kc):
    acc = x_ref[...]
    for k in range(w_ref.shape[0] // kc):
        sl = slice(k * kc, (k + 1) * kc)
        ga = _sigmoid(ga_ref[:, sl].astype(F32))
        gb = _sigmoid(gb_ref[:, sl].astype(F32))
        m = (ga * oa_ref[:, sl].astype(F32) + gb * ob_ref[:, sl].astype(F32)).astype(BF16)
        acc = acc + _dot(m, w_ref[sl, :])
    o_ref[...] = acc


def _out_proj(oa, ob, ga, gb, x2, w_o, *, tm):
    T, D = x2.shape
    row = lambda i: (i, 0)
    return pl.pallas_call(
        functools.partial(_out_proj_kernel, kc=512),
        grid=(T // tm,),
        in_specs=[pl.BlockSpec((tm, D), row)] * 5 + [pl.BlockSpec((D, D), lambda i: (0, 0))],
        out_specs=pl.BlockSpec((tm, D), row),
        out_shape=jax.ShapeDtypeStruct((T, D), F32),
        compiler_params=_cparams("parallel"),
        name="out_proj",
    )(oa, ob, ga, gb, x2, w_o)


def _router_kernel(hp_ref, hs_ref, g_ref, w_ref, b_ref, u_ref, idx_ref, gate_ref, *, n_exp, n_p):
    h = jnp.where(pl.program_id(0) < n_p, hp_ref[...], hs_ref[...])
    u = _rms(h, g_ref[...])
    u_ref[...] = _pack_bf16_pairs(u)
    logits = jnp.dot(u, w_ref[...], precision=lax.Precision.HIGHEST, preferred_element_type=F32) + b_ref[...]
    tm = logits.shape[0]
    lane_e = lax.broadcasted_iota(jnp.int32, (tm, n_exp), 1)
    lane_o = lax.broadcasted_iota(jnp.int32, (tm, LANE), 1)
    idx_out = jnp.zeros((tm, LANE), jnp.int32)
    val_out = jnp.zeros((tm, LANE), F32)
    vals = []
    for k in range(TOP_K):
        m = jnp.max(logits, axis=-1, keepdims=True)
        i = jnp.min(jnp.where(logits == m, lane_e, n_exp), axis=-1, keepdims=True)
        logits = jnp.where(lane_e == i, -jnp.inf, logits)
        vals.append(m)
        idx_out = jnp.where(lane_o == k, i, idx_out)
    es = [jnp.exp(v - vals[0]) for v in vals]
    den = es[0]
    for e in es[1:]:
        den = den + e
    for k in range(TOP_K):
        val_out = jnp.where(lane_o == k, es[k] / den, val_out)
    idx_ref[...] = idx_out
    gate_ref[...] = val_out


def _two_group_specs(shape, n_p):
    return [pl.BlockSpec(shape, lambda i: (jnp.minimum(i, n_p - 1), 0)),
            pl.BlockSpec(shape, lambda i: (jnp.maximum(i - n_p, 0), 0))]


def _router(h2p, h2s, g, w_r, b_r, *, tm):
    (Tp, D), Ts = h2p.shape, h2s.shape[0]
    T = Tp + Ts
    n_exp = w_r.shape[1]
    row = lambda i: (i, 0)
    fix = lambda i: (0, 0)
    return pl.pallas_call(
        functools.partial(_router_kernel, n_exp=n_exp, n_p=Tp // tm),
        grid=(T // tm,),
        in_specs=_two_group_specs((tm, D), Tp // tm)
                 + [pl.BlockSpec((1, D), fix), pl.BlockSpec((D, n_exp), fix), pl.BlockSpec((1, n_exp), fix)],
        out_specs=[pl.BlockSpec((tm, D // 2), row), pl.BlockSpec((tm, LANE), row), pl.BlockSpec((tm, LANE), row)],
        out_shape=[jax.ShapeDtypeStruct((T, D // 2), jnp.uint32), jax.ShapeDtypeStruct((T, LANE), jnp.int32),
                   jax.ShapeDtypeStruct((T, LANE), F32)],
        compiler_params=_cparams("arbitrary"),
        name="router",
    )(h2p, h2s, g, w_r, b_r)


def _row_copy(src_ref, dst_ref, src_row, dst_row, sem):
    return pltpu.make_async_copy(src_ref.at[pl.ds(src_row, 1)], dst_ref.at[pl.ds(dst_row, 1)], sem)


def _dispatch_kernel(zs_ref, ze_ref, dest_hbm, u_ref, xs_hbm, idx_smem, zrow_ref, sem_idx, sem, sem_z, *, R, n_exp):
    i = pl.program_id(0)
    n = pl.num_programs(0)
    slot = i % 2
    N = R * TOP_K

    def idx_copy(step, s):
        return pltpu.make_async_copy(dest_hbm.at[step], idx_smem.at[pl.ds(pl.multiple_of(s * N, N), N)],
                                     sem_idx.at[s])

    def start_rows(s):
        def body(r, c):
            for k in range(TOP_K):
                _row_copy(u_ref, xs_hbm, r, idx_smem[s * N + r * TOP_K + k], sem).start()
            return c
        lax.fori_loop(0, R, body, 0, unroll=8)

    def wait_rows():
        def body(r, c):
            for k in range(TOP_K):
                _row_copy(u_ref, xs_hbm, 0, 0, sem).wait()
            return c
        lax.fori_loop(0, R, body, 0, unroll=8)

    def zero_rows(start):
        def per_expert(e, c):
            def body(r, c2):
                cp = _row_copy(zrow_ref, xs_hbm, 0, r, sem_z)
                cp.start() if start else cp.wait()
                return c2
            return lax.fori_loop(zs_ref[e], ze_ref[e], body, c)
        lax.fori_loop(0, n_exp, per_expert, 0)

    @pl.when(i == 0)
    def _():
        zrow_ref[...] = jnp.zeros_like(zrow_ref)
        idx_copy(0, 0).start()

    idx_copy(i, slot).wait()

    @pl.when(i + 1 < n)
    def _():
        idx_copy(i + 1, 1 - slot).start()

    start_rows(slot)

    @pl.when(i == n - 1)
    def _():
        zero_rows(True)

    wait_rows()

    @pl.when(i == n - 1)
    def _():
        zero_rows(False)


def _dispatch(zs, ze, dest2, u, *, P, R):
    n = dest2.shape[0]
    D = u.shape[1]
    grid_spec = pltpu.PrefetchScalarGridSpec(
        num_scalar_prefetch=2,
        grid=(n,),
        in_specs=[pl.BlockSpec(memory_space=pl.ANY), pl.BlockSpec((R, D), lambda i, zs, ze: (i, 0))],
        out_specs=pl.BlockSpec(memory_space=pl.ANY),
        scratch_shapes=[pltpu.SMEM((2 * R * TOP_K,), jnp.int32), pltpu.VMEM((8, D), u.dtype),
                        pltpu.SemaphoreType.DMA((2,)), pltpu.SemaphoreType.DMA, pltpu.SemaphoreType.DMA],
    )
    return pl.pallas_call(
        functools.partial(_dispatch_kernel, R=R, n_exp=zs.shape[0]),
        grid_spec=grid_spec,
        out_shape=jax.ShapeDtypeStruct((P, D), u.dtype),
        compiler_params=_cparams("arbitrary"),
        name="moe_dispatch",
    )(zs, ze, dest2, u)


def _expert_kernel(te_ref, nv_ref, x_ref, wg_ref, wu_ref, bg_ref, bu_ref, wd_ref, bd_ref, y_ref, xb_ref):
    i = pl.program_id(0)
    c = pl.program_id(1)
    half = x_ref.shape[1]

    @pl.when(c == 0)
    def _():
        y_ref[...] = jnp.broadcast_to(bd_ref[0], y_ref.shape)
        hi, lo = _unpack_bf16_pairs(x_ref[...])
        xb_ref[:, :half] = hi
        xb_ref[:, half:] = lo

    @pl.when(nv_ref[i] > 0)
    def _():
        xb = xb_ref[...]
        gate = jnp.minimum(_dot(xb, wg_ref[0]) + bg_ref[0], SWIGLU_LIMIT)
        up = jnp.clip(_dot(xb, wu_ref[0]) + bu_ref[0], -SWIGLU_LIMIT, SWIGLU_LIMIT)
        act = gate * _sigmoid(SWIGLU_ALPHA * gate) * (up + 1.0)
        y_ref[...] += _dot(act.astype(BF16), wd_ref[0])


def _experts(tile_e, tile_nv, xs, w_gu, b_gu, w_dn, b_dn, *, tm, fc):
    P, half = xs.shape
    D = 2 * half
    E, _, ff2 = w_gu.shape
    ff = ff2 // 2
    nc = ff // fc
    n_tiles = P // tm

    def cc(i, c, nv):
        return jnp.where(nv[i] > 0, c, nc - 1)

    grid_spec = pltpu.PrefetchScalarGridSpec(
        num_scalar_prefetch=2,
        grid=(n_tiles, nc),
        in_specs=[
            pl.BlockSpec((tm, half), lambda i, c, te, nv: (jnp.where(nv[i] > 0, i, 0), 0)),
            pl.BlockSpec((1, D, fc), lambda i, c, te, nv: (te[i], 0, cc(i, c, nv))),
            pl.BlockSpec((1, D, fc), lambda i, c, te, nv: (te[i], 0, nc + cc(i, c, nv))),
            pl.BlockSpec((1, 1, fc), lambda i, c, te, nv: (te[i], 0, cc(i, c, nv))),
            pl.BlockSpec((1, 1, fc), lambda i, c, te, nv: (te[i], 0, nc + cc(i, c, nv))),
            pl.BlockSpec((1, fc, D), lambda i, c, te, nv: (te[i], cc(i, c, nv), 0)),
            pl.BlockSpec((1, 1, D), lambda i, c, te, nv: (te[i], 0, 0)),
        ],
        out_specs=pl.BlockSpec((tm, D), lambda i, c, te, nv: (i, 0)),
        scratch_shapes=[pltpu.VMEM((tm, D), BF16)],
    )
    return pl.pallas_call(
        _expert_kernel,
        grid_spec=grid_spec,
        out_shape=jax.ShapeDtypeStruct((P, D), F32),
        compiler_params=_cparams("arbitrary", "arbitrary"),
        name="moe_experts",
    )(tile_e, tile_nv, xs, w_gu, w_gu, b_gu, b_gu, w_dn, b_dn)


def _combine_kernel(dest_hbm, gate_ref, hp_ref, hs_ref, ys_hbm, g_ref, yp_ref, ysm_ref, buf_ref, idx_smem, sem_idx,
                    sem, *, n_p):
    i = pl.program_id(0)
    n = pl.num_programs(0)
    R = hp_ref.shape[0]
    slot = i % 2
    N = R * TOP_K

    def idx_copy(step, s):
        return pltpu.make_async_copy(dest_hbm.at[step], idx_smem.at[pl.ds(pl.multiple_of(s * N, N), N)],
                                     sem_idx.at[s])

    def start_rows(s):
        def body(r, c):
            for k in range(TOP_K):
                _row_copy(ys_hbm, buf_ref.at[s, k], idx_smem[s * N + r * TOP_K + k], r, sem.at[s]).start()
            return c
        lax.fori_loop(0, R, body, 0, unroll=8)

    def wait_rows(s):
        def body(r, c):
            for k in range(TOP_K):
                _row_copy(ys_hbm, buf_ref.at[s, k], 0, r, sem.at[s]).wait()
            return c
        lax.fori_loop(0, R, body, 0, unroll=8)

    @pl.when(i == 0)
    def _():
        cp = idx_copy(0, 0)
        cp.start()
        cp.wait()
        start_rows(0)

        @pl.when(n > 1)
        def _():
            idx_copy(1, 1).start()

    @pl.when(i + 1 < n)
    def _():
        idx_copy(i + 1, 1 - slot).wait()
        start_rows(1 - slot)

    @pl.when(i + 2 < n)
    def _():
        idx_copy(i + 2, slot).start()

    wait_rows(slot)
    is_p = i < n_p
    h = jnp.where(is_p, hp_ref[...], hs_ref[...])
    gate = gate_ref[...]
    for k in range(TOP_K):
        h = h + gate[:, k:k + 1] * buf_ref[slot, k]
    y = _rms(h, g_ref[...])

    @pl.when(is_p)
    def _():
        yp_ref[...] = y

    @pl.when(jnp.logical_not(is_p))
    def _():
        ysm_ref[...] = y


def _combine(dest2, gate, h2p, h2s, ys, g, *, R):
    (Tp, D), Ts = h2p.shape, h2s.shape[0]
    n_p = Tp // R
    row = lambda i: (i, 0)
    return pl.pallas_call(
        functools.partial(_combine_kernel, n_p=n_p),
        grid=((Tp + Ts) // R,),
        in_specs=[pl.BlockSpec(memory_space=pl.ANY), pl.BlockSpec((R, LANE), row)] + _two_group_specs((R, D), n_p)
                 + [pl.BlockSpec(memory_space=pl.ANY), pl.BlockSpec((1, D), lambda i: (0, 0))],
        out_specs=_two_group_specs((R, D), n_p),
        out_shape=[jax.ShapeDtypeStruct((Tp, D), F32), jax.ShapeDtypeStruct((Ts, D), F32)],
        scratch_shapes=[pltpu.VMEM((2, TOP_K, R, D), F32), pltpu.SMEM((2 * R * TOP_K,), jnp.int32),
                        pltpu.SemaphoreType.DMA((2,)), pltpu.SemaphoreType.DMA((2,))],
        compiler_params=_cparams("arbitrary"),
        name="moe_combine",
    )(dest2, gate, h2p, h2s, ys, g)


def _rope_tables(pos, rope, rows):
    half = rope // 2
    inv = ROPE_BASE ** (-jnp.arange(half, dtype=F32) / half)
    ang = pos.astype(F32)[:, None] * inv[None, :]
    cos, sin = jnp.cos(ang), jnp.sin(ang)
    z = jnp.zeros((pos.shape[0], LANE - rope), F32)
    cos_t = jnp.concatenate([cos, cos, z], axis=1)
    sin_t = jnp.concatenate([-sin, sin, z], axis=1)
    rep = max(1, rows // pos.shape[0])
    return jnp.tile(cos_t, (rep, 1)), jnp.tile(sin_t, (rep, 1))


def _swap_halves(w):
    half = w.shape[-1] // 2
    return jnp.concatenate([w[..., half:], w[..., :half]], axis=-1)


def _mixer(x, pos, past, wts, *, tm):
    B, S, D = x.shape
    T = B * S
    heads, ql, kl, rope = wts["heads"], wts["ql"], wts["kl"], wts["rope"]
    x2 = x.reshape(T, D)
    cos_t, sin_t = _rope_tables(pos, rope, tm)
    u, cqn, lat, kr, krb = _small_proj(x2, wts["norm_mix_g"], wts["w_small"], wts["q_norm_g"], wts["kv_norm_g"],
                                       cos_t, sin_t, tm=tm, ql=ql, kl=kl, rope=rope)
    hq, hf, hi, hg, ga, gb = _big_proj(u, wts["w_big"], (BF16, F32, BF16, BF16, BF16, BF16), tm=tm, tn=256)
    qf = _q_proj(cqn, wts["wq_raw"], wts["wq_sw"], cos_t, sin_t, tm=tm, heads=heads)
    F = hq.shape[1]
    r3 = lambda a: a.reshape(B, S, a.shape[1])
    if past is None:
        kf, v = _kv_proj(lat, krb, wts["w_uk"], wts["w_uv"], tm=tm, heads=heads)
        o_a = _attention(r3(qf), r3(kf), r3(v), heads=heads, tq=min(512, S), tk=min(256, S), hpb=2)
        o_b, st = _hgrn(r3(hq), r3(hf), r3(hi), r3(hg), wts["lb"], wts["hgrn_g"], None, heads=wts["hg_heads"],
                        L=CHUNK)
    else:
        cache_lat, cache_kr, state = past
        o_a = _attention_sample(r3(qf), cache_lat, cache_kr, r3(lat), r3(kr), wts["w_uk_t"], wts["w_uv_h"],
                                heads=heads)
        o_b, st = _hgrn(r3(hq), r3(hf), r3(hi), r3(hg), wts["lb"], wts["hgrn_g"], state, heads=wts["hg_heads"], L=S)
    h2 = _out_proj(o_a.reshape(T, D), o_b.reshape(T, F), ga, gb, x2, wts["w_o"], tm=256)
    return h2, lat.reshape(B, S, kl), kr.reshape(B, S, rope), st


def kernel(x_prompt, x_sample, cache_mla_latent, cache_mla_krope, state_hgrn, norm_mix_g, w_in, q_norm_g, w_uq,
           kv_norm_g, w_uk, w_uv, hgrn_lb_raw, hgrn_norm_g, w_o, norm_ffn_g, w_router, b_router, w_gate_up,
           b_gate_up, w_down, b_down, norm_final_g):
    depth = w_in.shape[0]
    assert depth == 1, "single-layer kernel"
    B, S, D = x_prompt.shape
    Bd, Sd, _ = x_sample.shape
    past_len = cache_mla_latent.shape[2]
    ql, heads, qk = w_uq.shape[1:]
    kl, _, nope = w_uk.shape[1:]
    vd = w_uv.shape[3]
    rope = qk - nope
    hg_heads, dk, dv = state_hgrn.shape[2:]
    n_exp, _, ff2 = w_gate_up.shape[1:]
    assert nope == LANE and vd == LANE and dk == LANE and dv == LANE and rope <= LANE
    hf_dim = hg_heads * dk
    scale = 1.0 / math.sqrt(nope + rope)
    l = 0

    wi = w_in[l]
    o_cq, o_kv, o_pe, o_hq = 0, ql, ql + kl, ql + kl + rope
    k_pe = wi[:, o_pe:o_pe + rope]
    zpad = jnp.zeros((D, LANE - rope), F32)
    w_small = jnp.concatenate([wi[:, o_cq:o_pe], k_pe, zpad, _swap_halves(k_pe), zpad], axis=1).astype(BF16)
    segs = [hf_dim, hf_dim, D, D, D, D]
    offs = [o_hq]
    for s_ in segs[:-1]:
        offs.append(offs[-1] + s_)
    w_big = [wi[:, o:o + s_].astype(BF16) for o, s_ in zip(offs, segs)]
    wq = w_uq[l] * scale
    zq = jnp.zeros((ql, heads, LANE - rope), F32)
    wq_raw = jnp.concatenate([wq[..., :nope], wq[..., nope:], zq], axis=-1).reshape(ql, heads * 2 * LANE).astype(BF16)
    wq_sw = jnp.concatenate([_swap_halves(wq[..., nope:]), zq], axis=-1).reshape(ql, heads * LANE).astype(BF16)
    lb_all = jnp.cumsum(jax.nn.softmax(hgrn_lb_raw.astype(F32), axis=0), axis=0)
    wts = dict(
        heads=heads, ql=ql, kl=kl, rope=rope, hg_heads=hg_heads,
        norm_mix_g=norm_mix_g[l][None], w_small=w_small, q_norm_g=q_norm_g[l][None], kv_norm_g=kv_norm_g[l][None],
        w_big=w_big, wq_raw=wq_raw, wq_sw=wq_sw,
        w_uk=w_uk[l].reshape(kl, heads * nope).astype(BF16), w_uv=w_uv[l].reshape(kl, heads * vd).astype(BF16),
        w_uk_t=jnp.transpose(w_uk[l], (1, 2, 0)).astype(BF16), w_uv_h=jnp.transpose(w_uv[l], (1, 0, 2)).astype(BF16),
        lb=lb_all[l][None], hgrn_g=hgrn_norm_g[l][None], w_o=w_o[l].astype(BF16),
    )

    Tp, Ts = B * S, Bd * Sd
    tm = 512
    assert S % tm == 0 or tm % S == 0
    assert Tp % tm == 0 and Ts % tm == 0 and tm % Sd == 0
    h2p, lat_p, kr_p, st_p = _mixer(x_prompt, jnp.arange(S), None, wts, tm=tm)
    past = (cache_mla_latent[l], cache_mla_krope[l], state_hgrn[l])
    h2s, lat_s, kr_s, st_s = _mixer(x_sample, past_len + jnp.arange(Sd), past, wts, tm=tm)

    T_all = Tp + Ts
    u_all, idx, gate = _router(h2p, h2s, norm_ffn_g[l][None], w_router[l], b_router[l][None], tm=tm)

    te_rows = 512
    A = T_all * TOP_K
    flat_e = idx[:, :TOP_K].reshape(A)
    onehot = (flat_e[:, None] == jnp.arange(n_exp, dtype=jnp.int32)[None, :]).astype(jnp.int32)
    csum = jnp.cumsum(onehot, axis=0)
    counts = csum[-1]
    padded = (counts + te_rows - 1) // te_rows * te_rows
    pad_end = jnp.cumsum(padded)
    pad_start = pad_end - padded
    used_end = pad_start + counts
    dest = jnp.sum((csum - 1 + pad_start[None, :]) * onehot, axis=1)
    n_tiles = -(-A // te_rows) + n_exp
    P = n_tiles * te_rows
    tile_start = jnp.arange(n_tiles, dtype=jnp.int32) * te_rows
    tile_e = jnp.minimum(jnp.searchsorted(pad_end, tile_start, side="right"), n_exp - 1).astype(jnp.int32)
    tile_nv = jnp.clip(used_end[tile_e] - tile_start, 0, te_rows).astype(jnp.int32)
    tile_nv = jnp.where(tile_start < pad_end[-1], tile_nv, 0)
    last_e = tile_e[jnp.maximum(jnp.sum((tile_start < pad_end[-1]).astype(jnp.int32)) - 1, 0)]
    tile_e = jnp.where(tile_nv > 0, tile_e, last_e)

    dR = 512
    xs = _dispatch(used_end.astype(jnp.int32), pad_end.astype(jnp.int32),
                   dest.reshape(T_all // dR, dR * TOP_K), u_all, P=P, R=dR)
    ys = _experts(tile_e, tile_nv, xs, w_gate_up[l].astype(BF16), b_gate_up[l][:, None, :],
                  w_down[l].astype(BF16), b_down[l][:, None, :], tm=te_rows, fc=512)

    cR = 256
    y_p, y_s = _combine(dest.reshape(T_all // cR, cR * TOP_K), gate, h2p, h2s, ys, norm_final_g[None], R=cR)

    return (y_p.reshape(B, S, D), y_s.reshape(Bd, Sd, D), lat_p[None], kr_p[None], st_p[None],
            lat_s[None], kr_s[None], st_s[None])
```

```python
import functools
import math

import jax
import jax.numpy as jnp
from jax import lax
from jax.experimental import pallas as pl
from jax.experimental.pallas import tpu as pltpu

F32 = jnp.float32
BF16 = jnp.bfloat16

CHUNK = 64
NORM_EPS = 1e-6
ROPE_BASE = 10000.0
NEG_INF = -1e30
TOP_K = 4
SWIGLU_ALPHA = 1.702
SWIGLU_LIMIT = 7.0

LANE = 128
VMEM_LIMIT = 56 * 1024 * 1024


def _cparams(*sem):
    return pltpu.CompilerParams(dimension_semantics=sem, vmem_limit_bytes=VMEM_LIMIT)


def _dot(a, b):
    return jnp.dot(a, b, preferred_element_type=F32)


def _dot_nt(a, b):
    return lax.dot_general(a, b, (((1,), (1,)), ((), ())), preferred_element_type=F32)


def _rms(x, g):
    return x * lax.rsqrt(jnp.mean(x * x, axis=-1, keepdims=True) + NORM_EPS) * g


def _sigmoid(x):
    return 1.0 / (1.0 + jnp.exp(-x))


def _pack_bf16_pairs(x):
    half = x.shape[1] // 2
    bits = lax.bitcast_convert_type(x.astype(BF16).astype(F32), jnp.uint32)
    return (bits[:, :half] & jnp.uint32(0xFFFF0000)) | (bits[:, half:] >> 16)


def _unpack_bf16_pairs(p):
    hi = lax.bitcast_convert_type(p & jnp.uint32(0xFFFF0000), F32).astype(BF16)
    lo = lax.bitcast_convert_type(p << 16, F32).astype(BF16)
    return hi, lo


def _small_proj_kernel(x_ref, g_ref, w_ref, qg_ref, kvg_ref, cos_ref, sin_ref,
                       u_ref, cqn_ref, lat_ref, kr_ref, krb_ref, *, ql, kl, rope):
    u = _rms(x_ref[...], g_ref[...]).astype(BF16)
    u_ref[...] = u
    y = _dot(u, w_ref[...])
    cqn_ref[...] = _rms(y[:, :ql], qg_ref[...]).astype(BF16)
    lat_ref[...] = _rms(y[:, ql:ql + kl], kvg_ref[...])
    o = ql + kl
    kr = y[:, o:o + LANE] * cos_ref[...] + y[:, o + LANE:o + 2 * LANE] * sin_ref[...]
    kr_ref[...] = kr[:, :rope]
    krb_ref[...] = kr.astype(BF16)


def _small_proj(x2, g, w_small, qg, kvg, cos_t, sin_t, *, tm, ql, kl, rope):
    T, D = x2.shape
    n = w_small.shape[1]
    nblk = cos_t.shape[0] // tm
    row = lambda i: (i, 0)
    fix = lambda i: (0, 0)
    tab = lambda i: (i % nblk, 0)
    return pl.pallas_call(
        functools.partial(_small_proj_kernel, ql=ql, kl=kl, rope=rope),
        grid=(T // tm,),
        in_specs=[pl.BlockSpec((tm, D), row), pl.BlockSpec((1, D), fix), pl.BlockSpec((D, n), fix),
                  pl.BlockSpec((1, ql), fix), pl.BlockSpec((1, kl), fix),
                  pl.BlockSpec((tm, LANE), tab), pl.BlockSpec((tm, LANE), tab)],
        out_specs=[pl.BlockSpec((tm, D), row), pl.BlockSpec((tm, ql), row), pl.BlockSpec((tm, kl), row),
                   pl.BlockSpec((tm, rope), row), pl.BlockSpec((tm, LANE), row)],
        out_shape=[jax.ShapeDtypeStruct((T, D), BF16), jax.ShapeDtypeStruct((T, ql), BF16),
                   jax.ShapeDtypeStruct((T, kl), F32), jax.ShapeDtypeStruct((T, rope), F32),
                   jax.ShapeDtypeStruct((T, LANE), BF16)],
        compiler_params=_cparams("parallel"),
        name="small_proj",
    )(x2, g, w_small, qg, kvg, cos_t, sin_t)


def _big_proj_kernel(u_ref, *refs, nseg):
    u = u_ref[...]
    for s in range(nseg):
        refs[nseg + s][...] = _dot(u, refs[s][...]).astype(refs[nseg + s].dtype)


def _big_proj(u, ws, out_dtypes, *, tm, tn):
    T, D = u.shape
    n = ws[0].shape[1]
    nseg = len(ws)
    return pl.pallas_call(
        functools.partial(_big_proj_kernel, nseg=nseg),
        grid=(T // tm, n // tn),
        in_specs=[pl.BlockSpec((tm, D), lambda i, j: (i, 0))]
                 + [pl.BlockSpec((D, tn), lambda i, j: (0, j)) for _ in ws],
        out_specs=[pl.BlockSpec((tm, tn), lambda i, j: (i, j)) for _ in ws],
        out_shape=[jax.ShapeDtypeStruct((T, n), dt) for dt in out_dtypes],
        compiler_params=_cparams("parallel", "arbitrary"),
        name="big_proj",
    )(u, *ws)


def _q_proj_kernel(c_ref, wr_ref, ws_ref, cos_ref, sin_ref, q_ref, *, heads):
    c = c_ref[...]
    raw = _dot(c, wr_ref[...])
    sw = _dot(c, ws_ref[...])
    cos = cos_ref[...]
    sin = sin_ref[...]
    for h in range(heads):
        b = 2 * LANE * h
        q_ref[:, b:b + LANE] = raw[:, b:b + LANE].astype(BF16)
        r = raw[:, b + LANE:b + 2 * LANE] * cos + sw[:, LANE * h:LANE * (h + 1)] * sin
        q_ref[:, b + LANE:b + 2 * LANE] = r.astype(BF16)


def _q_proj(cqn, wq_raw, wq_sw, cos_t, sin_t, *, tm, heads):
    T, ql = cqn.shape
    nblk = cos_t.shape[0] // tm
    row = lambda i: (i, 0)
    fix = lambda i: (0, 0)
    tab = lambda i: (i % nblk, 0)
    n = wq_raw.shape[1]
    return pl.pallas_call(
        functools.partial(_q_proj_kernel, heads=heads),
        grid=(T // tm,),
        in_specs=[pl.BlockSpec((tm, ql), row), pl.BlockSpec(wq_raw.shape, fix), pl.BlockSpec(wq_sw.shape, fix),
                  pl.BlockSpec((tm, LANE), tab), pl.BlockSpec((tm, LANE), tab)],
        out_specs=pl.BlockSpec((tm, n), row),
        out_shape=jax.ShapeDtypeStruct((T, n), BF16),
        compiler_params=_cparams("parallel"),
        name="q_proj",
    )(cqn, wq_raw, wq_sw, cos_t, sin_t)


def _kv_proj_kernel(lat_ref, krb_ref, wk_ref, wv_ref, k_ref, v_ref, *, heads):
    lat = lat_ref[...].astype(BF16)
    kn = _dot(lat, wk_ref[...]).astype(BF16)
    v_ref[...] = _dot(lat, wv_ref[...]).astype(BF16)
    krb = krb_ref[...]
    for h in range(heads):
        b = 2 * LANE * h
        k_ref[:, b:b + LANE] = kn[:, LANE * h:LANE * (h + 1)]
        k_ref[:, b + LANE:b + 2 * LANE] = krb


def _kv_proj(lat, krb, wk, wv, *, tm, heads):
    T, kl = lat.shape
    row = lambda i: (i, 0)
    fix = lambda i: (0, 0)
    return pl.pallas_call(
        functools.partial(_kv_proj_kernel, heads=heads),
        grid=(T // tm,),
        in_specs=[pl.BlockSpec((tm, kl), row), pl.BlockSpec((tm, LANE), row),
                  pl.BlockSpec(wk.shape, fix), pl.BlockSpec(wv.shape, fix)],
        out_specs=[pl.BlockSpec((tm, 2 * LANE * heads), row), pl.BlockSpec((tm, wv.shape[1]), row)],
        out_shape=[jax.ShapeDtypeStruct((T, 2 * LANE * heads), BF16), jax.ShapeDtypeStruct((T, wv.shape[1]), BF16)],
        compiler_params=_cparams("parallel"),
        name="kv_proj",
    )(lat, krb, wk, wv)


def _attn_kernel(q_ref, k_ref, v_ref, o_ref, *, tq, tk, hpb):
    qi = pl.program_id(2)
    kd = 2 * LANE
    vd = v_ref.shape[2] // hpb
    qs = [q_ref[0, :, g * kd:(g + 1) * kd] for g in range(hpb)]

    def update(carry, s, v):
        m, l, acc = carry
        m_new = jnp.maximum(m, jnp.max(s, axis=-1, keepdims=True))
        alpha = jnp.exp(m - m_new)
        p = jnp.exp(s - m_new)
        l = alpha * l + jnp.sum(p, axis=-1, keepdims=True)
        acc = alpha * acc + _dot(p.astype(BF16), v)
        return m_new, l, acc

    def block(carry, start, mask, r0):
        new = []
        for g in range(hpb):
            k = k_ref[0, pl.ds(start, tk), g * kd:(g + 1) * kd]
            v = v_ref[0, pl.ds(start, tk), g * vd:(g + 1) * vd]
            s = _dot_nt(qs[g][r0:], k)
            if mask is not None:
                s = jnp.where(mask, s, NEG_INF)
            upd = update(tuple(a[r0:] for a in carry[g]), s, v)
            new.append(tuple(jnp.concatenate([a[:r0], b], axis=0) if r0 else b for a, b in zip(carry[g], upd)))
        return tuple(new)

    one = (jnp.full((tq, 1), NEG_INF, F32), jnp.zeros((tq, 1), F32), jnp.zeros((tq, vd), F32))
    nfull = qi * (tq // tk)
    carry = lax.fori_loop(0, nfull, lambda j, c: block(c, pl.multiple_of(j * tk, tk), None, 0), (one,) * hpb)
    qc = lax.broadcasted_iota(jnp.int32, (tq, tk), 0) // CHUNK
    kc = lax.broadcasted_iota(jnp.int32, (tq, tk), 1) // CHUNK
    for d in range(tq // tk):
        mask = (kc + d * (tk // CHUNK) <= qc)[d * tk:]
        carry = block(carry, pl.multiple_of((nfull + d) * tk, tk), mask, d * tk)
    for g in range(hpb):
        m, l, acc = carry[g]
        o_ref[0, :, g * vd:(g + 1) * vd] = (acc / l).astype(BF16)


def _attention(qf, kf, v, *, heads, tq, tk, hpb):
    B, S, _ = qf.shape
    vd = v.shape[2] // heads
    return pl.pallas_call(
        functools.partial(_attn_kernel, tq=tq, tk=tk, hpb=hpb),
        grid=(B, heads // hpb, S // tq),
        in_specs=[pl.BlockSpec((1, tq, 2 * LANE * hpb), lambda b, h, i: (b, i, h)),
                  pl.BlockSpec((1, S, 2 * LANE * hpb), lambda b, h, i: (b, 0, h)),
                  pl.BlockSpec((1, S, vd * hpb), lambda b, h, i: (b, 0, h))],
        out_specs=pl.BlockSpec((1, tq, vd * hpb), lambda b, h, i: (b, i, h)),
        out_shape=jax.ShapeDtypeStruct((B, S, heads * vd), BF16),
        compiler_params=_cparams("parallel", "parallel", "arbitrary"),
        name="mla_prompt_attn",
    )(qf, kf, v)


def _attn_sample_kernel(q_ref, cl_ref, ck_ref, ln_ref, kn_ref, wuk_ref, wuv_ref, o_ref,
                        ql_ref, qr_ref, ol_ref, *, heads, rope):
    sd = q_ref.shape[1]
    for h in range(heads):
        b = 2 * LANE * h
        ql_ref[h * sd:(h + 1) * sd, :] = _dot(q_ref[0, :, b:b + LANE], wuk_ref[h]).astype(BF16)
        qr_ref[h * sd:(h + 1) * sd, :] = q_ref[0, :, b + LANE:b + 2 * LANE]
    ql = ql_ref[...]
    qr = qr_ref[...][:, :rope]
    cl = cl_ref[0].astype(BF16)
    ck = ck_ref[0].astype(BF16)
    ln = ln_ref[0].astype(BF16)
    kn = kn_ref[0].astype(BF16)
    s_past = _dot_nt(ql, cl) + _dot_nt(qr, ck)
    s_new = _dot_nt(ql, ln) + _dot_nt(qr, kn)
    m = jnp.maximum(jnp.max(s_past, axis=-1, keepdims=True), jnp.max(s_new, axis=-1, keepdims=True))
    p_past = jnp.exp(s_past - m)
    p_new = jnp.exp(s_new - m)
    l = jnp.sum(p_past, axis=-1, keepdims=True) + jnp.sum(p_new, axis=-1, keepdims=True)
    o_lat = (_dot(p_past.astype(BF16), cl) + _dot(p_new.astype(BF16), ln)) / l
    ol_ref[...] = o_lat.astype(BF16)
    vd = wuv_ref.shape[2]
    for h in range(heads):
        o_ref[0, :, h * vd:(h + 1) * vd] = _dot(ol_ref[h * sd:(h + 1) * sd, :], wuv_ref[h]).astype(BF16)


def _attention_sample(qf, cache_lat, cache_kr, lat_new, kr_new, wuk_t, wuv, *, heads):
    B, sd, _ = qf.shape
    P, kl = cache_lat.shape[1:]
    rope = cache_kr.shape[2]
    vd = wuv.shape[2]
    b3 = lambda b: (b, 0, 0)
    fix = lambda b: (0, 0, 0)
    return pl.pallas_call(
        functools.partial(_attn_sample_kernel, heads=heads, rope=rope),
        grid=(B,),
        in_specs=[pl.BlockSpec((1, sd, qf.shape[2]), b3), pl.BlockSpec((1, P, kl), b3), pl.BlockSpec((1, P, rope), b3),
                  pl.BlockSpec((1, sd, kl), b3), pl.BlockSpec((1, sd, rope), b3),
                  pl.BlockSpec(wuk_t.shape, fix), pl.BlockSpec(wuv.shape, fix)],
        out_specs=pl.BlockSpec((1, sd, heads * vd), b3),
        out_shape=jax.ShapeDtypeStruct((B, sd, heads * vd), BF16),
        scratch_shapes=[pltpu.VMEM((heads * sd, kl), BF16), pltpu.VMEM((heads * sd, LANE), BF16),
                        pltpu.VMEM((heads * sd, kl), BF16)],
        compiler_params=_cparams("parallel"),
        name="mla_sample_attn",
    )(qf, cache_lat, cache_kr, lat_new, kr_new, wuk_t, wuv)


def _hgrn_kernel(*refs, heads, has_state):
    if has_state:
        hq_ref, hf_ref, hi_ref, hg_ref, lb_ref, g_ref, s0_ref, o_ref, sn_ref, st_ref = refs
    else:
        hq_ref, hf_ref, hi_ref, hg_ref, lb_ref, g_ref, o_ref, sn_ref, st_ref = refs
    c = pl.program_id(1)
    L = hq_ref.shape[1]
    dk = st_ref.shape[2]

    @pl.when(c == 0)
    def _():
        if has_state:
            for h in range(heads):
                st_ref[h] = s0_ref[0, h].T
        else:
            st_ref[...] = jnp.zeros_like(st_ref)

    row = lax.broadcasted_iota(jnp.int32, (L, L), 0)
    col = lax.broadcasted_iota(jnp.int32, (L, L), 1)
    causal = col <= row
    tri = jnp.where(causal, 1.0, 0.0).astype(BF16)
    lb = lb_ref[...]
    f = lb + (1.0 - lb) * _sigmoid(hf_ref[0])
    kk = 1.0 - f
    logf = jnp.log(f)
    hi = logf.astype(BF16)
    r1 = logf - hi.astype(F32)
    mid = r1.astype(BF16)
    lo = (r1 - mid.astype(F32)).astype(BF16)
    cum = _dot(tri, hi) + _dot(tri, mid) + _dot(tri, lo)
    tot = cum[L - 1:L, :]
    q_dec = (hq_ref[0].astype(F32) * jnp.exp(cum)).astype(BF16)
    k_dec = (kk * jnp.exp(-cum)).astype(BF16)
    k2 = (kk * jnp.exp(tot - cum)).astype(BF16)
    decay = jnp.exp(tot)
    vb = hi_ref[0]
    v32 = vb.astype(F32)
    sls = [slice(h * dk, (h + 1) * dk) for h in range(heads)]
    atts = [jnp.where(causal, _dot_nt(q_dec[:, sl], k_dec[:, sl]), 0.0).astype(BF16) for sl in sls]
    sts = [st_ref[h] for h in range(heads)]
    outs = [_dot_nt(q_dec[:, sl], sts[h].astype(BF16)) + _dot(atts[h], vb[:, sl]) for h, sl in enumerate(sls)]
    for h, sl in enumerate(sls):
        st_ref[h] = decay[:, sl] * sts[h] + _dot(v32[:, sl].T.astype(BF16), k2[:, sl])
    g = g_ref[...]
    for h, sl in enumerate(sls):
        o = outs[h]
        on = o * lax.rsqrt(jnp.mean(o * o, axis=-1, keepdims=True) + NORM_EPS) * g
        hg = hg_ref[0, :, sl].astype(F32)
        o_ref[0, :, sl] = (on * (hg * _sigmoid(hg))).astype(BF16)

    @pl.when(c == pl.num_programs(1) - 1)
    def _():
        for h in range(heads):
            sn_ref[0, h] = st_ref[h].T


def _hgrn(hq, hf, hi, hg, lb, g, s0, *, heads, L):
    B, S, F = hq.shape
    dk = F // heads
    blk = lambda b, c: (b, c, 0)
    fix2 = lambda b, c: (0, 0)
    st4 = lambda b, c: (b, 0, 0, 0)
    in_specs = [pl.BlockSpec((1, L, F), blk)] * 4 + [pl.BlockSpec((1, F), fix2), pl.BlockSpec((1, dk), fix2)]
    args = [hq, hf, hi, hg, lb, g]
    if s0 is not None:
        in_specs.append(pl.BlockSpec((1, heads, dk, dk), st4))
        args.append(s0)
    return pl.pallas_call(
        functools.partial(_hgrn_kernel, heads=heads, has_state=s0 is not None),
        grid=(B, S // L),
        in_specs=in_specs,
        out_specs=[pl.BlockSpec((1, L, F), blk), pl.BlockSpec((1, heads, dk, dk), st4)],
        out_shape=[jax.ShapeDtypeStruct((B, S, F), BF16), jax.ShapeDtypeStruct((B, heads, dk, dk), F32)],
        scratch_shapes=[pltpu.VMEM((heads, dk, dk), F32)],
        compiler_params=_cparams("parallel", "arbitrary"),
        name="hgrn",
    )(*args)


def _out_proj_kernel(oa_ref, ob_ref, ga_ref, gb_ref, x_ref, w_ref, o_ref, *, kc):
    acc = x_ref[...]
    for k in range(w_ref.shape[0] // kc):
        sl = slice(k * kc, (k + 1) * kc)
        ga = _sigmoid(ga_ref[:, sl].astype(F32))
        gb = _sigmoid(gb_ref[:, sl].astype(F32))
        m = (ga * oa_ref[:, sl].astype(F32) + gb * ob_ref[:, sl].astype(F32)).astype(BF16)
        acc = acc + _dot(m, w_ref[sl, :])
    o_ref[...] = acc


def _out_proj(oa, ob, ga, gb, x2, w_o, *, tm):
    T, D = x2.shape
    row = lambda i: (i, 0)
    return pl.pallas_call(
        functools.partial(_out_proj_kernel, kc=512),
        grid=(T // tm,),
        in_specs=[pl.BlockSpec((tm, D), row)] * 5 + [pl.BlockSpec((D, D), lambda i: (0, 0))],
        out_specs=pl.BlockSpec((tm, D), row),
        out_shape=jax.ShapeDtypeStruct((T, D), F32),
        compiler_params=_cparams("parallel"),
        name="out_proj",
    )(oa, ob, ga, gb, x2, w_o)


def _split3(x):
    hi = x.astype(BF16)
    r = x - hi.astype(F32)
    mid = r.astype(BF16)
    lo = (r - mid.astype(F32)).astype(BF16)
    return hi, mid, lo


def _router_kernel(hp_ref, hs_ref, g_ref, w3_ref, b_ref, u_ref, idx_ref, gate_ref, cnt_ref, run_ref, *, n_exp, n_p):
    step = pl.program_id(0)

    @pl.when(step == 0)
    def _():
        run_ref[...] = jnp.zeros_like(run_ref)

    h = jnp.where(step < n_p, hp_ref[...], hs_ref[...])
    u = _rms(h, g_ref[...])
    u_ref[...] = _pack_bf16_pairs(u)
    u_hi, u_mid, u_lo = _split3(u)
    e1, e2 = n_exp, 2 * n_exp
    p_hi = _dot(u_hi, w3_ref[...])
    p_mid = _dot(u_mid, w3_ref[:, :e2])
    p_lo = _dot(u_lo, w3_ref[:, :e1])
    logits = (p_hi[:, :e1] + (p_hi[:, e1:e2] + p_mid[:, :e1]) + (p_hi[:, e2:] + p_mid[:, e1:] + p_lo)) + b_ref[...]
    tm = logits.shape[0]
    lane_e = lax.broadcasted_iota(jnp.int32, (tm, n_exp), 1)
    lane_o = lax.broadcasted_iota(jnp.int32, (tm, LANE), 1)
    idx_out = jnp.zeros((tm, LANE), jnp.int32)
    val_out = jnp.zeros((tm, LANE), F32)
    vals, picks = [], []
    for k in range(TOP_K):
        m = jnp.max(logits, axis=-1, keepdims=True)
        i = jnp.min(jnp.where(logits == m, lane_e, n_exp), axis=-1, keepdims=True)
        pick = lane_e == i
        logits = jnp.where(pick, -jnp.inf, logits)
        vals.append(m)
        picks.append(pick)
        idx_out = jnp.where(lane_o == k, i, idx_out)
    es = [jnp.exp(v - vals[0]) for v in vals]
    den = es[0]
    for e in es[1:]:
        den = den + e
    for k in range(TOP_K):
        val_out = jnp.where(lane_o == k, es[k] / den, val_out)
    multi = jnp.zeros((tm, n_exp), F32)
    for pick in picks:
        multi = multi + jnp.where(pick, 1.0, 0.0)
    earlier = lax.broadcasted_iota(jnp.int32, (tm, tm), 1) < lax.broadcasted_iota(jnp.int32, (tm, tm), 0)
    before = _dot(jnp.where(earlier, 1.0, 0.0).astype(BF16), multi.astype(BF16)) + run_ref[...]
    for k in range(TOP_K):
        rank = jnp.sum(jnp.where(picks[k], before, 0.0), axis=-1, keepdims=True).astype(jnp.int32)
        idx_out = jnp.where(lane_o == TOP_K + k, rank, idx_out)
    run_ref[...] += jnp.sum(multi, axis=0, keepdims=True)
    cnt_ref[...] = run_ref[...]
    idx_ref[...] = idx_out
    gate_ref[...] = val_out


def _two_group_specs(shape, n_p):
    return [pl.BlockSpec(shape, lambda i: (jnp.minimum(i, n_p - 1), 0)),
            pl.BlockSpec(shape, lambda i: (jnp.maximum(i - n_p, 0), 0))]


def _router(h2p, h2s, g, w_r, b_r, *, tm):
    (Tp, D), Ts = h2p.shape, h2s.shape[0]
    T = Tp + Ts
    n_exp = w_r.shape[1]
    w3 = jnp.concatenate(_split3(w_r), axis=1)
    row = lambda i: (i, 0)
    fix = lambda i: (0, 0)
    return pl.pallas_call(
        functools.partial(_router_kernel, n_exp=n_exp, n_p=Tp // tm),
        grid=(T // tm,),
        in_specs=_two_group_specs((tm, D), Tp // tm)
                 + [pl.BlockSpec((1, D), fix), pl.BlockSpec((D, 3 * n_exp), fix), pl.BlockSpec((1, n_exp), fix)],
        out_specs=[pl.BlockSpec((tm, D // 2), row), pl.BlockSpec((tm, LANE), row), pl.BlockSpec((tm, LANE), row),
                   pl.BlockSpec((1, n_exp), fix)],
        out_shape=[jax.ShapeDtypeStruct((T, D // 2), jnp.uint32), jax.ShapeDtypeStruct((T, LANE), jnp.int32),
                   jax.ShapeDtypeStruct((T, LANE), F32), jax.ShapeDtypeStruct((1, n_exp), F32)],
        scratch_shapes=[pltpu.VMEM((1, n_exp), F32)],
        compiler_params=_cparams("arbitrary"),
        name="router",
    )(h2p, h2s, g, w3, b_r)


def _row_copy(src_ref, dst_ref, src_row, dst_row, sem):
    return pltpu.make_async_copy(src_ref.at[pl.ds(src_row, 1)], dst_ref.at[pl.ds(dst_row, 1)], sem)


def _dispatch_kernel(zs_ref, ze_ref, dest_hbm, u_ref, xs_hbm, idx_smem, zrow_ref, sem_idx, sem, sem_z, *, R, n_exp):
    i = pl.program_id(0)
    n = pl.num_programs(0)
    slot = i % 2
    N = R * TOP_K

    def idx_copy(step, s):
        return pltpu.make_async_copy(dest_hbm.at[step], idx_smem.at[pl.ds(pl.multiple_of(s * N, N), N)],
                                     sem_idx.at[s])

    def start_rows(s):
        def body(r, c):
            for k in range(TOP_K):
                _row_copy(u_ref, xs_hbm, r, idx_smem[s * N + r * TOP_K + k], sem).start()
            return c
        lax.fori_loop(0, R, body, 0, unroll=8)

    def wait_rows():
        def body(r, c):
            for k in range(TOP_K):
                _row_copy(u_ref, xs_hbm, 0, 0, sem).wait()
            return c
        lax.fori_loop(0, R, body, 0, unroll=8)

    def zero_rows(start):
        def per_expert(e, c):
            def body(r, c2):
                cp = _row_copy(zrow_ref, xs_hbm, 0, r, sem_z)
                cp.start() if start else cp.wait()
                return c2
            return lax.fori_loop(zs_ref[e], ze_ref[e], body, c)
        lax.fori_loop(0, n_exp, per_expert, 0)

    @pl.when(i == 0)
    def _():
        zrow_ref[...] = jnp.zeros_like(zrow_ref)
        idx_copy(0, 0).start()

    idx_copy(i, slot).wait()

    @pl.when(i + 1 < n)
    def _():
        idx_copy(i + 1, 1 - slot).start()

    start_rows(slot)

    @pl.when(i == n - 1)
    def _():
        zero_rows(True)

    wait_rows()

    @pl.when(i == n - 1)
    def _():
        zero_rows(False)


def _dispatch(zs, ze, dest2, u, *, P, R):
    n = dest2.shape[0]
    D = u.shape[1]
    grid_spec = pltpu.PrefetchScalarGridSpec(
        num_scalar_prefetch=2,
        grid=(n,),
        in_specs=[pl.BlockSpec(memory_space=pl.ANY), pl.BlockSpec((R, D), lambda i, zs, ze: (i, 0))],
        out_specs=pl.BlockSpec(memory_space=pl.ANY),
        scratch_shapes=[pltpu.SMEM((2 * R * TOP_K,), jnp.int32), pltpu.VMEM((8, D), u.dtype),
                        pltpu.SemaphoreType.DMA((2,)), pltpu.SemaphoreType.DMA, pltpu.SemaphoreType.DMA],
    )
    return pl.pallas_call(
        functools.partial(_dispatch_kernel, R=R, n_exp=zs.shape[0]),
        grid_spec=grid_spec,
        out_shape=jax.ShapeDtypeStruct((P, D), u.dtype),
        compiler_params=_cparams("arbitrary"),
        name="moe_dispatch",
    )(zs, ze, dest2, u)


def _expert_kernel(te_ref, nv_ref, x_ref, wg_ref, wu_ref, bg_ref, bu_ref, wd_ref, bd_ref, y_ref, xb_ref):
    i = pl.program_id(0)
    c = pl.program_id(1)
    half = x_ref.shape[1]

    @pl.when(c == 0)
    def _():
        y_ref[...] = jnp.broadcast_to(bd_ref[0], y_ref.shape)
        hi, lo = _unpack_bf16_pairs(x_ref[...])
        xb_ref[:, :half] = hi
        xb_ref[:, half:] = lo

    @pl.when(nv_ref[i] > 0)
    def _():
        xb = xb_ref[...]
        gate = jnp.minimum(_dot(xb, wg_ref[0]) + bg_ref[0], SWIGLU_LIMIT)
        up = jnp.clip(_dot(xb, wu_ref[0]) + bu_ref[0], -SWIGLU_LIMIT, SWIGLU_LIMIT)
        act = gate * _sigmoid(SWIGLU_ALPHA * gate) * (up + 1.0)
        y_ref[...] += _dot(act.astype(BF16), wd_ref[0])


def _experts(tile_e, tile_nv, xs, w_gu, b_gu, w_dn, b_dn, *, tm, fc):
    P, half = xs.shape
    D = 2 * half
    E, _, ff2 = w_gu.shape
    ff = ff2 // 2
    nc = ff // fc
    n_tiles = P // tm

    def cc(i, c, nv):
        return jnp.where(nv[i] > 0, c, nc - 1)

    grid_spec = pltpu.PrefetchScalarGridSpec(
        num_scalar_prefetch=2,
        grid=(n_tiles, nc),
        in_specs=[
            pl.BlockSpec((tm, half), lambda i, c, te, nv: (jnp.where(nv[i] > 0, i, 0), 0)),
            pl.BlockSpec((1, D, fc), lambda i, c, te, nv: (te[i], 0, cc(i, c, nv))),
            pl.BlockSpec((1, D, fc), lambda i, c, te, nv: (te[i], 0, nc + cc(i, c, nv))),
            pl.BlockSpec((1, 1, fc), lambda i, c, te, nv: (te[i], 0, cc(i, c, nv))),
            pl.BlockSpec((1, 1, fc), lambda i, c, te, nv: (te[i], 0, nc + cc(i, c, nv))),
            pl.BlockSpec((1, fc, D), lambda i, c, te, nv: (te[i], cc(i, c, nv), 0)),
            pl.BlockSpec((1, 1, D), lambda i, c, te, nv: (te[i], 0, 0)),
        ],
        out_specs=pl.BlockSpec((tm, D), lambda i, c, te, nv: (i, 0)),
        scratch_shapes=[pltpu.VMEM((tm, D), BF16)],
    )
    return pl.pallas_call(
        _expert_kernel,
        grid_spec=grid_spec,
        out_shape=jax.ShapeDtypeStruct((P, D), F32),
        compiler_params=_cparams("arbitrary", "arbitrary"),
        name="moe_experts",
    )(tile_e, tile_nv, xs, w_gu, w_gu, b_gu, b_gu, w_dn, b_dn)


def _combine_kernel(dest_hbm, gate_ref, hp_ref, hs_ref, ys_hbm, g_ref, yp_ref, ysm_ref, buf_ref, idx_smem, sem_idx,
                    sem, *, n_p):
    i = pl.program_id(0)
    n = pl.num_programs(0)
    R = hp_ref.shape[0]
    slot = i % 2
    N = R * TOP_K

    def idx_copy(step, s):
        return pltpu.make_async_copy(dest_hbm.at[step], idx_smem.at[pl.ds(pl.multiple_of(s * N, N), N)],
                                     sem_idx.at[s])

    def start_rows(s):
        def body(r, c):
            for k in range(TOP_K):
                _row_copy(ys_hbm, buf_ref.at[s, k], idx_smem[s * N + r * TOP_K + k], r, sem.at[s]).start()
            return c
        lax.fori_loop(0, R, body, 0, unroll=8)

    def wait_rows(s):
        def body(r, c):
            for k in range(TOP_K):
                _row_copy(ys_hbm, buf_ref.at[s, k], 0, r, sem.at[s]).wait()
            return c
        lax.fori_loop(0, R, body, 0, unroll=8)

    @pl.when(i == 0)
    def _():
        cp = idx_copy(0, 0)
        cp.start()
        cp.wait()
        start_rows(0)

        @pl.when(n > 1)
        def _():
            idx_copy(1, 1).start()

    @pl.when(i + 1 < n)
    def _():
        idx_copy(i + 1, 1 - slot).wait()
        start_rows(1 - slot)

    @pl.when(i + 2 < n)
    def _():
        idx_copy(i + 2, slot).start()

    wait_rows(slot)
    is_p = i < n_p
    h = jnp.where(is_p, hp_ref[...], hs_ref[...])
    gate = gate_ref[...]
    for k in range(TOP_K):
        h = h + gate[:, k:k + 1] * buf_ref[slot, k]
    y = _rms(h, g_ref[...])

    @pl.when(is_p)
    def _():
        yp_ref[...] = y

    @pl.when(jnp.logical_not(is_p))
    def _():
        ysm_ref[...] = y


def _combine(dest2, gate, h2p, h2s, ys, g, *, R):
    (Tp, D), Ts = h2p.shape, h2s.shape[0]
    n_p = Tp // R
    row = lambda i: (i, 0)
    return pl.pallas_call(
        functools.partial(_combine_kernel, n_p=n_p),
        grid=((Tp + Ts) // R,),
        in_specs=[pl.BlockSpec(memory_space=pl.ANY), pl.BlockSpec((R, LANE), row)] + _two_group_specs((R, D), n_p)
                 + [pl.BlockSpec(memory_space=pl.ANY), pl.BlockSpec((1, D), lambda i: (0, 0))],
        out_specs=_two_group_specs((R, D), n_p),
        out_shape=[jax.ShapeDtypeStruct((Tp, D), F32), jax.ShapeDtypeStruct((Ts, D), F32)],
        scratch_shapes=[pltpu.VMEM((2, TOP_K, R, D), F32), pltpu.SMEM((2 * R * TOP_K,), jnp.int32),
                        pltpu.SemaphoreType.DMA((2,)), pltpu.SemaphoreType.DMA((2,))],
        compiler_params=_cparams("arbitrary"),
        name="moe_combine",
    )(dest2, gate, h2p, h2s, ys, g)


def _rope_tables(pos, rope, rows):
    half = rope // 2
    inv = ROPE_BASE ** (-jnp.arange(half, dtype=F32) / half)
    ang = pos.astype(F32)[:, None] * inv[None, :]
    cos, sin = jnp.cos(ang), jnp.sin(ang)
    z = jnp.zeros((pos.shape[0], LANE - rope), F32)
    cos_t = jnp.concatenate([cos, cos, z], axis=1)
    sin_t = jnp.concatenate([-sin, sin, z], axis=1)
    rep = max(1, rows // pos.shape[0])
    return jnp.tile(cos_t, (rep, 1)), jnp.tile(sin_t, (rep, 1))


def _swap_halves(w):
    half = w.shape[-1] // 2
    return jnp.concatenate([w[..., half:], w[..., :half]], axis=-1)


def _mixer(x, pos, past, wts, *, tm):
    B, S, D = x.shape
    T = B * S
    heads, ql, kl, rope = wts["heads"], wts["ql"], wts["kl"], wts["rope"]
    x2 = x.reshape(T, D)
    cos_t, sin_t = _rope_tables(pos, rope, tm)
    u, cqn, lat, kr, krb = _small_proj(x2, wts["norm_mix_g"], wts["w_small"], wts["q_norm_g"], wts["kv_norm_g"],
                                       cos_t, sin_t, tm=tm, ql=ql, kl=kl, rope=rope)
    hq, hf, hi, hg, ga, gb = _big_proj(u, wts["w_big"], (BF16, F32, BF16, BF16, BF16, BF16), tm=tm, tn=512)
    qf = _q_proj(cqn, wts["wq_raw"], wts["wq_sw"], cos_t, sin_t, tm=tm, heads=heads)
    F = hq.shape[1]
    r3 = lambda a: a.reshape(B, S, a.shape[1])
    if past is None:
        kf, v = _kv_proj(lat, krb, wts["w_uk"], wts["w_uv"], tm=tm, heads=heads)
        o_a = _attention(r3(qf), r3(kf), r3(v), heads=heads, tq=min(2048, S), tk=min(1024, S), hpb=2)
        o_b, st = _hgrn(r3(hq), r3(hf), r3(hi), r3(hg), wts["lb"], wts["hgrn_g"], None, heads=wts["hg_heads"],
                        L=CHUNK)
    else:
        cache_lat, cache_kr, state = past
        o_a = _attention_sample(r3(qf), cache_lat, cache_kr, r3(lat), r3(kr), wts["w_uk_t"], wts["w_uv_h"],
                                heads=heads)
        o_b, st = _hgrn(r3(hq), r3(hf), r3(hi), r3(hg), wts["lb"], wts["hgrn_g"], state, heads=wts["hg_heads"], L=S)
    h2 = _out_proj(o_a.reshape(T, D), o_b.reshape(T, F), ga, gb, x2, wts["w_o"], tm=256)
    return h2, lat.reshape(B, S, kl), kr.reshape(B, S, rope), st


def kernel(x_prompt, x_sample, cache_mla_latent, cache_mla_krope, state_hgrn, norm_mix_g, w_in, q_norm_g, w_uq,
           kv_norm_g, w_uk, w_uv, hgrn_lb_raw, hgrn_norm_g, w_o, norm_ffn_g, w_router, b_router, w_gate_up,
           b_gate_up, w_down, b_down, norm_final_g):
    depth = w_in.shape[0]
    assert depth == 1, "single-layer kernel"
    B, S, D = x_prompt.shape
    Bd, Sd, _ = x_sample.shape
    past_len = cache_mla_latent.shape[2]
    ql, heads, qk = w_uq.shape[1:]
    kl, _, nope = w_uk.shape[1:]
    vd = w_uv.shape[3]
    rope = qk - nope
    hg_heads, dk, dv = state_hgrn.shape[2:]
    n_exp, _, ff2 = w_gate_up.shape[1:]
    assert nope == LANE and vd == LANE and dk == LANE and dv == LANE and rope <= LANE
    hf_dim = hg_heads * dk
    scale = 1.0 / math.sqrt(nope + rope)
    l = 0

    wi = w_in[l]
    o_cq, o_kv, o_pe, o_hq = 0, ql, ql + kl, ql + kl + rope
    k_pe = wi[:, o_pe:o_pe + rope]
    zpad = jnp.zeros((D, LANE - rope), F32)
    w_small = jnp.concatenate([wi[:, o_cq:o_pe], k_pe, zpad, _swap_halves(k_pe), zpad], axis=1).astype(BF16)
    segs = [hf_dim, hf_dim, D, D, D, D]
    offs = [o_hq]
    for s_ in segs[:-1]:
        offs.append(offs[-1] + s_)
    w_big = [wi[:, o:o + s_].astype(BF16) for o, s_ in zip(offs, segs)]
    wq = w_uq[l] * scale
    zq = jnp.zeros((ql, heads, LANE - rope), F32)
    wq_raw = jnp.concatenate([wq[..., :nope], wq[..., nope:], zq], axis=-1).reshape(ql, heads * 2 * LANE).astype(BF16)
    wq_sw = jnp.concatenate([_swap_halves(wq[..., nope:]), zq], axis=-1).reshape(ql, heads * LANE).astype(BF16)
    lb_all = jnp.cumsum(jax.nn.softmax(hgrn_lb_raw.astype(F32), axis=0), axis=0)
    wts = dict(
        heads=heads, ql=ql, kl=kl, rope=rope, hg_heads=hg_heads,
        norm_mix_g=norm_mix_g[l][None], w_small=w_small, q_norm_g=q_norm_g[l][None], kv_norm_g=kv_norm_g[l][None],
        w_big=w_big, wq_raw=wq_raw, wq_sw=wq_sw,
        w_uk=w_uk[l].reshape(kl, heads * nope).astype(BF16), w_uv=w_uv[l].reshape(kl, heads * vd).astype(BF16),
        w_uk_t=jnp.transpose(w_uk[l], (1, 2, 0)).astype(BF16), w_uv_h=jnp.transpose(w_uv[l], (1, 0, 2)).astype(BF16),
        lb=lb_all[l][None], hgrn_g=hgrn_norm_g[l][None], w_o=w_o[l].astype(BF16),
    )

    Tp, Ts = B * S, Bd * Sd
    tm = 512
    assert S % tm == 0 or tm % S == 0
    assert Tp % tm == 0 and Ts % tm == 0 and tm % Sd == 0
    h2p, lat_p, kr_p, st_p = _mixer(x_prompt, jnp.arange(S), None, wts, tm=tm)
    past = (cache_mla_latent[l], cache_mla_krope[l], state_hgrn[l])
    h2s, lat_s, kr_s, st_s = _mixer(x_sample, past_len + jnp.arange(Sd), past, wts, tm=tm)

    T_all = Tp + Ts
    u_all, idx, gate, cnt = _router(h2p, h2s, norm_ffn_g[l][None], w_router[l], b_router[l][None], tm=tm)

    te_rows = 512
    A = T_all * TOP_K
    counts = cnt[0].astype(jnp.int32)
    padded = (counts + te_rows - 1) // te_rows * te_rows
    pad_end = jnp.cumsum(padded)
    pad_start = pad_end - padded
    used_end = pad_start + counts
    e_sel, rank = idx[:, :TOP_K], idx[:, TOP_K:2 * TOP_K]
    onehot = e_sel[:, :, None] == jnp.arange(n_exp, dtype=jnp.int32)[None, None, :]
    dest = (rank + jnp.sum(jnp.where(onehot, pad_start[None, None, :], 0), axis=-1)).reshape(A)
    n_tiles = -(-A // te_rows) + n_exp
    P = n_tiles * te_rows
    tile_start = jnp.arange(n_tiles, dtype=jnp.int32) * te_rows
    tile_e = jnp.minimum(jnp.searchsorted(pad_end, tile_start, side="right"), n_exp - 1).astype(jnp.int32)
    tile_nv = jnp.clip(used_end[tile_e] - tile_start, 0, te_rows).astype(jnp.int32)
    tile_nv = jnp.where(tile_start < pad_end[-1], tile_nv, 0)
    last_e = tile_e[jnp.maximum(jnp.sum((tile_start < pad_end[-1]).astype(jnp.int32)) - 1, 0)]
    tile_e = jnp.where(tile_nv > 0, tile_e, last_e)

    dR = 512
    xs = _dispatch(used_end.astype(jnp.int32), pad_end.astype(jnp.int32),
                   dest.reshape(T_all // dR, dR * TOP_K), u_all, P=P, R=dR)
    ys = _experts(tile_e, tile_nv, xs, w_gate_up[l].astype(BF16), b_gate_up[l][:, None, :],
                  w_down[l].astype(BF16), b_down[l][:, None, :], tm=te_rows, fc=1024)

    cR = 256
    y_p, y_s = _combine(dest.reshape(T_all // cR, cR * TOP_K), gate, h2p, h2s, ys, norm_final_g[None], R=cR)

    return (y_p.reshape(B, S, D), y_s.reshape(Bd, Sd, D), lat_p[None], kr_p[None], st_p[None],
            lat_s[None], kr_s[None], st_s[None])
```

```python
import functools
import math

import jax
import jax.numpy as jnp
from jax import lax
from jax.experimental import pallas as pl
from jax.experimental.pallas import tpu as pltpu

F32 = jnp.float32
BF16 = jnp.bfloat16

CHUNK = 64
NORM_EPS = 1e-6
ROPE_BASE = 10000.0
NEG_INF = -1e30
TOP_K = 4
SWIGLU_ALPHA = 1.702
SWIGLU_LIMIT = 7.0

LANE = 128
VMEM_LIMIT = 56 * 1024 * 1024


def _cparams(*sem):
    return pltpu.CompilerParams(dimension_semantics=sem, vmem_limit_bytes=VMEM_LIMIT)


def _dot(a, b):
    return jnp.dot(a, b, preferred_element_type=F32)


def _dot_nt(a, b):
    return lax.dot_general(a, b, (((1,), (1,)), ((), ())), preferred_element_type=F32)


def _rms(x, g):
    return x * lax.rsqrt(jnp.mean(x * x, axis=-1, keepdims=True) + NORM_EPS) * g


def _sigmoid(x):
    return 1.0 / (1.0 + jnp.exp(-x))


def _pack_bf16_pairs(x):
    half = x.shape[1] // 2
    bits = lax.bitcast_convert_type(x.astype(BF16).astype(F32), jnp.uint32)
    return (bits[:, :half] & jnp.uint32(0xFFFF0000)) | (bits[:, half:] >> 16)


def _unpack_bf16_pairs(p):
    hi = lax.bitcast_convert_type(p & jnp.uint32(0xFFFF0000), F32).astype(BF16)
    lo = lax.bitcast_convert_type(p << 16, F32).astype(BF16)
    return hi, lo


def _small_proj_kernel(x_ref, g_ref, w_ref, qg_ref, kvg_ref, cos_ref, sin_ref,
                       u_ref, cqn_ref, lat_ref, kr_ref, krb_ref, *, ql, kl, rope):
    u = _rms(x_ref[...], g_ref[...]).astype(BF16)
    u_ref[...] = u
    y = _dot(u, w_ref[...])
    cqn_ref[...] = _rms(y[:, :ql], qg_ref[...]).astype(BF16)
    lat_ref[...] = _rms(y[:, ql:ql + kl], kvg_ref[...])
    o = ql + kl
    kr = y[:, o:o + LANE] * cos_ref[...] + y[:, o + LANE:o + 2 * LANE] * sin_ref[...]
    kr_ref[...] = kr[:, :rope]
    krb_ref[...] = kr.astype(BF16)


def _small_proj(x2, g, w_small, qg, kvg, cos_t, sin_t, *, tm, ql, kl, rope):
    T, D = x2.shape
    n = w_small.shape[1]
    nblk = cos_t.shape[0] // tm
    row = lambda i: (i, 0)
    fix = lambda i: (0, 0)
    tab = lambda i: (i % nblk, 0)
    return pl.pallas_call(
        functools.partial(_small_proj_kernel, ql=ql, kl=kl, rope=rope),
        grid=(T // tm,),
        in_specs=[pl.BlockSpec((tm, D), row), pl.BlockSpec((1, D), fix), pl.BlockSpec((D, n), fix),
                  pl.BlockSpec((1, ql), fix), pl.BlockSpec((1, kl), fix),
                  pl.BlockSpec((tm, LANE), tab), pl.BlockSpec((tm, LANE), tab)],
        out_specs=[pl.BlockSpec((tm, D), row), pl.BlockSpec((tm, ql), row), pl.BlockSpec((tm, kl), row),
                   pl.BlockSpec((tm, rope), row), pl.BlockSpec((tm, LANE), row)],
        out_shape=[jax.ShapeDtypeStruct((T, D), BF16), jax.ShapeDtypeStruct((T, ql), BF16),
                   jax.ShapeDtypeStruct((T, kl), F32), jax.ShapeDtypeStruct((T, rope), F32),
                   jax.ShapeDtypeStruct((T, LANE), BF16)],
        compiler_params=_cparams("parallel"),
        name="small_proj",
    )(x2, g, w_small, qg, kvg, cos_t, sin_t)


def _big_proj_kernel(u_ref, *refs, nseg, nside):
    ws, sides = refs[:nseg], refs[nseg:nseg + nside]
    outs, side_outs = refs[nseg + nside:2 * nseg + nside], refs[2 * nseg + nside:]
    u = u_ref[...]
    for w_ref, o_ref in zip(ws, outs):
        o_ref[...] = _dot(u, w_ref[...]).astype(o_ref.dtype)
    for s_ref, o_ref in zip(sides, side_outs):
        o_ref[...] = s_ref[...].astype(o_ref.dtype)


def _big_proj(u, ws, out_dtypes, side, *, tm, tn):
    T, D = u.shape
    n = ws[0].shape[1]
    nseg = len(ws)
    nj = n // tn
    steps = (T // tm) * nj
    side2 = [a.reshape(-1, a.shape[-1]) for a in side]
    side_blocks = [(a.shape[0] // steps, a.shape[1]) for a in side2]
    assert all(a.shape[0] % steps == 0 and b[0] % 16 == 0 for a, b in zip(side2, side_blocks))
    slab = lambda i, j: (i * nj + j, 0)
    res = pl.pallas_call(
        functools.partial(_big_proj_kernel, nseg=nseg, nside=len(side)),
        grid=(T // tm, nj),
        in_specs=[pl.BlockSpec((tm, D), lambda i, j: (i, 0))]
                 + [pl.BlockSpec((D, tn), lambda i, j: (0, j)) for _ in ws]
                 + [pl.BlockSpec(b, slab) for b in side_blocks],
        out_specs=[pl.BlockSpec((tm, tn), lambda i, j: (i, j)) for _ in ws]
                  + [pl.BlockSpec(b, slab) for b in side_blocks],
        out_shape=[jax.ShapeDtypeStruct((T, n), dt) for dt in out_dtypes]
                  + [jax.ShapeDtypeStruct(a.shape, BF16) for a in side2],
        compiler_params=_cparams("arbitrary", "arbitrary"),
        name="big_proj",
    )(u, *ws, *side2)
    return res[:nseg], [o.reshape(a.shape) for o, a in zip(res[nseg:], side)]


def _q_proj_kernel(c_ref, wr_ref, ws_ref, cos_ref, sin_ref, q_ref, *, heads):
    c = c_ref[...]
    raw = _dot(c, wr_ref[...])
    sw = _dot(c, ws_ref[...])
    cos = cos_ref[...]
    sin = sin_ref[...]
    for h in range(heads):
        b = 2 * LANE * h
        q_ref[:, b:b + LANE] = raw[:, b:b + LANE].astype(BF16)
        r = raw[:, b + LANE:b + 2 * LANE] * cos + sw[:, LANE * h:LANE * (h + 1)] * sin
        q_ref[:, b + LANE:b + 2 * LANE] = r.astype(BF16)


def _q_proj(cqn, wq_raw, wq_sw, cos_t, sin_t, *, tm, heads):
    T, ql = cqn.shape
    nblk = cos_t.shape[0] // tm
    row = lambda i: (i, 0)
    fix = lambda i: (0, 0)
    tab = lambda i: (i % nblk, 0)
    n = wq_raw.shape[1]
    return pl.pallas_call(
        functools.partial(_q_proj_kernel, heads=heads),
        grid=(T // tm,),
        in_specs=[pl.BlockSpec((tm, ql), row), pl.BlockSpec(wq_raw.shape, fix), pl.BlockSpec(wq_sw.shape, fix),
                  pl.BlockSpec((tm, LANE), tab), pl.BlockSpec((tm, LANE), tab)],
        out_specs=pl.BlockSpec((tm, n), row),
        out_shape=jax.ShapeDtypeStruct((T, n), BF16),
        compiler_params=_cparams("parallel"),
        name="q_proj",
    )(cqn, wq_raw, wq_sw, cos_t, sin_t)


def _kv_proj_kernel(lat_ref, krb_ref, wk_ref, wv_ref, k_ref, v_ref, *, heads):
    lat = lat_ref[...].astype(BF16)
    kn = _dot(lat, wk_ref[...]).astype(BF16)
    v_ref[...] = _dot(lat, wv_ref[...]).astype(BF16)
    krb = krb_ref[...]
    for h in range(heads):
        b = 2 * LANE * h
        k_ref[:, b:b + LANE] = kn[:, LANE * h:LANE * (h + 1)]
        k_ref[:, b + LANE:b + 2 * LANE] = krb


def _kv_proj(lat, krb, wk, wv, *, tm, heads):
    T, kl = lat.shape
    row = lambda i: (i, 0)
    fix = lambda i: (0, 0)
    return pl.pallas_call(
        functools.partial(_kv_proj_kernel, heads=heads),
        grid=(T // tm,),
        in_specs=[pl.BlockSpec((tm, kl), row), pl.BlockSpec((tm, LANE), row),
                  pl.BlockSpec(wk.shape, fix), pl.BlockSpec(wv.shape, fix)],
        out_specs=[pl.BlockSpec((tm, 2 * LANE * heads), row), pl.BlockSpec((tm, wv.shape[1]), row)],
        out_shape=[jax.ShapeDtypeStruct((T, 2 * LANE * heads), BF16), jax.ShapeDtypeStruct((T, wv.shape[1]), BF16)],
        compiler_params=_cparams("parallel"),
        name="kv_proj",
    )(lat, krb, wk, wv)


def _attn_kernel(q_ref, k_ref, v_ref, o_ref, *, tq, tk, hpb):
    qi = pl.program_id(2)
    kd = 2 * LANE
    vd = v_ref.shape[2] // hpb
    qs = [q_ref[0, :, g * kd:(g + 1) * kd] for g in range(hpb)]

    def update(carry, s, v):
        m, l, acc = carry
        m_new = jnp.maximum(m, jnp.max(s, axis=-1, keepdims=True))
        alpha = jnp.exp(m - m_new)
        p = jnp.exp(s - m_new)
        l = alpha * l + jnp.sum(p, axis=-1, keepdims=True)
        acc = alpha * acc + _dot(p.astype(BF16), v)
        return m_new, l, acc

    def block(carry, start, mask, r0):
        new = []
        for g in range(hpb):
            k = k_ref[0, pl.ds(start, tk), g * kd:(g + 1) * kd]
            v = v_ref[0, pl.ds(start, tk), g * vd:(g + 1) * vd]
            s = _dot_nt(qs[g][r0:], k)
            if mask is not None:
                s = jnp.where(mask, s, NEG_INF)
            upd = update(tuple(a[r0:] for a in carry[g]), s, v)
            new.append(tuple(jnp.concatenate([a[:r0], b], axis=0) if r0 else b for a, b in zip(carry[g], upd)))
        return tuple(new)

    one = (jnp.full((tq, 1), NEG_INF, F32), jnp.zeros((tq, 1), F32), jnp.zeros((tq, vd), F32))
    nfull = qi * (tq // tk)
    carry = lax.fori_loop(0, nfull, lambda j, c: block(c, pl.multiple_of(j * tk, tk), None, 0), (one,) * hpb)
    qc = lax.broadcasted_iota(jnp.int32, (tq, tk), 0) // CHUNK
    kc = lax.broadcasted_iota(jnp.int32, (tq, tk), 1) // CHUNK
    for d in range(tq // tk):
        mask = (kc + d * (tk // CHUNK) <= qc)[d * tk:]
        carry = block(carry, pl.multiple_of((nfull + d) * tk, tk), mask, d * tk)
    for g in range(hpb):
        m, l, acc = carry[g]
        o_ref[0, :, g * vd:(g + 1) * vd] = (acc / l).astype(BF16)


def _attention(qf, kf, v, *, heads, tq, tk, hpb):
    B, S, _ = qf.shape
    vd = v.shape[2] // heads
    return pl.pallas_call(
        functools.partial(_attn_kernel, tq=tq, tk=tk, hpb=hpb),
        grid=(B, heads // hpb, S // tq),
        in_specs=[pl.BlockSpec((1, tq, 2 * LANE * hpb), lambda b, h, i: (b, i, h)),
                  pl.BlockSpec((1, S, 2 * LANE * hpb), lambda b, h, i: (b, 0, h)),
                  pl.BlockSpec((1, S, vd * hpb), lambda b, h, i: (b, 0, h))],
        out_specs=pl.BlockSpec((1, tq, vd * hpb), lambda b, h, i: (b, i, h)),
        out_shape=jax.ShapeDtypeStruct((B, S, heads * vd), BF16),
        compiler_params=_cparams("parallel", "parallel", "arbitrary"),
        name="mla_prompt_attn",
    )(qf, kf, v)


def _attn_sample_kernel(q_ref, cl_ref, ck_ref, ln_ref, kn_ref, wuk_ref, wuv_ref, o_ref,
                        ql_ref, qr_ref, ol_ref, *, heads, rope):
    sd = q_ref.shape[1]
    for h in range(heads):
        b = 2 * LANE * h
        ql_ref[h * sd:(h + 1) * sd, :] = _dot(q_ref[0, :, b:b + LANE], wuk_ref[h]).astype(BF16)
        qr_ref[h * sd:(h + 1) * sd, :] = q_ref[0, :, b + LANE:b + 2 * LANE]
    ql = ql_ref[...]
    qr = qr_ref[...][:, :rope]
    cl = cl_ref[0].astype(BF16)
    ck = ck_ref[0].astype(BF16)
    ln = ln_ref[0].astype(BF16)
    kn = kn_ref[0].astype(BF16)
    s_past = _dot_nt(ql, cl) + _dot_nt(qr, ck)
    s_new = _dot_nt(ql, ln) + _dot_nt(qr, kn)
    m = jnp.maximum(jnp.max(s_past, axis=-1, keepdims=True), jnp.max(s_new, axis=-1, keepdims=True))
    p_past = jnp.exp(s_past - m)
    p_new = jnp.exp(s_new - m)
    l = jnp.sum(p_past, axis=-1, keepdims=True) + jnp.sum(p_new, axis=-1, keepdims=True)
    o_lat = (_dot(p_past.astype(BF16), cl) + _dot(p_new.astype(BF16), ln)) / l
    ol_ref[...] = o_lat.astype(BF16)
    vd = wuv_ref.shape[2]
    for h in range(heads):
        o_ref[0, :, h * vd:(h + 1) * vd] = _dot(ol_ref[h * sd:(h + 1) * sd, :], wuv_ref[h]).astype(BF16)


def _attention_sample(qf, cache_lat, cache_kr, lat_new, kr_new, wuk_t, wuv, *, heads):
    B, sd, _ = qf.shape
    P, kl = cache_lat.shape[1:]
    rope = cache_kr.shape[2]
    vd = wuv.shape[2]
    b3 = lambda b: (b, 0, 0)
    fix = lambda b: (0, 0, 0)
    return pl.pallas_call(
        functools.partial(_attn_sample_kernel, heads=heads, rope=rope),
        grid=(B,),
        in_specs=[pl.BlockSpec((1, sd, qf.shape[2]), b3), pl.BlockSpec((1, P, kl), b3), pl.BlockSpec((1, P, rope), b3),
                  pl.BlockSpec((1, sd, kl), b3), pl.BlockSpec((1, sd, rope), b3),
                  pl.BlockSpec(wuk_t.shape, fix), pl.BlockSpec(wuv.shape, fix)],
        out_specs=pl.BlockSpec((1, sd, heads * vd), b3),
        out_shape=jax.ShapeDtypeStruct((B, sd, heads * vd), BF16),
        scratch_shapes=[pltpu.VMEM((heads * sd, kl), BF16), pltpu.VMEM((heads * sd, LANE), BF16),
                        pltpu.VMEM((heads * sd, kl), BF16)],
        compiler_params=_cparams("parallel"),
        name="mla_sample_attn",
    )(qf, cache_lat, cache_kr, lat_new, kr_new, wuk_t, wuv)


def _hgrn_kernel(*refs, heads, has_state):
    if has_state:
        hq_ref, hf_ref, hi_ref, hg_ref, lb_ref, g_ref, s0_ref, o_ref, sn_ref, st_ref = refs
    else:
        hq_ref, hf_ref, hi_ref, hg_ref, lb_ref, g_ref, o_ref, sn_ref, st_ref = refs
    c = pl.program_id(1)
    L = hq_ref.shape[1]
    dk = st_ref.shape[2]

    @pl.when(c == 0)
    def _():
        if has_state:
            for h in range(heads):
                st_ref[h] = s0_ref[0, h].T
        else:
            st_ref[...] = jnp.zeros_like(st_ref)

    row = lax.broadcasted_iota(jnp.int32, (L, L), 0)
    col = lax.broadcasted_iota(jnp.int32, (L, L), 1)
    causal = col <= row
    tri = jnp.where(causal, 1.0, 0.0).astype(BF16)
    lb = lb_ref[...]
    f = lb + (1.0 - lb) * _sigmoid(hf_ref[0])
    kk = 1.0 - f
    logf = jnp.log(f)
    hi = logf.astype(BF16)
    r1 = logf - hi.astype(F32)
    mid = r1.astype(BF16)
    lo = (r1 - mid.astype(F32)).astype(BF16)
    cum = _dot(tri, hi) + _dot(tri, mid) + _dot(tri, lo)
    tot = cum[L - 1:L, :]
    q_dec = (hq_ref[0].astype(F32) * jnp.exp(cum)).astype(BF16)
    k_dec = (kk * jnp.exp(-cum)).astype(BF16)
    k2 = (kk * jnp.exp(tot - cum)).astype(BF16)
    decay = jnp.exp(tot)
    vb = hi_ref[0]
    v32 = vb.astype(F32)
    sls = [slice(h * dk, (h + 1) * dk) for h in range(heads)]
    atts = [jnp.where(causal, _dot_nt(q_dec[:, sl], k_dec[:, sl]), 0.0).astype(BF16) for sl in sls]
    sts = [st_ref[h] for h in range(heads)]
    outs = [_dot_nt(q_dec[:, sl], sts[h].astype(BF16)) + _dot(atts[h], vb[:, sl]) for h, sl in enumerate(sls)]
    for h, sl in enumerate(sls):
        st_ref[h] = decay[:, sl] * sts[h] + _dot(v32[:, sl].T.astype(BF16), k2[:, sl])
    g = g_ref[...]
    for h, sl in enumerate(sls):
        o = outs[h]
        on = o * lax.rsqrt(jnp.mean(o * o, axis=-1, keepdims=True) + NORM_EPS) * g
        hg = hg_ref[0, :, sl].astype(F32)
        o_ref[0, :, sl] = (on * (hg * _sigmoid(hg))).astype(BF16)

    @pl.when(c == pl.num_programs(1) - 1)
    def _():
        for h in range(heads):
            sn_ref[0, h] = st_ref[h].T


def _hgrn(hq, hf, hi, hg, lb, g, s0, *, heads, L):
    B, S, F = hq.shape
    dk = F // heads
    blk = lambda b, c: (b, c, 0)
    fix2 = lambda b, c: (0, 0)
    st4 = lambda b, c: (b, 0, 0, 0)
    in_specs = [pl.BlockSpec((1, L, F), blk)] * 4 + [pl.BlockSpec((1, F), fix2), pl.BlockSpec((1, dk), fix2)]
    args = [hq, hf, hi, hg, lb, g]
    if s0 is not None:
        in_specs.append(pl.BlockSpec((1, heads, dk, dk), st4))
        args.append(s0)
    return pl.pallas_call(
        functools.partial(_hgrn_kernel, heads=heads, has_state=s0 is not None),
        grid=(B, S // L),
        in_specs=in_specs,
        out_specs=[pl.BlockSpec((1, L, F), blk), pl.BlockSpec((1, heads, dk, dk), st4)],
        out_shape=[jax.ShapeDtypeStruct((B, S, F), BF16), jax.ShapeDtypeStruct((B, heads, dk, dk), F32)],
        scratch_shapes=[pltpu.VMEM((heads, dk, dk), F32)],
        compiler_params=_cparams("parallel", "arbitrary"),
        name="hgrn",
    )(*args)


def _out_proj_kernel(oa_ref, ob_ref, ga_ref, gb_ref, x_ref, w_ref, o_ref, *, kc):
    acc = x_ref[...]
    for k in range(w_ref.shape[0] // kc):
        sl = slice(k * kc, (k + 1) * kc)
        ga = _sigmoid(ga_ref[:, sl].astype(F32))
        gb = _sigmoid(gb_ref[:, sl].astype(F32))
        m = (ga * oa_ref[:, sl].astype(F32) + gb * ob_ref[:, sl].astype(F32)).astype(BF16)
        acc = acc + _dot(m, w_ref[sl, :])
    o_ref[...] = acc


def _out_proj(oa, ob, ga, gb, x2, w_o, *, tm):
    T, D = x2.shape
    row = lambda i: (i, 0)
    return pl.pallas_call(
        functools.partial(_out_proj_kernel, kc=512),
        grid=(T // tm,),
        in_specs=[pl.BlockSpec((tm, D), row)] * 5 + [pl.BlockSpec((D, D), lambda i: (0, 0))],
        out_specs=pl.BlockSpec((tm, D), row),
        out_shape=jax.ShapeDtypeStruct((T, D), F32),
        compiler_params=_cparams("parallel"),
        name="out_proj",
    )(oa, ob, ga, gb, x2, w_o)


def _split3(x):
    hi = x.astype(BF16)
    r = x - hi.astype(F32)
    mid = r.astype(BF16)
    lo = (r - mid.astype(F32)).astype(BF16)
    return hi, mid, lo


def _router_kernel(hp_ref, hs_ref, g_ref, w3_ref, b_ref, u_ref, idx_ref, gate_ref, cnt_ref, run_ref, *, n_exp, n_p):
    step = pl.program_id(0)

    @pl.when(step == 0)
    def _():
        run_ref[...] = jnp.zeros_like(run_ref)

    h = jnp.where(step < n_p, hp_ref[...], hs_ref[...])
    u = _rms(h, g_ref[...])
    u_ref[...] = _pack_bf16_pairs(u)
    u_hi, u_mid, u_lo = _split3(u)
    e1, e2 = n_exp, 2 * n_exp
    p_hi = _dot(u_hi, w3_ref[...])
    p_mid = _dot(u_mid, w3_ref[:, :e2])
    p_lo = _dot(u_lo, w3_ref[:, :e1])
    logits = (p_hi[:, :e1] + (p_hi[:, e1:e2] + p_mid[:, :e1]) + (p_hi[:, e2:] + p_mid[:, e1:] + p_lo)) + b_ref[...]
    tm = logits.shape[0]
    lane_e = lax.broadcasted_iota(jnp.int32, (tm, n_exp), 1)
    lane_o = lax.broadcasted_iota(jnp.int32, (tm, LANE), 1)
    idx_out = jnp.zeros((tm, LANE), jnp.int32)
    val_out = jnp.zeros((tm, LANE), F32)
    vals, picks = [], []
    for k in range(TOP_K):
        m = jnp.max(logits, axis=-1, keepdims=True)
        i = jnp.min(jnp.where(logits == m, lane_e, n_exp), axis=-1, keepdims=True)
        pick = lane_e == i
        logits = jnp.where(pick, -jnp.inf, logits)
        vals.append(m)
        picks.append(pick)
        idx_out = jnp.where(lane_o == k, i, idx_out)
    es = [jnp.exp(v - vals[0]) for v in vals]
    den = es[0]
    for e in es[1:]:
        den = den + e
    for k in range(TOP_K):
        val_out = jnp.where(lane_o == k, es[k] / den, val_out)
    multi = jnp.zeros((tm, n_exp), F32)
    for pick in picks:
        multi = multi + jnp.where(pick, 1.0, 0.0)
    earlier = lax.broadcasted_iota(jnp.int32, (tm, tm), 1) < lax.broadcasted_iota(jnp.int32, (tm, tm), 0)
    before = _dot(jnp.where(earlier, 1.0, 0.0).astype(BF16), multi.astype(BF16)) + run_ref[...]
    for k in range(TOP_K):
        rank = jnp.sum(jnp.where(picks[k], before, 0.0), axis=-1, keepdims=True).astype(jnp.int32)
        idx_out = jnp.where(lane_o == TOP_K + k, rank, idx_out)
    run_ref[...] += jnp.sum(multi, axis=0, keepdims=True)
    cnt_ref[...] = run_ref[...]
    idx_ref[...] = idx_out
    gate_ref[...] = val_out


def _two_group_specs(shape, n_p):
    return [pl.BlockSpec(shape, lambda i: (jnp.minimum(i, n_p - 1), 0)),
            pl.BlockSpec(shape, lambda i: (jnp.maximum(i - n_p, 0), 0))]


def _router(h2p, h2s, g, w_r, b_r, *, tm):
    (Tp, D), Ts = h2p.shape, h2s.shape[0]
    T = Tp + Ts
    n_exp = w_r.shape[1]
    w3 = jnp.concatenate(_split3(w_r), axis=1)
    row = lambda i: (i, 0)
    fix = lambda i: (0, 0)
    return pl.pallas_call(
        functools.partial(_router_kernel, n_exp=n_exp, n_p=Tp // tm),
        grid=(T // tm,),
        in_specs=_two_group_specs((tm, D), Tp // tm)
                 + [pl.BlockSpec((1, D), fix), pl.BlockSpec((D, 3 * n_exp), fix), pl.BlockSpec((1, n_exp), fix)],
        out_specs=[pl.BlockSpec((tm, D // 2), row), pl.BlockSpec((tm, LANE), row), pl.BlockSpec((tm, LANE), row),
                   pl.BlockSpec((1, n_exp), fix)],
        out_shape=[jax.ShapeDtypeStruct((T, D // 2), jnp.uint32), jax.ShapeDtypeStruct((T, LANE), jnp.int32),
                   jax.ShapeDtypeStruct((T, LANE), F32), jax.ShapeDtypeStruct((1, n_exp), F32)],
        scratch_shapes=[pltpu.VMEM((1, n_exp), F32)],
        compiler_params=_cparams("arbitrary"),
        name="router",
    )(h2p, h2s, g, w3, b_r)


def _row_copy(src_ref, dst_ref, src_row, dst_row, sem):
    return pltpu.make_async_copy(src_ref.at[pl.ds(src_row, 1)], dst_ref.at[pl.ds(dst_row, 1)], sem)


def _dispatch_kernel(zs_ref, ze_ref, dest_hbm, u_ref, xs_hbm, idx_smem, zrow_ref, sem_idx, sem, sem_z, *, R, n_exp):
    i = pl.program_id(0)
    n = pl.num_programs(0)
    slot = i % 2
    N = R * TOP_K

    def idx_copy(step, s):
        return pltpu.make_async_copy(dest_hbm.at[step], idx_smem.at[pl.ds(pl.multiple_of(s * N, N), N)],
                                     sem_idx.at[s])

    def start_rows(s):
        def body(r, c):
            for k in range(TOP_K):
                _row_copy(u_ref, xs_hbm, r, idx_smem[s * N + r * TOP_K + k], sem).start()
            return c
        lax.fori_loop(0, R, body, 0, unroll=8)

    def wait_rows():
        def body(r, c):
            for k in range(TOP_K):
                _row_copy(u_ref, xs_hbm, 0, 0, sem).wait()
            return c
        lax.fori_loop(0, R, body, 0, unroll=8)

    def zero_rows(start):
        def per_expert(e, c):
            def body(r, c2):
                cp = _row_copy(zrow_ref, xs_hbm, 0, r, sem_z)
                cp.start() if start else cp.wait()
                return c2
            return lax.fori_loop(zs_ref[e], ze_ref[e], body, c)
        lax.fori_loop(0, n_exp, per_expert, 0)

    @pl.when(i == 0)
    def _():
        zrow_ref[...] = jnp.zeros_like(zrow_ref)
        idx_copy(0, 0).start()

    idx_copy(i, slot).wait()

    @pl.when(i + 1 < n)
    def _():
        idx_copy(i + 1, 1 - slot).start()

    start_rows(slot)

    @pl.when(i == n - 1)
    def _():
        zero_rows(True)

    wait_rows()

    @pl.when(i == n - 1)
    def _():
        zero_rows(False)


def _dispatch(zs, ze, dest2, u, *, P, R):
    n = dest2.shape[0]
    D = u.shape[1]
    grid_spec = pltpu.PrefetchScalarGridSpec(
        num_scalar_prefetch=2,
        grid=(n,),
        in_specs=[pl.BlockSpec(memory_space=pl.ANY), pl.BlockSpec((R, D), lambda i, zs, ze: (i, 0))],
        out_specs=pl.BlockSpec(memory_space=pl.ANY),
        scratch_shapes=[pltpu.SMEM((2 * R * TOP_K,), jnp.int32), pltpu.VMEM((8, D), u.dtype),
                        pltpu.SemaphoreType.DMA((2,)), pltpu.SemaphoreType.DMA, pltpu.SemaphoreType.DMA],
    )
    return pl.pallas_call(
        functools.partial(_dispatch_kernel, R=R, n_exp=zs.shape[0]),
        grid_spec=grid_spec,
        out_shape=jax.ShapeDtypeStruct((P, D), u.dtype),
        compiler_params=_cparams("arbitrary"),
        name="moe_dispatch",
    )(zs, ze, dest2, u)


def _expert_kernel(te_ref, nv_ref, x_ref, wg_ref, wu_ref, bg_ref, bu_ref, wd_ref, bd_ref, y_ref, xb_ref):
    i = pl.program_id(0)
    c = pl.program_id(1)
    half = x_ref.shape[1]

    live = nv_ref[i] > 0

    @pl.when(c == 0)
    def _():
        hi, lo = _unpack_bf16_pairs(x_ref[...])
        xb_ref[:, :half] = hi
        xb_ref[:, half:] = lo

    def chunk(first):
        xb = xb_ref[...]
        gate = jnp.minimum(_dot(xb, wg_ref[0]) + bg_ref[0], SWIGLU_LIMIT)
        up = jnp.clip(_dot(xb, wu_ref[0]) + bu_ref[0], -SWIGLU_LIMIT, SWIGLU_LIMIT)
        act = gate * _sigmoid(SWIGLU_ALPHA * gate) * (up + 1.0)
        down = _dot(act.astype(BF16), wd_ref[0])
        if first:
            y_ref[...] = down + bd_ref[0]
        else:
            y_ref[...] += down

    @pl.when(jnp.logical_and(live, c == 0))
    def _():
        chunk(True)

    @pl.when(jnp.logical_and(live, c > 0))
    def _():
        chunk(False)

    @pl.when(jnp.logical_and(jnp.logical_not(live), c == 0))
    def _():
        y_ref[...] = jnp.broadcast_to(bd_ref[0], y_ref.shape)


def _experts(tile_e, tile_nv, xs, w_gu, b_gu, w_dn, b_dn, *, tm, fc):
    P, half = xs.shape
    D = 2 * half
    E, _, ff2 = w_gu.shape
    ff = ff2 // 2
    nc = ff // fc
    n_tiles = P // tm

    def cc(i, c, nv):
        return jnp.where(nv[i] > 0, c, nc - 1)

    grid_spec = pltpu.PrefetchScalarGridSpec(
        num_scalar_prefetch=2,
        grid=(n_tiles, nc),
        in_specs=[
            pl.BlockSpec((tm, half), lambda i, c, te, nv: (jnp.where(nv[i] > 0, i, 0), 0)),
            pl.BlockSpec((1, D, fc), lambda i, c, te, nv: (te[i], 0, cc(i, c, nv))),
            pl.BlockSpec((1, D, fc), lambda i, c, te, nv: (te[i], 0, nc + cc(i, c, nv))),
            pl.BlockSpec((1, 1, fc), lambda i, c, te, nv: (te[i], 0, cc(i, c, nv))),
            pl.BlockSpec((1, 1, fc), lambda i, c, te, nv: (te[i], 0, nc + cc(i, c, nv))),
            pl.BlockSpec((1, fc, D), lambda i, c, te, nv: (te[i], cc(i, c, nv), 0)),
            pl.BlockSpec((1, 1, D), lambda i, c, te, nv: (te[i], 0, 0)),
        ],
        out_specs=pl.BlockSpec((tm, D), lambda i, c, te, nv: (i, 0)),
        scratch_shapes=[pltpu.VMEM((tm, D), BF16)],
    )
    return pl.pallas_call(
        _expert_kernel,
        grid_spec=grid_spec,
        out_shape=jax.ShapeDtypeStruct((P, D), F32),
        compiler_params=_cparams("arbitrary", "arbitrary"),
        name="moe_experts",
    )(tile_e, tile_nv, xs, w_gu, w_gu, b_gu, b_gu, w_dn, b_dn)


def _combine_kernel(dest_hbm, gate_ref, hp_ref, hs_ref, ys_hbm, g_ref, yp_ref, ysm_ref, buf_ref, idx_smem, sem_idx,
                    sem, *, n_p):
    i = pl.program_id(0)
    n = pl.num_programs(0)
    R = hp_ref.shape[0]
    slot = i % 2
    N = R * TOP_K

    def idx_copy(step, s):
        return pltpu.make_async_copy(dest_hbm.at[step], idx_smem.at[pl.ds(pl.multiple_of(s * N, N), N)],
                                     sem_idx.at[s])

    def start_rows(s):
        def body(r, c):
            for k in range(TOP_K):
                _row_copy(ys_hbm, buf_ref.at[s, k], idx_smem[s * N + r * TOP_K + k], r, sem.at[s]).start()
            return c
        lax.fori_loop(0, R, body, 0, unroll=8)

    def wait_rows(s):
        def body(r, c):
            for k in range(TOP_K):
                _row_copy(ys_hbm, buf_ref.at[s, k], 0, r, sem.at[s]).wait()
            return c
        lax.fori_loop(0, R, body, 0, unroll=8)

    @pl.when(i == 0)
    def _():
        cp = idx_copy(0, 0)
        cp.start()
        cp.wait()
        start_rows(0)

        @pl.when(n > 1)
        def _():
            idx_copy(1, 1).start()

    @pl.when(i + 1 < n)
    def _():
        idx_copy(i + 1, 1 - slot).wait()
        start_rows(1 - slot)

    @pl.when(i + 2 < n)
    def _():
        idx_copy(i + 2, slot).start()

    wait_rows(slot)
    is_p = i < n_p
    h = jnp.where(is_p, hp_ref[...], hs_ref[...])
    gate = gate_ref[...]
    for k in range(TOP_K):
        h = h + gate[:, k:k + 1] * buf_ref[slot, k]
    y = _rms(h, g_ref[...])

    @pl.when(is_p)
    def _():
        yp_ref[...] = y

    @pl.when(jnp.logical_not(is_p))
    def _():
        ysm_ref[...] = y


def _combine(dest2, gate, h2p, h2s, ys, g, *, R):
    (Tp, D), Ts = h2p.shape, h2s.shape[0]
    n_p = Tp // R
    row = lambda i: (i, 0)
    return pl.pallas_call(
        functools.partial(_combine_kernel, n_p=n_p),
        grid=((Tp + Ts) // R,),
        in_specs=[pl.BlockSpec(memory_space=pl.ANY), pl.BlockSpec((R, LANE), row)] + _two_group_specs((R, D), n_p)
                 + [pl.BlockSpec(memory_space=pl.ANY), pl.BlockSpec((1, D), lambda i: (0, 0))],
        out_specs=_two_group_specs((R, D), n_p),
        out_shape=[jax.ShapeDtypeStruct((Tp, D), F32), jax.ShapeDtypeStruct((Ts, D), F32)],
        scratch_shapes=[pltpu.VMEM((2, TOP_K, R, D), F32), pltpu.SMEM((2 * R * TOP_K,), jnp.int32),
                        pltpu.SemaphoreType.DMA((2,)), pltpu.SemaphoreType.DMA((2,))],
        compiler_params=_cparams("arbitrary"),
        name="moe_combine",
    )(dest2, gate, h2p, h2s, ys, g)


def _rope_tables(pos, rope, rows):
    half = rope // 2
    inv = ROPE_BASE ** (-jnp.arange(half, dtype=F32) / half)
    ang = pos.astype(F32)[:, None] * inv[None, :]
    cos, sin = jnp.cos(ang), jnp.sin(ang)
    z = jnp.zeros((pos.shape[0], LANE - rope), F32)
    cos_t = jnp.concatenate([cos, cos, z], axis=1)
    sin_t = jnp.concatenate([-sin, sin, z], axis=1)
    rep = max(1, rows // pos.shape[0])
    return jnp.tile(cos_t, (rep, 1)), jnp.tile(sin_t, (rep, 1))


def _swap_halves(w):
    half = w.shape[-1] // 2
    return jnp.concatenate([w[..., half:], w[..., :half]], axis=-1)


def _mixer(x, pos, past, wts, side, *, tm):
    B, S, D = x.shape
    T = B * S
    heads, ql, kl, rope = wts["heads"], wts["ql"], wts["kl"], wts["rope"]
    x2 = x.reshape(T, D)
    cos_t, sin_t = _rope_tables(pos, rope, tm)
    u, cqn, lat, kr, krb = _small_proj(x2, wts["norm_mix_g"], wts["w_small"], wts["q_norm_g"], wts["kv_norm_g"],
                                       cos_t, sin_t, tm=tm, ql=ql, kl=kl, rope=rope)
    tm_big = min(1024, T)
    (hq, hf, hi, hg, ga, gb), side_out = _big_proj(u, wts["w_big"], (BF16, F32, BF16, BF16, BF16, BF16), side,
                                                   tm=tm_big, tn=256)
    qf = _q_proj(cqn, wts["wq_raw"], wts["wq_sw"], cos_t, sin_t, tm=tm, heads=heads)
    F = hq.shape[1]
    r3 = lambda a: a.reshape(B, S, a.shape[1])
    if past is None:
        kf, v = _kv_proj(lat, krb, wts["w_uk"], wts["w_uv"], tm=tm, heads=heads)
        o_a = _attention(r3(qf), r3(kf), r3(v), heads=heads, tq=min(2048, S), tk=min(1024, S), hpb=2)
        o_b, st = _hgrn(r3(hq), r3(hf), r3(hi), r3(hg), wts["lb"], wts["hgrn_g"], None, heads=wts["hg_heads"],
                        L=CHUNK)
    else:
        cache_lat, cache_kr, state = past
        o_a = _attention_sample(r3(qf), cache_lat, cache_kr, r3(lat), r3(kr), wts["w_uk_t"], wts["w_uv_h"],
                                heads=heads)
        o_b, st = _hgrn(r3(hq), r3(hf), r3(hi), r3(hg), wts["lb"], wts["hgrn_g"], state, heads=wts["hg_heads"], L=S)
    h2 = _out_proj(o_a.reshape(T, D), o_b.reshape(T, F), ga, gb, x2, wts["w_o"], tm=256)
    return h2, lat.reshape(B, S, kl), kr.reshape(B, S, rope), st, side_out


def kernel(x_prompt, x_sample, cache_mla_latent, cache_mla_krope, state_hgrn, norm_mix_g, w_in, q_norm_g, w_uq,
           kv_norm_g, w_uk, w_uv, hgrn_lb_raw, hgrn_norm_g, w_o, norm_ffn_g, w_router, b_router, w_gate_up,
           b_gate_up, w_down, b_down, norm_final_g):
    depth = w_in.shape[0]
    assert depth == 1, "single-layer kernel"
    B, S, D = x_prompt.shape
    Bd, Sd, _ = x_sample.shape
    past_len = cache_mla_latent.shape[2]
    ql, heads, qk = w_uq.shape[1:]
    kl, _, nope = w_uk.shape[1:]
    vd = w_uv.shape[3]
    rope = qk - nope
    hg_heads, dk, dv = state_hgrn.shape[2:]
    n_exp, _, ff2 = w_gate_up.shape[1:]
    assert nope == LANE and vd == LANE and dk == LANE and dv == LANE and rope <= LANE
    hf_dim = hg_heads * dk
    scale = 1.0 / math.sqrt(nope + rope)
    l = 0

    wi = w_in[l]
    o_cq, o_kv, o_pe, o_hq = 0, ql, ql + kl, ql + kl + rope
    k_pe = wi[:, o_pe:o_pe + rope]
    zpad = jnp.zeros((D, LANE - rope), F32)
    w_small = jnp.concatenate([wi[:, o_cq:o_pe], k_pe, zpad, _swap_halves(k_pe), zpad], axis=1).astype(BF16)
    segs = [hf_dim, hf_dim, D, D, D, D]
    offs = [o_hq]
    for s_ in segs[:-1]:
        offs.append(offs[-1] + s_)
    w_big = [wi[:, o:o + s_].astype(BF16) for o, s_ in zip(offs, segs)]
    wq = w_uq[l] * scale
    zq = jnp.zeros((ql, heads, LANE - rope), F32)
    wq_raw = jnp.concatenate([wq[..., :nope], wq[..., nope:], zq], axis=-1).reshape(ql, heads * 2 * LANE).astype(BF16)
    wq_sw = jnp.concatenate([_swap_halves(wq[..., nope:]), zq], axis=-1).reshape(ql, heads * LANE).astype(BF16)
    lb_all = jnp.cumsum(jax.nn.softmax(hgrn_lb_raw.astype(F32), axis=0), axis=0)
    wts = dict(
        heads=heads, ql=ql, kl=kl, rope=rope, hg_heads=hg_heads,
        norm_mix_g=norm_mix_g[l][None], w_small=w_small, q_norm_g=q_norm_g[l][None], kv_norm_g=kv_norm_g[l][None],
        w_big=w_big, wq_raw=wq_raw, wq_sw=wq_sw,
        w_uk=w_uk[l].reshape(kl, heads * nope).astype(BF16), w_uv=w_uv[l].reshape(kl, heads * vd).astype(BF16),
        w_uk_t=jnp.transpose(w_uk[l], (1, 2, 0)).astype(BF16), w_uv_h=jnp.transpose(w_uv[l], (1, 0, 2)).astype(BF16),
        lb=lb_all[l][None], hgrn_g=hgrn_norm_g[l][None], w_o=w_o[l].astype(BF16),
    )

    Tp, Ts = B * S, Bd * Sd
    tm = 512
    assert S % tm == 0 or tm % S == 0
    assert Tp % tm == 0 and Ts % tm == 0 and tm % Sd == 0
    h2p, lat_p, kr_p, st_p, (w_gu_b, w_dn_b) = _mixer(x_prompt, jnp.arange(S), None, wts,
                                                       [w_gate_up[l], w_down[l]], tm=tm)
    past = (cache_mla_latent[l], cache_mla_krope[l], state_hgrn[l])
    h2s, lat_s, kr_s, st_s, _ = _mixer(x_sample, past_len + jnp.arange(Sd), past, wts, [], tm=tm)

    T_all = Tp + Ts
    u_all, idx, gate, cnt = _router(h2p, h2s, norm_ffn_g[l][None], w_router[l], b_router[l][None], tm=tm)

    te_rows = 512
    A = T_all * TOP_K
    counts = cnt[0].astype(jnp.int32)
    padded = (counts + te_rows - 1) // te_rows * te_rows
    pad_end = jnp.cumsum(padded)
    pad_start = pad_end - padded
    used_end = pad_start + counts
    e_sel, rank = idx[:, :TOP_K], idx[:, TOP_K:2 * TOP_K]
    onehot = e_sel[:, :, None] == jnp.arange(n_exp, dtype=jnp.int32)[None, None, :]
    dest = (rank + jnp.sum(jnp.where(onehot, pad_start[None, None, :], 0), axis=-1)).reshape(A)
    n_tiles = -(-A // te_rows) + n_exp
    P = n_tiles * te_rows
    tile_start = jnp.arange(n_tiles, dtype=jnp.int32) * te_rows
    tile_e = jnp.minimum(jnp.sum((tile_start[:, None] >= pad_end[None, :]).astype(jnp.int32), axis=1), n_exp - 1)
    tile_nv = jnp.clip(used_end[tile_e] - tile_start, 0, te_rows).astype(jnp.int32)
    tile_nv = jnp.where(tile_start < pad_end[-1], tile_nv, 0)
    last_e = tile_e[jnp.maximum(jnp.sum((tile_start < pad_end[-1]).astype(jnp.int32)) - 1, 0)]
    tile_e = jnp.where(tile_nv > 0, tile_e, last_e)

    dR = 512
    xs = _dispatch(used_end.astype(jnp.int32), pad_end.astype(jnp.int32),
                   dest.reshape(T_all // dR, dR * TOP_K), u_all, P=P, R=dR)
    ys = _experts(tile_e, tile_nv, xs, w_gu_b, b_gate_up[l][:, None, :], w_dn_b, b_down[l][:, None, :],
                  tm=te_rows, fc=1024)

    cR = 256
    y_p, y_s = _combine(dest.reshape(T_all // cR, cR * TOP_K), gate, h2p, h2s, ys, norm_final_g[None], R=cR)

    return (y_p.reshape(B, S, D), y_s.reshape(Bd, Sd, D), lat_p[None], kr_p[None], st_p[None],
            lat_s[None], kr_s[None], st_s[None])
```

```python
import functools
import math

import jax
import jax.numpy as jnp
from jax import lax
from jax.experimental import pallas as pl
from jax.experimental.pallas import tpu as pltpu

F32 = jnp.float32
BF16 = jnp.bfloat16

CHUNK = 64
NORM_EPS = 1e-6
ROPE_BASE = 10000.0
NEG_INF = -1e30
TOP_K = 4
SWIGLU_ALPHA = 1.702
SWIGLU_LIMIT = 7.0

LANE = 128
VMEM_LIMIT = 56 * 1024 * 1024


def _cparams(*sem):
    return pltpu.CompilerParams(dimension_semantics=sem, vmem_limit_bytes=VMEM_LIMIT)


def _dot(a, b):
    return jnp.dot(a, b, preferred_element_type=F32)


def _dot_nt(a, b):
    return lax.dot_general(a, b, (((1,), (1,)), ((), ())), preferred_element_type=F32)


def _rms(x, g):
    return x * lax.rsqrt(jnp.mean(x * x, axis=-1, keepdims=True) + NORM_EPS) * g


def _sigmoid(x):
    return 1.0 / (1.0 + jnp.exp(-x))


def _pack_bf16_pairs(x):
    half = x.shape[1] // 2
    bits = lax.bitcast_convert_type(x.astype(BF16).astype(F32), jnp.uint32)
    return (bits[:, :half] & jnp.uint32(0xFFFF0000)) | (bits[:, half:] >> 16)


def _unpack_bf16_pairs(p):
    hi = lax.bitcast_convert_type(p & jnp.uint32(0xFFFF0000), F32).astype(BF16)
    lo = lax.bitcast_convert_type(p << 16, F32).astype(BF16)
    return hi, lo


def _small_proj_kernel(x_ref, g_ref, w_ref, qg_ref, kvg_ref, cos_ref, sin_ref,
                       u_ref, cqn_ref, lat_ref, kr_ref, krb_ref, *, ql, kl, rope):
    u = _rms(x_ref[...], g_ref[...]).astype(BF16)
    u_ref[...] = u
    y = _dot(u, w_ref[...])
    cqn_ref[...] = _rms(y[:, :ql], qg_ref[...]).astype(BF16)
    lat_ref[...] = _rms(y[:, ql:ql + kl], kvg_ref[...])
    o = ql + kl
    kr = y[:, o:o + LANE] * cos_ref[...] + y[:, o + LANE:o + 2 * LANE] * sin_ref[...]
    kr_ref[...] = kr[:, :rope]
    krb_ref[...] = kr.astype(BF16)


def _small_proj(x2, g, w_small, qg, kvg, cos_t, sin_t, *, tm, ql, kl, rope):
    T, D = x2.shape
    n = w_small.shape[1]
    nblk = cos_t.shape[0] // tm
    row = lambda i: (i, 0)
    fix = lambda i: (0, 0)
    tab = lambda i: (i % nblk, 0)
    return pl.pallas_call(
        functools.partial(_small_proj_kernel, ql=ql, kl=kl, rope=rope),
        grid=(T // tm,),
        in_specs=[pl.BlockSpec((tm, D), row), pl.BlockSpec((1, D), fix), pl.BlockSpec((D, n), fix),
                  pl.BlockSpec((1, ql), fix), pl.BlockSpec((1, kl), fix),
                  pl.BlockSpec((tm, LANE), tab), pl.BlockSpec((tm, LANE), tab)],
        out_specs=[pl.BlockSpec((tm, D), row), pl.BlockSpec((tm, ql), row), pl.BlockSpec((tm, kl), row),
                   pl.BlockSpec((tm, rope), row), pl.BlockSpec((tm, LANE), row)],
        out_shape=[jax.ShapeDtypeStruct((T, D), BF16), jax.ShapeDtypeStruct((T, ql), BF16),
                   jax.ShapeDtypeStruct((T, kl), F32), jax.ShapeDtypeStruct((T, rope), F32),
                   jax.ShapeDtypeStruct((T, LANE), BF16)],
        compiler_params=_cparams("parallel"),
        name="small_proj",
    )(x2, g, w_small, qg, kvg, cos_t, sin_t)


def _big_proj_kernel(u_ref, *refs, nseg, nside):
    ws, sides = refs[:nseg], refs[nseg:nseg + nside]
    outs, side_outs = refs[nseg + nside:2 * nseg + nside], refs[2 * nseg + nside:]
    u = u_ref[...]
    for w_ref, o_ref in zip(ws, outs):
        o_ref[...] = _dot(u, w_ref[...]).astype(o_ref.dtype)
    for s_ref, o_ref in zip(sides, side_outs):
        o_ref[...] = s_ref[...].astype(o_ref.dtype)


def _big_proj(u, ws, out_dtypes, side, *, tm, tn):
    T, D = u.shape
    n = ws[0].shape[1]
    nseg = len(ws)
    nj = n // tn
    steps = (T // tm) * nj
    side2 = [a.reshape(-1, a.shape[-1]) for a in side]
    side_blocks = [(a.shape[0] // steps, a.shape[1]) for a in side2]
    assert all(a.shape[0] % steps == 0 and b[0] % 16 == 0 for a, b in zip(side2, side_blocks))
    slab = lambda i, j: (i * nj + j, 0)
    res = pl.pallas_call(
        functools.partial(_big_proj_kernel, nseg=nseg, nside=len(side)),
        grid=(T // tm, nj),
        in_specs=[pl.BlockSpec((tm, D), lambda i, j: (i, 0))]
                 + [pl.BlockSpec((D, tn), lambda i, j: (0, j)) for _ in ws]
                 + [pl.BlockSpec(b, slab) for b in side_blocks],
        out_specs=[pl.BlockSpec((tm, tn), lambda i, j: (i, j)) for _ in ws]
                  + [pl.BlockSpec(b, slab) for b in side_blocks],
        out_shape=[jax.ShapeDtypeStruct((T, n), dt) for dt in out_dtypes]
                  + [jax.ShapeDtypeStruct(a.shape, BF16) for a in side2],
        compiler_params=_cparams("arbitrary", "arbitrary"),
        name="big_proj",
    )(u, *ws, *side2)
    return res[:nseg], [o.reshape(a.shape) for o, a in zip(res[nseg:], side)]


def _q_proj_kernel(c_ref, wr_ref, ws_ref, cos_ref, sin_ref, q_ref, *, heads):
    c = c_ref[...]
    raw = _dot(c, wr_ref[...])
    sw = _dot(c, ws_ref[...])
    cos = cos_ref[...]
    sin = sin_ref[...]
    for h in range(heads):
        b = 2 * LANE * h
        q_ref[:, b:b + LANE] = raw[:, b:b + LANE].astype(BF16)
        r = raw[:, b + LANE:b + 2 * LANE] * cos + sw[:, LANE * h:LANE * (h + 1)] * sin
        q_ref[:, b + LANE:b + 2 * LANE] = r.astype(BF16)


def _q_proj(cqn, wq_raw, wq_sw, cos_t, sin_t, *, tm, heads):
    T, ql = cqn.shape
    nblk = cos_t.shape[0] // tm
    row = lambda i: (i, 0)
    fix = lambda i: (0, 0)
    tab = lambda i: (i % nblk, 0)
    n = wq_raw.shape[1]
    return pl.pallas_call(
        functools.partial(_q_proj_kernel, heads=heads),
        grid=(T // tm,),
        in_specs=[pl.BlockSpec((tm, ql), row), pl.BlockSpec(wq_raw.shape, fix), pl.BlockSpec(wq_sw.shape, fix),
                  pl.BlockSpec((tm, LANE), tab), pl.BlockSpec((tm, LANE), tab)],
        out_specs=pl.BlockSpec((tm, n), row),
        out_shape=jax.ShapeDtypeStruct((T, n), BF16),
        compiler_params=_cparams("parallel"),
        name="q_proj",
    )(cqn, wq_raw, wq_sw, cos_t, sin_t)


def _kv_proj_kernel(lat_ref, krb_ref, wk_ref, wv_ref, k_ref, v_ref, *, heads):
    lat = lat_ref[...].astype(BF16)
    kn = _dot(lat, wk_ref[...]).astype(BF16)
    v_ref[...] = _dot(lat, wv_ref[...]).astype(BF16)
    krb = krb_ref[...]
    for h in range(heads):
        b = 2 * LANE * h
        k_ref[:, b:b + LANE] = kn[:, LANE * h:LANE * (h + 1)]
        k_ref[:, b + LANE:b + 2 * LANE] = krb


def _kv_proj(lat, krb, wk, wv, *, tm, heads):
    T, kl = lat.shape
    row = lambda i: (i, 0)
    fix = lambda i: (0, 0)
    return pl.pallas_call(
        functools.partial(_kv_proj_kernel, heads=heads),
        grid=(T // tm,),
        in_specs=[pl.BlockSpec((tm, kl), row), pl.BlockSpec((tm, LANE), row),
                  pl.BlockSpec(wk.shape, fix), pl.BlockSpec(wv.shape, fix)],
        out_specs=[pl.BlockSpec((tm, 2 * LANE * heads), row), pl.BlockSpec((tm, wv.shape[1]), row)],
        out_shape=[jax.ShapeDtypeStruct((T, 2 * LANE * heads), BF16), jax.ShapeDtypeStruct((T, wv.shape[1]), BF16)],
        compiler_params=_cparams("parallel"),
        name="kv_proj",
    )(lat, krb, wk, wv)


def _attn_body(q_ref, k_ref, v_ref, o_ref, *, tq, tk, hpb):
    S = q_ref.shape[1]
    kd = 2 * LANE
    vd = v_ref.shape[2] // hpb

    def update(carry, s, v):
        m, l, acc = carry
        m_new = jnp.maximum(m, jnp.max(s, axis=-1, keepdims=True))
        alpha = jnp.exp(m - m_new)
        p = jnp.exp(s - m_new)
        l = alpha * l + jnp.sum(p, axis=-1, keepdims=True)
        acc = alpha * acc + _dot(p.astype(BF16), v)
        return m_new, l, acc

    for r0 in range(0, S, tq):
        seen = r0 + tq
        blocks = [(k0, min(tk, seen - k0)) for k0 in range(0, seen, tk)]
        for g in range(hpb):
            q = q_ref[0, r0:r0 + tq, g * kd:(g + 1) * kd]
            carry = (jnp.full((tq, 1), NEG_INF, F32), jnp.zeros((tq, 1), F32), jnp.zeros((tq, vd), F32))
            for k0, kl in blocks:
                k = k_ref[0, k0:k0 + kl, g * kd:(g + 1) * kd]
                v = v_ref[0, k0:k0 + kl, g * vd:(g + 1) * vd]
                s = _dot_nt(q, k)
                if k0 + kl > r0:
                    qc = (lax.broadcasted_iota(jnp.int32, (tq, kl), 0) + r0) // CHUNK
                    kc = (lax.broadcasted_iota(jnp.int32, (tq, kl), 1) + k0) // CHUNK
                    s = jnp.where(kc <= qc, s, NEG_INF)
                carry = update(carry, s, v)
            m, l, acc = carry
            o_ref[0, r0:r0 + tq, g * vd:(g + 1) * vd] = (acc / l).astype(BF16)


def _attn_kernel(q_ref, k_ref, v_ref, o_ref, *, tq, tk, hpb):
    _attn_body(q_ref, k_ref, v_ref, o_ref, tq=tq, tk=tk, hpb=hpb)


def _attention(qf, kf, v, *, heads, tq, tk, hpb):
    B, S, _ = qf.shape
    vd = v.shape[2] // heads
    seq = lambda b, h: (b, 0, h)
    return pl.pallas_call(
        functools.partial(_attn_kernel, tq=tq, tk=tk, hpb=hpb),
        grid=(B, heads // hpb),
        in_specs=[pl.BlockSpec((1, S, 2 * LANE * hpb), seq), pl.BlockSpec((1, S, 2 * LANE * hpb), seq),
                  pl.BlockSpec((1, S, vd * hpb), seq)],
        out_specs=pl.BlockSpec((1, S, vd * hpb), seq),
        out_shape=jax.ShapeDtypeStruct((B, S, heads * vd), BF16),
        compiler_params=_cparams("parallel", "parallel"),
        name="mla_prompt_attn",
    )(qf, kf, v)


def _attn_sample_kernel(q_ref, cl_ref, ck_ref, ln_ref, kn_ref, wuk_ref, wuv_ref, o_ref,
                        ql_ref, qr_ref, ol_ref, *, heads, rope):
    sd = q_ref.shape[1]
    for h in range(heads):
        b = 2 * LANE * h
        ql_ref[h * sd:(h + 1) * sd, :] = _dot(q_ref[0, :, b:b + LANE], wuk_ref[h]).astype(BF16)
        qr_ref[h * sd:(h + 1) * sd, :] = q_ref[0, :, b + LANE:b + 2 * LANE]
    ql = ql_ref[...]
    qr = qr_ref[...][:, :rope]
    cl = cl_ref[0].astype(BF16)
    ck = ck_ref[0].astype(BF16)
    ln = ln_ref[0].astype(BF16)
    kn = kn_ref[0].astype(BF16)
    s_past = _dot_nt(ql, cl) + _dot_nt(qr, ck)
    s_new = _dot_nt(ql, ln) + _dot_nt(qr, kn)
    m = jnp.maximum(jnp.max(s_past, axis=-1, keepdims=True), jnp.max(s_new, axis=-1, keepdims=True))
    p_past = jnp.exp(s_past - m)
    p_new = jnp.exp(s_new - m)
    l = jnp.sum(p_past, axis=-1, keepdims=True) + jnp.sum(p_new, axis=-1, keepdims=True)
    o_lat = (_dot(p_past.astype(BF16), cl) + _dot(p_new.astype(BF16), ln)) / l
    ol_ref[...] = o_lat.astype(BF16)
    vd = wuv_ref.shape[2]
    for h in range(heads):
        o_ref[0, :, h * vd:(h + 1) * vd] = _dot(ol_ref[h * sd:(h + 1) * sd, :], wuv_ref[h]).astype(BF16)


def _attention_sample(qf, cache_lat, cache_kr, lat_new, kr_new, wuk_t, wuv, *, heads):
    B, sd, _ = qf.shape
    P, kl = cache_lat.shape[1:]
    rope = cache_kr.shape[2]
    vd = wuv.shape[2]
    b3 = lambda b: (b, 0, 0)
    fix = lambda b: (0, 0, 0)
    return pl.pallas_call(
        functools.partial(_attn_sample_kernel, heads=heads, rope=rope),
        grid=(B,),
        in_specs=[pl.BlockSpec((1, sd, qf.shape[2]), b3), pl.BlockSpec((1, P, kl), b3), pl.BlockSpec((1, P, rope), b3),
                  pl.BlockSpec((1, sd, kl), b3), pl.BlockSpec((1, sd, rope), b3),
                  pl.BlockSpec(wuk_t.shape, fix), pl.BlockSpec(wuv.shape, fix)],
        out_specs=pl.BlockSpec((1, sd, heads * vd), b3),
        out_shape=jax.ShapeDtypeStruct((B, sd, heads * vd), BF16),
        scratch_shapes=[pltpu.VMEM((heads * sd, kl), BF16), pltpu.VMEM((heads * sd, LANE), BF16),
                        pltpu.VMEM((heads * sd, kl), BF16)],
        compiler_params=_cparams("parallel"),
        name="mla_sample_attn",
    )(qf, cache_lat, cache_kr, lat_new, kr_new, wuk_t, wuv)


def _hgrn_chunk(hq_ref, hf_ref, hi_ref, hg_ref, lb_ref, g_ref, o_ref, st_ref, rows, *, heads):
    L = rows.stop - rows.start
    dk = st_ref.shape[2]
    row = lax.broadcasted_iota(jnp.int32, (L, L), 0)
    col = lax.broadcasted_iota(jnp.int32, (L, L), 1)
    causal = col <= row
    tri = jnp.where(causal, 1.0, 0.0).astype(BF16)
    lb = lb_ref[...]
    f = lb + (1.0 - lb) * _sigmoid(hf_ref[0, rows, :])
    kk = 1.0 - f
    hi, mid, lo = _split3(jnp.log(f))
    cum = _dot(tri, hi) + _dot(tri, mid) + _dot(tri, lo)
    tot = cum[L - 1:L, :]
    q_dec = (hq_ref[0, rows, :].astype(F32) * jnp.exp(cum)).astype(BF16)
    k_dec = (kk * jnp.exp(-cum)).astype(BF16)
    k2 = (kk * jnp.exp(tot - cum)).astype(BF16)
    decay = jnp.exp(tot)
    vb = hi_ref[0, rows, :]
    v32 = vb.astype(F32)
    sls = [slice(h * dk, (h + 1) * dk) for h in range(heads)]
    atts = [jnp.where(causal, _dot_nt(q_dec[:, sl], k_dec[:, sl]), 0.0).astype(BF16) for sl in sls]
    sts = [st_ref[h] for h in range(heads)]
    outs = [_dot_nt(q_dec[:, sl], sts[h].astype(BF16)) + _dot(atts[h], vb[:, sl]) for h, sl in enumerate(sls)]
    for h, sl in enumerate(sls):
        st_ref[h] = decay[:, sl] * sts[h] + _dot(v32[:, sl].T.astype(BF16), k2[:, sl])
    g = g_ref[...]
    for h, sl in enumerate(sls):
        o = outs[h]
        on = o * lax.rsqrt(jnp.mean(o * o, axis=-1, keepdims=True) + NORM_EPS) * g
        hg = hg_ref[0, rows, sl].astype(F32)
        o_ref[0, rows, sl] = (on * (hg * _sigmoid(hg))).astype(BF16)


def _hgrn_kernel(*refs, heads, has_state, chunk):
    if has_state:
        hq_ref, hf_ref, hi_ref, hg_ref, lb_ref, g_ref, s0_ref, o_ref, sn_ref, st_ref = refs
    else:
        hq_ref, hf_ref, hi_ref, hg_ref, lb_ref, g_ref, o_ref, sn_ref, st_ref = refs
    c = pl.program_id(1)

    @pl.when(c == 0)
    def _():
        if has_state:
            for h in range(heads):
                st_ref[h] = s0_ref[0, h].T
        else:
            st_ref[...] = jnp.zeros_like(st_ref)

    for c0 in range(0, hq_ref.shape[1], chunk):
        _hgrn_chunk(hq_ref, hf_ref, hi_ref, hg_ref, lb_ref, g_ref, o_ref, st_ref, slice(c0, c0 + chunk), heads=heads)

    @pl.when(c == pl.num_programs(1) - 1)
    def _():
        for h in range(heads):
            sn_ref[0, h] = st_ref[h].T


def _hgrn(hq, hf, hi, hg, lb, g, s0, *, heads, L, chunk):
    B, S, F = hq.shape
    dk = F // heads
    blk = lambda b, c: (b, c, 0)
    fix2 = lambda b, c: (0, 0)
    st4 = lambda b, c: (b, 0, 0, 0)
    in_specs = [pl.BlockSpec((1, L, F), blk)] * 4 + [pl.BlockSpec((1, F), fix2), pl.BlockSpec((1, dk), fix2)]
    args = [hq, hf, hi, hg, lb, g]
    if s0 is not None:
        in_specs.append(pl.BlockSpec((1, heads, dk, dk), st4))
        args.append(s0)
    return pl.pallas_call(
        functools.partial(_hgrn_kernel, heads=heads, has_state=s0 is not None, chunk=chunk),
        grid=(B, S // L),
        in_specs=in_specs,
        out_specs=[pl.BlockSpec((1, L, F), blk), pl.BlockSpec((1, heads, dk, dk), st4)],
        out_shape=[jax.ShapeDtypeStruct((B, S, F), BF16), jax.ShapeDtypeStruct((B, heads, dk, dk), F32)],
        scratch_shapes=[pltpu.VMEM((heads, dk, dk), F32)],
        compiler_params=_cparams("parallel", "arbitrary"),
        name="hgrn",
    )(*args)


def _out_proj_kernel(oa_ref, ob_ref, ga_ref, gb_ref, x_ref, w_ref, o_ref, *, kc):
    acc = x_ref[...]
    for k in range(w_ref.shape[0] // kc):
        sl = slice(k * kc, (k + 1) * kc)
        ga = _sigmoid(ga_ref[:, sl].astype(F32))
        gb = _sigmoid(gb_ref[:, sl].astype(F32))
        m = (ga * oa_ref[:, sl].astype(F32) + gb * ob_ref[:, sl].astype(F32)).astype(BF16)
        acc = acc + _dot(m, w_ref[sl, :])
    o_ref[...] = acc


def _out_proj(oa, ob, ga, gb, x2, w_o, *, tm):
    T, D = x2.shape
    row = lambda i: (i, 0)
    return pl.pallas_call(
        functools.partial(_out_proj_kernel, kc=512),
        grid=(T // tm,),
        in_specs=[pl.BlockSpec((tm, D), row)] * 5 + [pl.BlockSpec((D, D), lambda i: (0, 0))],
        out_specs=pl.BlockSpec((tm, D), row),
        out_shape=jax.ShapeDtypeStruct((T, D), F32),
        compiler_params=_cparams("parallel"),
        name="out_proj",
    )(oa, ob, ga, gb, x2, w_o)


def _split3(x):
    hi = x.astype(BF16)
    r = x - hi.astype(F32)
    mid = r.astype(BF16)
    lo = (r - mid.astype(F32)).astype(BF16)
    return hi, mid, lo


def _router_kernel(hp_ref, hs_ref, g_ref, w3_ref, b_ref, u_ref, idx_ref, gate_ref, cnt_ref, run_ref, *, n_exp, n_p):
    step = pl.program_id(0)

    @pl.when(step == 0)
    def _():
        run_ref[...] = jnp.zeros_like(run_ref)

    h = jnp.where(step < n_p, hp_ref[...], hs_ref[...])
    u = _rms(h, g_ref[...])
    u_ref[...] = _pack_bf16_pairs(u)
    u_hi, u_mid, u_lo = _split3(u)
    e1, e2 = n_exp, 2 * n_exp
    p_hi = _dot(u_hi, w3_ref[...])
    p_mid = _dot(u_mid, w3_ref[:, :e2])
    p_lo = _dot(u_lo, w3_ref[:, :e1])
    logits = (p_hi[:, :e1] + (p_hi[:, e1:e2] + p_mid[:, :e1]) + (p_hi[:, e2:] + p_mid[:, e1:] + p_lo)) + b_ref[...]
    tm = logits.shape[0]
    lane_e = lax.broadcasted_iota(jnp.int32, (tm, n_exp), 1)
    lane_o = lax.broadcasted_iota(jnp.int32, (tm, LANE), 1)
    idx_out = jnp.zeros((tm, LANE), jnp.int32)
    val_out = jnp.zeros((tm, LANE), F32)
    vals, picks = [], []
    for k in range(TOP_K):
        m = jnp.max(logits, axis=-1, keepdims=True)
        i = jnp.min(jnp.where(logits == m, lane_e, n_exp), axis=-1, keepdims=True)
        pick = lane_e == i
        logits = jnp.where(pick, -jnp.inf, logits)
        vals.append(m)
        picks.append(pick)
        idx_out = jnp.where(lane_o == k, i, idx_out)
    es = [jnp.exp(v - vals[0]) for v in vals]
    den = es[0]
    for e in es[1:]:
        den = den + e
    for k in range(TOP_K):
        val_out = jnp.where(lane_o == k, es[k] / den, val_out)
    multi = jnp.zeros((tm, n_exp), F32)
    for pick in picks:
        multi = multi + jnp.where(pick, 1.0, 0.0)
    earlier = lax.broadcasted_iota(jnp.int32, (tm, tm), 1) < lax.broadcasted_iota(jnp.int32, (tm, tm), 0)
    before = _dot(jnp.where(earlier, 1.0, 0.0).astype(BF16), multi.astype(BF16)) + run_ref[...]
    for k in range(TOP_K):
        rank = jnp.sum(jnp.where(picks[k], before, 0.0), axis=-1, keepdims=True).astype(jnp.int32)
        idx_out = jnp.where(lane_o == TOP_K + k, rank, idx_out)
    run_ref[...] += jnp.sum(multi, axis=0, keepdims=True)
    cnt_ref[...] = run_ref[...]
    idx_ref[...] = idx_out
    gate_ref[...] = val_out


def _two_group_specs(shape, n_p):
    return [pl.BlockSpec(shape, lambda i: (jnp.minimum(i, n_p - 1), 0)),
            pl.BlockSpec(shape, lambda i: (jnp.maximum(i - n_p, 0), 0))]


def _router(h2p, h2s, g, w_r, b_r, *, tm):
    (Tp, D), Ts = h2p.shape, h2s.shape[0]
    T = Tp + Ts
    n_exp = w_r.shape[1]
    w3 = jnp.concatenate(_split3(w_r), axis=1)
    row = lambda i: (i, 0)
    fix = lambda i: (0, 0)
    return pl.pallas_call(
        functools.partial(_router_kernel, n_exp=n_exp, n_p=Tp // tm),
        grid=(T // tm,),
        in_specs=_two_group_specs((tm, D), Tp // tm)
                 + [pl.BlockSpec((1, D), fix), pl.BlockSpec((D, 3 * n_exp), fix), pl.BlockSpec((1, n_exp), fix)],
        out_specs=[pl.BlockSpec((tm, D // 2), row), pl.BlockSpec((tm, LANE), row), pl.BlockSpec((tm, LANE), row),
                   pl.BlockSpec((1, n_exp), fix)],
        out_shape=[jax.ShapeDtypeStruct((T, D // 2), jnp.uint32), jax.ShapeDtypeStruct((T, LANE), jnp.int32),
                   jax.ShapeDtypeStruct((T, LANE), F32), jax.ShapeDtypeStruct((1, n_exp), F32)],
        scratch_shapes=[pltpu.VMEM((1, n_exp), F32)],
        compiler_params=_cparams("arbitrary"),
        name="router",
    )(h2p, h2s, g, w3, b_r)


def _row_copy(src_ref, dst_ref, src_row, dst_row, sem):
    return pltpu.make_async_copy(src_ref.at[pl.ds(src_row, 1)], dst_ref.at[pl.ds(dst_row, 1)], sem)


def _dispatch_kernel(zs_ref, ze_ref, dest_hbm, u_ref, xs_hbm, idx_smem, zrow_ref, sem_idx, sem, sem_z, *, R, n_exp):
    i = pl.program_id(0)
    n = pl.num_programs(0)
    slot = i % 2
    N = R * TOP_K

    def idx_copy(step, s):
        return pltpu.make_async_copy(dest_hbm.at[step], idx_smem.at[pl.ds(pl.multiple_of(s * N, N), N)],
                                     sem_idx.at[s])

    def start_rows(s):
        def body(r, c):
            for k in range(TOP_K):
                _row_copy(u_ref, xs_hbm, r, idx_smem[s * N + r * TOP_K + k], sem).start(priority=k % 2)
            return c
        lax.fori_loop(0, R, body, 0, unroll=8)

    def wait_rows():
        def body(r, c):
            for k in range(TOP_K):
                _row_copy(u_ref, xs_hbm, 0, 0, sem).wait()
            return c
        lax.fori_loop(0, R, body, 0, unroll=8)

    def zero_rows(start):
        def per_expert(e, c):
            def body(r, c2):
                cp = _row_copy(zrow_ref, xs_hbm, 0, r, sem_z)
                cp.start() if start else cp.wait()
                return c2
            return lax.fori_loop(zs_ref[e], ze_ref[e], body, c)
        lax.fori_loop(0, n_exp, per_expert, 0)

    @pl.when(i == 0)
    def _():
        zrow_ref[...] = jnp.zeros_like(zrow_ref)
        idx_copy(0, 0).start()

    idx_copy(i, slot).wait()

    @pl.when(i + 1 < n)
    def _():
        idx_copy(i + 1, 1 - slot).start()

    start_rows(slot)

    @pl.when(i == n - 1)
    def _():
        zero_rows(True)

    wait_rows()

    @pl.when(i == n - 1)
    def _():
        zero_rows(False)


def _dispatch(zs, ze, dest2, u, *, P, R):
    n = dest2.shape[0]
    D = u.shape[1]
    grid_spec = pltpu.PrefetchScalarGridSpec(
        num_scalar_prefetch=2,
        grid=(n,),
        in_specs=[pl.BlockSpec(memory_space=pl.ANY), pl.BlockSpec((R, D), lambda i, zs, ze: (i, 0))],
        out_specs=pl.BlockSpec(memory_space=pl.ANY),
        scratch_shapes=[pltpu.SMEM((2 * R * TOP_K,), jnp.int32), pltpu.VMEM((8, D), u.dtype),
                        pltpu.SemaphoreType.DMA((2,)), pltpu.SemaphoreType.DMA, pltpu.SemaphoreType.DMA],
    )
    return pl.pallas_call(
        functools.partial(_dispatch_kernel, R=R, n_exp=zs.shape[0]),
        grid_spec=grid_spec,
        out_shape=jax.ShapeDtypeStruct((P, D), u.dtype),
        compiler_params=_cparams("arbitrary"),
        name="moe_dispatch",
    )(zs, ze, dest2, u)


def _expert_kernel(te_ref, nv_ref, x_ref, wg_ref, wu_ref, bg_ref, bu_ref, wd_ref, bd_ref, y_ref, xb_ref):
    i = pl.program_id(0)
    c = pl.program_id(1)
    half = x_ref.shape[1]

    live = nv_ref[i] > 0

    @pl.when(c == 0)
    def _():
        hi, lo = _unpack_bf16_pairs(x_ref[...])
        xb_ref[:, :half] = hi
        xb_ref[:, half:] = lo

    def chunk(first):
        xb = xb_ref[...]
        gate = jnp.minimum(_dot(xb, wg_ref[0]) + bg_ref[0], SWIGLU_LIMIT)
        up = jnp.clip(_dot(xb, wu_ref[0]) + bu_ref[0], -SWIGLU_LIMIT, SWIGLU_LIMIT)
        act = gate * _sigmoid(SWIGLU_ALPHA * gate) * (up + 1.0)
        down = _dot(act.astype(BF16), wd_ref[0])
        if first:
            y_ref[...] = down + bd_ref[0]
        else:
            y_ref[...] += down

    @pl.when(jnp.logical_and(live, c == 0))
    def _():
        chunk(True)

    @pl.when(jnp.logical_and(live, c > 0))
    def _():
        chunk(False)

    @pl.when(jnp.logical_and(jnp.logical_not(live), c == 0))
    def _():
        y_ref[...] = jnp.broadcast_to(bd_ref[0], y_ref.shape)


def _experts(tile_e, tile_nv, xs, w_gu, b_gu, w_dn, b_dn, *, tm, fc):
    P, half = xs.shape
    D = 2 * half
    E, _, ff2 = w_gu.shape
    ff = ff2 // 2
    nc = ff // fc
    n_tiles = P // tm

    def cc(i, c, nv):
        return jnp.where(nv[i] > 0, c, nc - 1)

    grid_spec = pltpu.PrefetchScalarGridSpec(
        num_scalar_prefetch=2,
        grid=(n_tiles, nc),
        in_specs=[
            pl.BlockSpec((tm, half), lambda i, c, te, nv: (jnp.where(nv[i] > 0, i, 0), 0)),
            pl.BlockSpec((1, D, fc), lambda i, c, te, nv: (te[i], 0, cc(i, c, nv))),
            pl.BlockSpec((1, D, fc), lambda i, c, te, nv: (te[i], 0, nc + cc(i, c, nv))),
            pl.BlockSpec((1, 1, fc), lambda i, c, te, nv: (te[i], 0, cc(i, c, nv))),
            pl.BlockSpec((1, 1, fc), lambda i, c, te, nv: (te[i], 0, nc + cc(i, c, nv))),
            pl.BlockSpec((1, fc, D), lambda i, c, te, nv: (te[i], cc(i, c, nv), 0)),
            pl.BlockSpec((1, 1, D), lambda i, c, te, nv: (te[i], 0, 0)),
        ],
        out_specs=pl.BlockSpec((tm, D), lambda i, c, te, nv: (i, 0)),
        scratch_shapes=[pltpu.VMEM((tm, D), BF16)],
    )
    return pl.pallas_call(
        _expert_kernel,
        grid_spec=grid_spec,
        out_shape=jax.ShapeDtypeStruct((P, D), F32),
        compiler_params=_cparams("arbitrary", "arbitrary"),
        name="moe_experts",
    )(tile_e, tile_nv, xs, w_gu, w_gu, b_gu, b_gu, w_dn, b_dn)


def _combine_kernel(dest_hbm, gate_ref, hp_ref, hs_ref, ys_hbm, g_ref, yp_ref, ysm_ref, buf_ref, idx_smem, sem_idx,
                    sem, *, n_p):
    i = pl.program_id(0)
    n = pl.num_programs(0)
    R = hp_ref.shape[0]
    slot = i % 2
    N = R * TOP_K

    def idx_copy(step, s):
        return pltpu.make_async_copy(dest_hbm.at[step], idx_smem.at[pl.ds(pl.multiple_of(s * N, N), N)],
                                     sem_idx.at[s])

    def start_rows(s):
        def body(r, c):
            for k in range(TOP_K):
                _row_copy(ys_hbm, buf_ref.at[s, k], idx_smem[s * N + r * TOP_K + k], r,
                          sem.at[s]).start(priority=k % 2)
            return c
        lax.fori_loop(0, R, body, 0, unroll=8)

    def wait_rows(s):
        def body(r, c):
            for k in range(TOP_K):
                _row_copy(ys_hbm, buf_ref.at[s, k], 0, r, sem.at[s]).wait()
            return c
        lax.fori_loop(0, R, body, 0, unroll=8)

    @pl.when(i == 0)
    def _():
        cp = idx_copy(0, 0)
        cp.start()
        cp.wait()
        start_rows(0)

        @pl.when(n > 1)
        def _():
            idx_copy(1, 1).start()

    @pl.when(i + 1 < n)
    def _():
        idx_copy(i + 1, 1 - slot).wait()
        start_rows(1 - slot)

    @pl.when(i + 2 < n)
    def _():
        idx_copy(i + 2, slot).start()

    wait_rows(slot)
    is_p = i < n_p
    h = jnp.where(is_p, hp_ref[...], hs_ref[...])
    gate = gate_ref[...]
    for k in range(TOP_K):
        h = h + gate[:, k:k + 1] * buf_ref[slot, k]
    y = _rms(h, g_ref[...])

    @pl.when(is_p)
    def _():
        yp_ref[...] = y

    @pl.when(jnp.logical_not(is_p))
    def _():
        ysm_ref[...] = y


def _combine(dest2, gate, h2p, h2s, ys, g, *, R):
    (Tp, D), Ts = h2p.shape, h2s.shape[0]
    n_p = Tp // R
    row = lambda i: (i, 0)
    return pl.pallas_call(
        functools.partial(_combine_kernel, n_p=n_p),
        grid=((Tp + Ts) // R,),
        in_specs=[pl.BlockSpec(memory_space=pl.ANY), pl.BlockSpec((R, LANE), row)] + _two_group_specs((R, D), n_p)
                 + [pl.BlockSpec(memory_space=pl.ANY), pl.BlockSpec((1, D), lambda i: (0, 0))],
        out_specs=_two_group_specs((R, D), n_p),
        out_shape=[jax.ShapeDtypeStruct((Tp, D), F32), jax.ShapeDtypeStruct((Ts, D), F32)],
        scratch_shapes=[pltpu.VMEM((2, TOP_K, R, D), F32), pltpu.SMEM((2 * R * TOP_K,), jnp.int32),
                        pltpu.SemaphoreType.DMA((2,)), pltpu.SemaphoreType.DMA((2,))],
        compiler_params=_cparams("arbitrary"),
        name="moe_combine",
    )(dest2, gate, h2p, h2s, ys, g)


def _rope_tables(pos, rope, rows):
    half = rope // 2
    inv = ROPE_BASE ** (-jnp.arange(half, dtype=F32) / half)
    ang = pos.astype(F32)[:, None] * inv[None, :]
    cos, sin = jnp.cos(ang), jnp.sin(ang)
    z = jnp.zeros((pos.shape[0], LANE - rope), F32)
    cos_t = jnp.concatenate([cos, cos, z], axis=1)
    sin_t = jnp.concatenate([-sin, sin, z], axis=1)
    rep = max(1, rows // pos.shape[0])
    return jnp.tile(cos_t, (rep, 1)), jnp.tile(sin_t, (rep, 1))


def _swap_halves(w):
    half = w.shape[-1] // 2
    return jnp.concatenate([w[..., half:], w[..., :half]], axis=-1)


def _mixer(x, pos, past, wts, side, *, tm):
    B, S, D = x.shape
    T = B * S
    heads, ql, kl, rope = wts["heads"], wts["ql"], wts["kl"], wts["rope"]
    x2 = x.reshape(T, D)
    cos_t, sin_t = _rope_tables(pos, rope, tm)
    u, cqn, lat, kr, krb = _small_proj(x2, wts["norm_mix_g"], wts["w_small"], wts["q_norm_g"], wts["kv_norm_g"],
                                       cos_t, sin_t, tm=tm, ql=ql, kl=kl, rope=rope)
    tm_big = min(1024, T)
    (hq, hf, hi, hg, ga, gb), side_out = _big_proj(u, wts["w_big"], (BF16, F32, BF16, BF16, BF16, BF16), side,
                                                   tm=tm_big, tn=256)
    qf = _q_proj(cqn, wts["wq_raw"], wts["wq_sw"], cos_t, sin_t, tm=tm, heads=heads)
    F = hq.shape[1]
    r3 = lambda a: a.reshape(B, S, a.shape[1])
    if past is None:
        kf, v = _kv_proj(lat, krb, wts["w_uk"], wts["w_uv"], tm=tm, heads=heads)
        o_a = _attention(r3(qf), r3(kf), r3(v), heads=heads, tq=min(512, S), tk=min(1024, S), hpb=2)
        o_b, st = _hgrn(r3(hq), r3(hf), r3(hi), r3(hg), wts["lb"], wts["hgrn_g"], None, heads=wts["hg_heads"],
                        L=min(4 * CHUNK, S), chunk=CHUNK)
    else:
        cache_lat, cache_kr, state = past
        o_a = _attention_sample(r3(qf), cache_lat, cache_kr, r3(lat), r3(kr), wts["w_uk_t"], wts["w_uv_h"],
                                heads=heads)
        o_b, st = _hgrn(r3(hq), r3(hf), r3(hi), r3(hg), wts["lb"], wts["hgrn_g"], state, heads=wts["hg_heads"], L=S,
                        chunk=S)
    h2 = _out_proj(o_a.reshape(T, D), o_b.reshape(T, F), ga, gb, x2, wts["w_o"], tm=256)
    return h2, lat.reshape(B, S, kl), kr.reshape(B, S, rope), st, side_out


def kernel(x_prompt, x_sample, cache_mla_latent, cache_mla_krope, state_hgrn, norm_mix_g, w_in, q_norm_g, w_uq,
           kv_norm_g, w_uk, w_uv, hgrn_lb_raw, hgrn_norm_g, w_o, norm_ffn_g, w_router, b_router, w_gate_up,
           b_gate_up, w_down, b_down, norm_final_g):
    depth = w_in.shape[0]
    assert depth == 1, "single-layer kernel"
    B, S, D = x_prompt.shape
    Bd, Sd, _ = x_sample.shape
    past_len = cache_mla_latent.shape[2]
    ql, heads, qk = w_uq.shape[1:]
    kl, _, nope = w_uk.shape[1:]
    vd = w_uv.shape[3]
    rope = qk - nope
    hg_heads, dk, dv = state_hgrn.shape[2:]
    n_exp, _, ff2 = w_gate_up.shape[1:]
    assert nope == LANE and vd == LANE and dk == LANE and dv == LANE and rope <= LANE
    hf_dim = hg_heads * dk
    scale = 1.0 / math.sqrt(nope + rope)
    l = 0

    wi = w_in[l]
    o_cq, o_kv, o_pe, o_hq = 0, ql, ql + kl, ql + kl + rope
    k_pe = wi[:, o_pe:o_pe + rope]
    zpad = jnp.zeros((D, LANE - rope), F32)
    w_small = jnp.concatenate([wi[:, o_cq:o_pe], k_pe, zpad, _swap_halves(k_pe), zpad], axis=1).astype(BF16)
    segs = [hf_dim, hf_dim, D, D, D, D]
    offs = [o_hq]
    for s_ in segs[:-1]:
        offs.append(offs[-1] + s_)
    w_big = [wi[:, o:o + s_].astype(BF16) for o, s_ in zip(offs, segs)]
    wq = w_uq[l] * scale
    zq = jnp.zeros((ql, heads, LANE - rope), F32)
    wq_raw = jnp.concatenate([wq[..., :nope], wq[..., nope:], zq], axis=-1).reshape(ql, heads * 2 * LANE).astype(BF16)
    wq_sw = jnp.concatenate([_swap_halves(wq[..., nope:]), zq], axis=-1).reshape(ql, heads * LANE).astype(BF16)
    lb_all = jnp.cumsum(jax.nn.softmax(hgrn_lb_raw.astype(F32), axis=0), axis=0)
    wts = dict(
        heads=heads, ql=ql, kl=kl, rope=rope, hg_heads=hg_heads,
        norm_mix_g=norm_mix_g[l][None], w_small=w_small, q_norm_g=q_norm_g[l][None], kv_norm_g=kv_norm_g[l][None],
        w_big=w_big, wq_raw=wq_raw, wq_sw=wq_sw,
        w_uk=w_uk[l].reshape(kl, heads * nope).astype(BF16), w_uv=w_uv[l].reshape(kl, heads * vd).astype(BF16),
        w_uk_t=jnp.transpose(w_uk[l], (1, 2, 0)).astype(BF16), w_uv_h=jnp.transpose(w_uv[l], (1, 0, 2)).astype(BF16),
        lb=lb_all[l][None], hgrn_g=hgrn_norm_g[l][None], w_o=w_o[l].astype(BF16),
    )

    Tp, Ts = B * S, Bd * Sd
    tm = 512
    assert S % tm == 0 or tm % S == 0
    assert Tp % tm == 0 and Ts % tm == 0 and tm % Sd == 0
    h2p, lat_p, kr_p, st_p, (w_gu_b, w_dn_b) = _mixer(x_prompt, jnp.arange(S), None, wts,
                                                       [w_gate_up[l], w_down[l]], tm=tm)
    past = (cache_mla_latent[l], cache_mla_krope[l], state_hgrn[l])
    h2s, lat_s, kr_s, st_s, _ = _mixer(x_sample, past_len + jnp.arange(Sd), past, wts, [], tm=tm)

    T_all = Tp + Ts
    u_all, idx, gate, cnt = _router(h2p, h2s, norm_ffn_g[l][None], w_router[l], b_router[l][None], tm=tm)

    te_rows = 512
    A = T_all * TOP_K
    counts = cnt[0].astype(jnp.int32)
    padded = (counts + te_rows - 1) // te_rows * te_rows
    pad_end = jnp.cumsum(padded)
    pad_start = pad_end - padded
    used_end = pad_start + counts
    e_sel, rank = idx[:, :TOP_K], idx[:, TOP_K:2 * TOP_K]
    onehot = e_sel[:, :, None] == jnp.arange(n_exp, dtype=jnp.int32)[None, None, :]
    dest = (rank + jnp.sum(jnp.where(onehot, pad_start[None, None, :], 0), axis=-1)).reshape(A)
    n_tiles = -(-A // te_rows) + n_exp
    P = n_tiles * te_rows
    tile_start = jnp.arange(n_tiles, dtype=jnp.int32) * te_rows
    tile_e = jnp.minimum(jnp.sum((tile_start[:, None] >= pad_end[None, :]).astype(jnp.int32), axis=1), n_exp - 1)
    tile_nv = jnp.clip(used_end[tile_e] - tile_start, 0, te_rows).astype(jnp.int32)
    tile_nv = jnp.where(tile_start < pad_end[-1], tile_nv, 0)
    last_e = tile_e[jnp.maximum(jnp.sum((tile_start < pad_end[-1]).astype(jnp.int32)) - 1, 0)]
    tile_e = jnp.where(tile_nv > 0, tile_e, last_e)

    dR = 512
    xs = _dispatch(used_end.astype(jnp.int32), pad_end.astype(jnp.int32),
                   dest.reshape(T_all // dR, dR * TOP_K), u_all, P=P, R=dR)
    ys = _experts(tile_e, tile_nv, xs, w_gu_b, b_gate_up[l][:, None, :], w_dn_b, b_down[l][:, None, :],
                  tm=te_rows, fc=1024)

    cR = 256
    y_p, y_s = _combine(dest.reshape(T_all // cR, cR * TOP_K), gate, h2p, h2s, ys, norm_final_g[None], R=cR)

    return (y_p.reshape(B, S, D), y_s.reshape(Bd, Sd, D), lat_p[None], kr_p[None], st_p[None],
            lat_s[None], kr_s[None], st_s[None])
```

```python
import functools
import math

import jax
import jax.numpy as jnp
from jax import lax
from jax.experimental import pallas as pl
from jax.experimental.pallas import tpu as pltpu

F32 = jnp.float32
BF16 = jnp.bfloat16

CHUNK = 64
NORM_EPS = 1e-6
ROPE_BASE = 10000.0
NEG_INF = -1e30
TOP_K = 4
SWIGLU_ALPHA = 1.702
SWIGLU_LIMIT = 7.0

LANE = 128
VMEM_LIMIT = 56 * 1024 * 1024


def _cparams(*sem):
    return pltpu.CompilerParams(dimension_semantics=sem, vmem_limit_bytes=VMEM_LIMIT)


def _dot(a, b):
    return jnp.dot(a, b, preferred_element_type=F32)


def _dot_nt(a, b):
    return lax.dot_general(a, b, (((1,), (1,)), ((), ())), preferred_element_type=F32)


def _rms(x, g):
    return x * lax.rsqrt(jnp.mean(x * x, axis=-1, keepdims=True) + NORM_EPS) * g


def _sigmoid(x):
    return 1.0 / (1.0 + jnp.exp(-x))


def _pack_bf16_pairs(x):
    half = x.shape[1] // 2
    bits = lax.bitcast_convert_type(x.astype(BF16).astype(F32), jnp.uint32)
    return (bits[:, :half] & jnp.uint32(0xFFFF0000)) | (bits[:, half:] >> 16)


def _unpack_bf16_pairs(p):
    hi = lax.bitcast_convert_type(p & jnp.uint32(0xFFFF0000), F32).astype(BF16)
    lo = lax.bitcast_convert_type(p << 16, F32).astype(BF16)
    return hi, lo


def _small_proj_kernel(x_ref, g_ref, w_ref, qg_ref, kvg_ref, cos_ref, sin_ref,
                       u_ref, cqn_ref, lat_ref, kr_ref, krb_ref, *, ql, kl, rope):
    u = _rms(x_ref[...], g_ref[...]).astype(BF16)
    u_ref[...] = u
    y = _dot(u, w_ref[...])
    cqn_ref[...] = _rms(y[:, :ql], qg_ref[...]).astype(BF16)
    lat_ref[...] = _rms(y[:, ql:ql + kl], kvg_ref[...])
    o = ql + kl
    kr = y[:, o:o + LANE] * cos_ref[...] + y[:, o + LANE:o + 2 * LANE] * sin_ref[...]
    kr_ref[...] = kr[:, :rope]
    krb_ref[...] = kr.astype(BF16)


def _small_proj(x2, g, w_small, qg, kvg, cos_t, sin_t, *, tm, ql, kl, rope):
    T, D = x2.shape
    n = w_small.shape[1]
    nblk = cos_t.shape[0] // tm
    row = lambda i: (i, 0)
    fix = lambda i: (0, 0)
    tab = lambda i: (i % nblk, 0)
    return pl.pallas_call(
        functools.partial(_small_proj_kernel, ql=ql, kl=kl, rope=rope),
        grid=(T // tm,),
        in_specs=[pl.BlockSpec((tm, D), row), pl.BlockSpec((1, D), fix), pl.BlockSpec((D, n), fix),
                  pl.BlockSpec((1, ql), fix), pl.BlockSpec((1, kl), fix),
                  pl.BlockSpec((tm, LANE), tab), pl.BlockSpec((tm, LANE), tab)],
        out_specs=[pl.BlockSpec((tm, D), row), pl.BlockSpec((tm, ql), row), pl.BlockSpec((tm, kl), row),
                   pl.BlockSpec((tm, rope), row), pl.BlockSpec((tm, LANE), row)],
        out_shape=[jax.ShapeDtypeStruct((T, D), BF16), jax.ShapeDtypeStruct((T, ql), BF16),
                   jax.ShapeDtypeStruct((T, kl), F32), jax.ShapeDtypeStruct((T, rope), F32),
                   jax.ShapeDtypeStruct((T, LANE), BF16)],
        compiler_params=_cparams("parallel"),
        name="small_proj",
    )(x2, g, w_small, qg, kvg, cos_t, sin_t)


def _big_proj_kernel(u_ref, *refs, nseg, nside):
    ws, sides = refs[:nseg], refs[nseg:nseg + nside]
    outs, side_outs = refs[nseg + nside:2 * nseg + nside], refs[2 * nseg + nside:]
    u = u_ref[...]
    for w_ref, o_ref in zip(ws, outs):
        o_ref[...] = _dot(u, w_ref[...]).astype(o_ref.dtype)
    for s_ref, o_ref in zip(sides, side_outs):
        o_ref[...] = s_ref[...].astype(o_ref.dtype)


def _big_proj(u, ws, out_dtypes, side, *, tm, tn):
    T, D = u.shape
    n = ws[0].shape[1]
    nseg = len(ws)
    nj = n // tn
    steps = (T // tm) * nj
    side2 = [a.reshape(-1, a.shape[-1]) for a in side]
    side_blocks = [(a.shape[0] // steps, a.shape[1]) for a in side2]
    assert all(a.shape[0] % steps == 0 and b[0] % 16 == 0 for a, b in zip(side2, side_blocks))
    slab = lambda i, j: (i * nj + j, 0)
    res = pl.pallas_call(
        functools.partial(_big_proj_kernel, nseg=nseg, nside=len(side)),
        grid=(T // tm, nj),
        in_specs=[pl.BlockSpec((tm, D), lambda i, j: (i, 0))]
                 + [pl.BlockSpec((D, tn), lambda i, j: (0, j)) for _ in ws]
                 + [pl.BlockSpec(b, slab) for b in side_blocks],
        out_specs=[pl.BlockSpec((tm, tn), lambda i, j: (i, j)) for _ in ws]
                  + [pl.BlockSpec(b, slab) for b in side_blocks],
        out_shape=[jax.ShapeDtypeStruct((T, n), dt) for dt in out_dtypes]
                  + [jax.ShapeDtypeStruct(a.shape, BF16) for a in side2],
        compiler_params=_cparams("arbitrary", "arbitrary"),
        name="big_proj",
    )(u, *ws, *side2)
    return res[:nseg], [o.reshape(a.shape) for o, a in zip(res[nseg:], side)]


def _q_proj_kernel(c_ref, wr_ref, ws_ref, cos_ref, sin_ref, q_ref, *, heads):
    c = c_ref[...]
    raw = _dot(c, wr_ref[...])
    sw = _dot(c, ws_ref[...])
    cos = cos_ref[...]
    sin = sin_ref[...]
    for h in range(heads):
        b = 2 * LANE * h
        q_ref[:, b:b + LANE] = raw[:, b:b + LANE].astype(BF16)
        r = raw[:, b + LANE:b + 2 * LANE] * cos + sw[:, LANE * h:LANE * (h + 1)] * sin
        q_ref[:, b + LANE:b + 2 * LANE] = r.astype(BF16)


def _q_proj(cqn, wq_raw, wq_sw, cos_t, sin_t, *, tm, heads):
    T, ql = cqn.shape
    nblk = cos_t.shape[0] // tm
    row = lambda i: (i, 0)
    fix = lambda i: (0, 0)
    tab = lambda i: (i % nblk, 0)
    n = wq_raw.shape[1]
    return pl.pallas_call(
        functools.partial(_q_proj_kernel, heads=heads),
        grid=(T // tm,),
        in_specs=[pl.BlockSpec((tm, ql), row), pl.BlockSpec(wq_raw.shape, fix), pl.BlockSpec(wq_sw.shape, fix),
                  pl.BlockSpec((tm, LANE), tab), pl.BlockSpec((tm, LANE), tab)],
        out_specs=pl.BlockSpec((tm, n), row),
        out_shape=jax.ShapeDtypeStruct((T, n), BF16),
        compiler_params=_cparams("parallel"),
        name="q_proj",
    )(cqn, wq_raw, wq_sw, cos_t, sin_t)


def _kv_proj_kernel(lat_ref, krb_ref, wk_ref, wv_ref, k_ref, v_ref, *, heads):
    lat = lat_ref[...].astype(BF16)
    kn = _dot(lat, wk_ref[...]).astype(BF16)
    v_ref[...] = _dot(lat, wv_ref[...]).astype(BF16)
    krb = krb_ref[...]
    for h in range(heads):
        b = 2 * LANE * h
        k_ref[:, b:b + LANE] = kn[:, LANE * h:LANE * (h + 1)]
        k_ref[:, b + LANE:b + 2 * LANE] = krb


def _kv_proj(lat, krb, wk, wv, *, tm, heads):
    T, kl = lat.shape
    row = lambda i: (i, 0)
    fix = lambda i: (0, 0)
    return pl.pallas_call(
        functools.partial(_kv_proj_kernel, heads=heads),
        grid=(T // tm,),
        in_specs=[pl.BlockSpec((tm, kl), row), pl.BlockSpec((tm, LANE), row),
                  pl.BlockSpec(wk.shape, fix), pl.BlockSpec(wv.shape, fix)],
        out_specs=[pl.BlockSpec((tm, 2 * LANE * heads), row), pl.BlockSpec((tm, wv.shape[1]), row)],
        out_shape=[jax.ShapeDtypeStruct((T, 2 * LANE * heads), BF16), jax.ShapeDtypeStruct((T, wv.shape[1]), BF16)],
        compiler_params=_cparams("parallel"),
        name="kv_proj",
    )(lat, krb, wk, wv)


def _attn_body(q_ref, k_ref, v_ref, o_ref, *, tq, tk, hpb):
    S = q_ref.shape[1]
    kd = 2 * LANE
    vd = v_ref.shape[2] // hpb

    def update(carry, s, v):
        m, l, acc = carry
        m_new = jnp.maximum(m, jnp.max(s, axis=-1, keepdims=True))
        alpha = jnp.exp(m - m_new)
        p = jnp.exp(s - m_new)
        l = alpha * l + jnp.sum(p, axis=-1, keepdims=True)
        acc = alpha * acc + _dot(p.astype(BF16), v)
        return m_new, l, acc

    for r0 in range(0, S, tq):
        seen = r0 + tq
        blocks = [(k0, min(tk, seen - k0)) for k0 in range(0, seen, tk)]
        for g in range(hpb):
            q = q_ref[0, r0:r0 + tq, g * kd:(g + 1) * kd]
            carry = (jnp.full((tq, 1), NEG_INF, F32), jnp.zeros((tq, 1), F32), jnp.zeros((tq, vd), F32))
            for k0, kl in blocks:
                k = k_ref[0, k0:k0 + kl, g * kd:(g + 1) * kd]
                v = v_ref[0, k0:k0 + kl, g * vd:(g + 1) * vd]
                s = _dot_nt(q, k)
                if k0 + kl > r0:
                    qc = (lax.broadcasted_iota(jnp.int32, (tq, kl), 0) + r0) // CHUNK
                    kc = (lax.broadcasted_iota(jnp.int32, (tq, kl), 1) + k0) // CHUNK
                    s = jnp.where(kc <= qc, s, NEG_INF)
                carry = update(carry, s, v)
            m, l, acc = carry
            o_ref[0, r0:r0 + tq, g * vd:(g + 1) * vd] = (acc / l).astype(BF16)


def _attn_kernel(q_ref, k_ref, v_ref, o_ref, *, tq, tk, hpb):
    _attn_body(q_ref, k_ref, v_ref, o_ref, tq=tq, tk=tk, hpb=hpb)


def _attention(qf, kf, v, *, heads, tq, tk, hpb):
    B, S, _ = qf.shape
    vd = v.shape[2] // heads
    seq = lambda b, h: (b, 0, h)
    return pl.pallas_call(
        functools.partial(_attn_kernel, tq=tq, tk=tk, hpb=hpb),
        grid=(B, heads // hpb),
        in_specs=[pl.BlockSpec((1, S, 2 * LANE * hpb), seq), pl.BlockSpec((1, S, 2 * LANE * hpb), seq),
                  pl.BlockSpec((1, S, vd * hpb), seq)],
        out_specs=pl.BlockSpec((1, S, vd * hpb), seq),
        out_shape=jax.ShapeDtypeStruct((B, S, heads * vd), BF16),
        compiler_params=_cparams("parallel", "parallel"),
        name="mla_prompt_attn",
    )(qf, kf, v)


def _attn_sample_kernel(q_ref, cl_ref, ck_ref, ln_ref, kn_ref, wuk_ref, wuv_ref, o_ref,
                        ql_ref, qr_ref, ol_ref, *, heads, rope):
    sd = q_ref.shape[1]
    for h in range(heads):
        b = 2 * LANE * h
        ql_ref[h * sd:(h + 1) * sd, :] = _dot(q_ref[0, :, b:b + LANE], wuk_ref[h]).astype(BF16)
        qr_ref[h * sd:(h + 1) * sd, :] = q_ref[0, :, b + LANE:b + 2 * LANE]
    ql = ql_ref[...]
    qr = qr_ref[...][:, :rope]
    cl = cl_ref[0].astype(BF16)
    ck = ck_ref[0].astype(BF16)
    ln = ln_ref[0].astype(BF16)
    kn = kn_ref[0].astype(BF16)
    s_past = _dot_nt(ql, cl) + _dot_nt(qr, ck)
    s_new = _dot_nt(ql, ln) + _dot_nt(qr, kn)
    m = jnp.maximum(jnp.max(s_past, axis=-1, keepdims=True), jnp.max(s_new, axis=-1, keepdims=True))
    p_past = jnp.exp(s_past - m)
    p_new = jnp.exp(s_new - m)
    l = jnp.sum(p_past, axis=-1, keepdims=True) + jnp.sum(p_new, axis=-1, keepdims=True)
    o_lat = (_dot(p_past.astype(BF16), cl) + _dot(p_new.astype(BF16), ln)) / l
    ol_ref[...] = o_lat.astype(BF16)
    vd = wuv_ref.shape[2]
    for h in range(heads):
        o_ref[0, :, h * vd:(h + 1) * vd] = _dot(ol_ref[h * sd:(h + 1) * sd, :], wuv_ref[h]).astype(BF16)


def _attention_sample(qf, cache_lat, cache_kr, lat_new, kr_new, wuk_t, wuv, *, heads):
    B, sd, _ = qf.shape
    P, kl = cache_lat.shape[1:]
    rope = cache_kr.shape[2]
    vd = wuv.shape[2]
    b3 = lambda b: (b, 0, 0)
    fix = lambda b: (0, 0, 0)
    return pl.pallas_call(
        functools.partial(_attn_sample_kernel, heads=heads, rope=rope),
        grid=(B,),
        in_specs=[pl.BlockSpec((1, sd, qf.shape[2]), b3), pl.BlockSpec((1, P, kl), b3), pl.BlockSpec((1, P, rope), b3),
                  pl.BlockSpec((1, sd, kl), b3), pl.BlockSpec((1, sd, rope), b3),
                  pl.BlockSpec(wuk_t.shape, fix), pl.BlockSpec(wuv.shape, fix)],
        out_specs=pl.BlockSpec((1, sd, heads * vd), b3),
        out_shape=jax.ShapeDtypeStruct((B, sd, heads * vd), BF16),
        scratch_shapes=[pltpu.VMEM((heads * sd, kl), BF16), pltpu.VMEM((heads * sd, LANE), BF16),
                        pltpu.VMEM((heads * sd, kl), BF16)],
        compiler_params=_cparams("parallel"),
        name="mla_sample_attn",
    )(qf, cache_lat, cache_kr, lat_new, kr_new, wuk_t, wuv)


def _hgrn_chunk(hq_ref, hf_ref, hi_ref, hg_ref, lb_ref, g_ref, o_ref, st_ref, rows, *, heads):
    L = rows.stop - rows.start
    dk = st_ref.shape[2]
    row = lax.broadcasted_iota(jnp.int32, (L, L), 0)
    col = lax.broadcasted_iota(jnp.int32, (L, L), 1)
    causal = col <= row
    tri = jnp.where(causal, 1.0, 0.0).astype(BF16)
    lb = lb_ref[...]
    f = lb + (1.0 - lb) * _sigmoid(hf_ref[0, rows, :])
    kk = 1.0 - f
    hi, mid, lo = _split3(jnp.log(f))
    cum = _dot(tri, hi) + _dot(tri, mid) + _dot(tri, lo)
    tot = cum[L - 1:L, :]
    q_dec = (hq_ref[0, rows, :].astype(F32) * jnp.exp(cum)).astype(BF16)
    k_dec = (kk * jnp.exp(-cum)).astype(BF16)
    k2 = (kk * jnp.exp(tot - cum)).astype(BF16)
    decay = jnp.exp(tot)
    vb = hi_ref[0, rows, :]
    v32 = vb.astype(F32)
    sls = [slice(h * dk, (h + 1) * dk) for h in range(heads)]
    atts = [jnp.where(causal, _dot_nt(q_dec[:, sl], k_dec[:, sl]), 0.0).astype(BF16) for sl in sls]
    sts = [st_ref[h] for h in range(heads)]
    outs = [_dot_nt(q_dec[:, sl], sts[h].astype(BF16)) + _dot(atts[h], vb[:, sl]) for h, sl in enumerate(sls)]
    for h, sl in enumerate(sls):
        st_ref[h] = decay[:, sl] * sts[h] + _dot(v32[:, sl].T.astype(BF16), k2[:, sl])
    g = g_ref[...]
    for h, sl in enumerate(sls):
        o = outs[h]
        on = o * lax.rsqrt(jnp.mean(o * o, axis=-1, keepdims=True) + NORM_EPS) * g
        hg = hg_ref[0, rows, sl].astype(F32)
        o_ref[0, rows, sl] = (on * (hg * _sigmoid(hg))).astype(BF16)


def _hgrn_kernel(*refs, heads, has_state, chunk):
    if has_state:
        hq_ref, hf_ref, hi_ref, hg_ref, lb_ref, g_ref, s0_ref, o_ref, sn_ref, st_ref = refs
    else:
        hq_ref, hf_ref, hi_ref, hg_ref, lb_ref, g_ref, o_ref, sn_ref, st_ref = refs
    c = pl.program_id(1)

    @pl.when(c == 0)
    def _():
        if has_state:
            for h in range(heads):
                st_ref[h] = s0_ref[0, h].T
        else:
            st_ref[...] = jnp.zeros_like(st_ref)

    for c0 in range(0, hq_ref.shape[1], chunk):
        _hgrn_chunk(hq_ref, hf_ref, hi_ref, hg_ref, lb_ref, g_ref, o_ref, st_ref, slice(c0, c0 + chunk), heads=heads)

    @pl.when(c == pl.num_programs(1) - 1)
    def _():
        for h in range(heads):
            sn_ref[0, h] = st_ref[h].T


def _hgrn(hq, hf, hi, hg, lb, g, s0, *, heads, L, chunk):
    B, S, F = hq.shape
    dk = F // heads
    blk = lambda b, c: (b, c, 0)
    fix2 = lambda b, c: (0, 0)
    st4 = lambda b, c: (b, 0, 0, 0)
    in_specs = [pl.BlockSpec((1, L, F), blk)] * 4 + [pl.BlockSpec((1, F), fix2), pl.BlockSpec((1, dk), fix2)]
    args = [hq, hf, hi, hg, lb, g]
    if s0 is not None:
        in_specs.append(pl.BlockSpec((1, heads, dk, dk), st4))
        args.append(s0)
    return pl.pallas_call(
        functools.partial(_hgrn_kernel, heads=heads, has_state=s0 is not None, chunk=chunk),
        grid=(B, S // L),
        in_specs=in_specs,
        out_specs=[pl.BlockSpec((1, L, F), blk), pl.BlockSpec((1, heads, dk, dk), st4)],
        out_shape=[jax.ShapeDtypeStruct((B, S, F), BF16), jax.ShapeDtypeStruct((B, heads, dk, dk), F32)],
        scratch_shapes=[pltpu.VMEM((heads, dk, dk), F32)],
        compiler_params=_cparams("parallel", "arbitrary"),
        name="hgrn",
    )(*args)


def _out_proj_kernel(oa_ref, ob_ref, ga_ref, gb_ref, x_ref, w_ref, o_ref, *, kc):
    acc = x_ref[...]
    for k in range(w_ref.shape[0] // kc):
        sl = slice(k * kc, (k + 1) * kc)
        ga = _sigmoid(ga_ref[:, sl].astype(F32))
        gb = _sigmoid(gb_ref[:, sl].astype(F32))
        m = (ga * oa_ref[:, sl].astype(F32) + gb * ob_ref[:, sl].astype(F32)).astype(BF16)
        acc = acc + _dot(m, w_ref[sl, :])
    o_ref[...] = acc


def _out_proj(oa, ob, ga, gb, x2, w_o, *, tm):
    T, D = x2.shape
    row = lambda i: (i, 0)
    return pl.pallas_call(
        functools.partial(_out_proj_kernel, kc=512),
        grid=(T // tm,),
        in_specs=[pl.BlockSpec((tm, D), row)] * 5 + [pl.BlockSpec((D, D), lambda i: (0, 0))],
        out_specs=pl.BlockSpec((tm, D), row),
        out_shape=jax.ShapeDtypeStruct((T, D), F32),
        compiler_params=_cparams("parallel"),
        name="out_proj",
    )(oa, ob, ga, gb, x2, w_o)


def _split3(x):
    hi = x.astype(BF16)
    r = x - hi.astype(F32)
    mid = r.astype(BF16)
    lo = (r - mid.astype(F32)).astype(BF16)
    return hi, mid, lo


def _router_kernel(hp_ref, hs_ref, g_ref, w3_ref, b_ref, u_ref, idx_ref, gate_ref, cnt_ref, run_ref, *, n_exp, n_p):
    step = pl.program_id(0)

    @pl.when(step == 0)
    def _():
        run_ref[...] = jnp.zeros_like(run_ref)

    h = jnp.where(step < n_p, hp_ref[...], hs_ref[...])
    u = _rms(h, g_ref[...])
    u_ref[...] = _pack_bf16_pairs(u)
    u_hi, u_mid, u_lo = _split3(u)
    e1, e2 = n_exp, 2 * n_exp
    p_hi = _dot(u_hi, w3_ref[...])
    p_mid = _dot(u_mid, w3_ref[:, :e2])
    p_lo = _dot(u_lo, w3_ref[:, :e1])
    logits = (p_hi[:, :e1] + (p_hi[:, e1:e2] + p_mid[:, :e1]) + (p_hi[:, e2:] + p_mid[:, e1:] + p_lo)) + b_ref[...]
    tm = logits.shape[0]
    lane_e = lax.broadcasted_iota(jnp.int32, (tm, n_exp), 1)
    lane_o = lax.broadcasted_iota(jnp.int32, (tm, LANE), 1)
    idx_out = jnp.zeros((tm, LANE), jnp.int32)
    val_out = jnp.zeros((tm, LANE), F32)
    vals, picks = [], []
    for k in range(TOP_K):
        m = jnp.max(logits, axis=-1, keepdims=True)
        i = jnp.min(jnp.where(logits == m, lane_e, n_exp), axis=-1, keepdims=True)
        pick = lane_e == i
        logits = jnp.where(pick, -jnp.inf, logits)
        vals.append(m)
        picks.append(pick)
        idx_out = jnp.where(lane_o == k, i, idx_out)
    es = [jnp.exp(v - vals[0]) for v in vals]
    den = es[0]
    for e in es[1:]:
        den = den + e
    for k in range(TOP_K):
        val_out = jnp.where(lane_o == k, es[k] / den, val_out)
    multi = jnp.zeros((tm, n_exp), F32)
    for pick in picks:
        multi = multi + jnp.where(pick, 1.0, 0.0)
    earlier = lax.broadcasted_iota(jnp.int32, (tm, tm), 1) < lax.broadcasted_iota(jnp.int32, (tm, tm), 0)
    before = _dot(jnp.where(earlier, 1.0, 0.0).astype(BF16), multi.astype(BF16)) + run_ref[...]
    for k in range(TOP_K):
        rank = jnp.sum(jnp.where(picks[k], before, 0.0), axis=-1, keepdims=True).astype(jnp.int32)
        idx_out = jnp.where(lane_o == TOP_K + k, rank, idx_out)
    run_ref[...] += jnp.sum(multi, axis=0, keepdims=True)
    cnt_ref[...] = run_ref[...]
    idx_ref[...] = idx_out
    gate_ref[...] = val_out


def _two_group_specs(shape, n_p):
    return [pl.BlockSpec(shape, lambda i: (jnp.minimum(i, n_p - 1), 0)),
            pl.BlockSpec(shape, lambda i: (jnp.maximum(i - n_p, 0), 0))]


def _router(h2p, h2s, g, w_r, b_r, *, tm):
    (Tp, D), Ts = h2p.shape, h2s.shape[0]
    T = Tp + Ts
    n_exp = w_r.shape[1]
    w3 = jnp.concatenate(_split3(w_r), axis=1)
    row = lambda i: (i, 0)
    fix = lambda i: (0, 0)
    return pl.pallas_call(
        functools.partial(_router_kernel, n_exp=n_exp, n_p=Tp // tm),
        grid=(T // tm,),
        in_specs=_two_group_specs((tm, D), Tp // tm)
                 + [pl.BlockSpec((1, D), fix), pl.BlockSpec((D, 3 * n_exp), fix), pl.BlockSpec((1, n_exp), fix)],
        out_specs=[pl.BlockSpec((tm, D // 2), row), pl.BlockSpec((tm, LANE), row), pl.BlockSpec((tm, LANE), row),
                   pl.BlockSpec((1, n_exp), fix)],
        out_shape=[jax.ShapeDtypeStruct((T, D // 2), jnp.uint32), jax.ShapeDtypeStruct((T, LANE), jnp.int32),
                   jax.ShapeDtypeStruct((T, LANE), F32), jax.ShapeDtypeStruct((1, n_exp), F32)],
        scratch_shapes=[pltpu.VMEM((1, n_exp), F32)],
        compiler_params=_cparams("arbitrary"),
        name="router",
    )(h2p, h2s, g, w3, b_r)


def _row_copy(src_ref, dst_ref, src_row, dst_row, sem):
    return pltpu.make_async_copy(src_ref.at[pl.ds(src_row, 1)], dst_ref.at[pl.ds(dst_row, 1)], sem)


def _dispatch_kernel(zs_ref, ze_ref, dest_hbm, u_ref, xs_hbm, idx_smem, zrow_ref, sem_idx, sem, sem_z, *, R, n_exp):
    i = pl.program_id(0)
    n = pl.num_programs(0)
    slot = i % 2
    N = R * TOP_K

    def idx_copy(step, s):
        return pltpu.make_async_copy(dest_hbm.at[step], idx_smem.at[pl.ds(pl.multiple_of(s * N, N), N)],
                                     sem_idx.at[s])

    def start_rows(s):
        def body(r, c):
            for k in range(TOP_K):
                _row_copy(u_ref, xs_hbm, r, idx_smem[s * N + r * TOP_K + k], sem).start(priority=k % 2)
            return c
        lax.fori_loop(0, R, body, 0, unroll=8)

    def wait_rows():
        def body(r, c):
            for k in range(TOP_K):
                _row_copy(u_ref, xs_hbm, 0, 0, sem).wait()
            return c
        lax.fori_loop(0, R, body, 0, unroll=8)

    def zero_rows(start):
        def per_expert(e, c):
            def body(r, c2):
                cp = _row_copy(zrow_ref, xs_hbm, 0, r, sem_z)
                cp.start() if start else cp.wait()
                return c2
            return lax.fori_loop(zs_ref[e], ze_ref[e], body, c)
        lax.fori_loop(0, n_exp, per_expert, 0)

    @pl.when(i == 0)
    def _():
        zrow_ref[...] = jnp.zeros_like(zrow_ref)
        idx_copy(0, 0).start()

    idx_copy(i, slot).wait()

    @pl.when(i + 1 < n)
    def _():
        idx_copy(i + 1, 1 - slot).start()

    start_rows(slot)

    @pl.when(i == n - 1)
    def _():
        zero_rows(True)

    wait_rows()

    @pl.when(i == n - 1)
    def _():
        zero_rows(False)


def _dispatch(zs, ze, dest2, u, *, P, R):
    n = dest2.shape[0]
    D = u.shape[1]
    grid_spec = pltpu.PrefetchScalarGridSpec(
        num_scalar_prefetch=2,
        grid=(n,),
        in_specs=[pl.BlockSpec(memory_space=pl.ANY), pl.BlockSpec((R, D), lambda i, zs, ze: (i, 0))],
        out_specs=pl.BlockSpec(memory_space=pl.ANY),
        scratch_shapes=[pltpu.SMEM((2 * R * TOP_K,), jnp.int32), pltpu.VMEM((8, D), u.dtype),
                        pltpu.SemaphoreType.DMA((2,)), pltpu.SemaphoreType.DMA, pltpu.SemaphoreType.DMA],
    )
    return pl.pallas_call(
        functools.partial(_dispatch_kernel, R=R, n_exp=zs.shape[0]),
        grid_spec=grid_spec,
        out_shape=jax.ShapeDtypeStruct((P, D), u.dtype),
        compiler_params=_cparams("arbitrary"),
        name="moe_dispatch",
    )(zs, ze, dest2, u)


def _expert_kernel(te_ref, nv_ref, x_ref, wg_ref, wu_ref, bg_ref, bu_ref, wd_ref, bd_ref, y_ref, xb_ref):
    i = pl.program_id(0)
    c = pl.program_id(1)
    half = x_ref.shape[1]

    live = nv_ref[i] > 0

    def chunk(first):
        if first:
            hi, lo = _unpack_bf16_pairs(x_ref[...])
            xb_ref[:, :half] = hi
            xb_ref[:, half:] = lo
            xb = jnp.concatenate([hi, lo], axis=1)
        else:
            xb = xb_ref[...]
        gate = jnp.minimum(_dot(xb, wg_ref[0]) + bg_ref[0], SWIGLU_LIMIT)
        up = jnp.clip(_dot(xb, wu_ref[0]) + bu_ref[0], -SWIGLU_LIMIT, SWIGLU_LIMIT)
        act = gate * _sigmoid(SWIGLU_ALPHA * gate) * (up + 1.0)
        down = _dot(act.astype(BF16), wd_ref[0])
        if first:
            y_ref[...] = down + bd_ref[0]
        else:
            y_ref[...] += down

    @pl.when(jnp.logical_and(live, c == 0))
    def _():
        chunk(True)

    @pl.when(jnp.logical_and(live, c > 0))
    def _():
        chunk(False)

    @pl.when(jnp.logical_and(jnp.logical_not(live), c == 0))
    def _():
        y_ref[...] = jnp.broadcast_to(bd_ref[0], y_ref.shape)


def _experts(tile_e, tile_nv, xs, w_gu, b_gu, w_dn, b_dn, *, tm, fc):
    P, half = xs.shape
    D = 2 * half
    E, _, ff2 = w_gu.shape
    ff = ff2 // 2
    nc = ff // fc
    n_tiles = P // tm

    def cc(i, c, nv):
        return jnp.where(nv[i] > 0, c, nc - 1)

    grid_spec = pltpu.PrefetchScalarGridSpec(
        num_scalar_prefetch=2,
        grid=(n_tiles, nc),
        in_specs=[
            pl.BlockSpec((tm, half), lambda i, c, te, nv: (jnp.where(nv[i] > 0, i, 0), 0)),
            pl.BlockSpec((1, D, fc), lambda i, c, te, nv: (te[i], 0, cc(i, c, nv))),
            pl.BlockSpec((1, D, fc), lambda i, c, te, nv: (te[i], 0, nc + cc(i, c, nv))),
            pl.BlockSpec((1, 1, fc), lambda i, c, te, nv: (te[i], 0, cc(i, c, nv))),
            pl.BlockSpec((1, 1, fc), lambda i, c, te, nv: (te[i], 0, nc + cc(i, c, nv))),
            pl.BlockSpec((1, fc, D), lambda i, c, te, nv: (te[i], cc(i, c, nv), 0)),
            pl.BlockSpec((1, 1, D), lambda i, c, te, nv: (te[i], 0, 0)),
        ],
        out_specs=pl.BlockSpec((tm, D), lambda i, c, te, nv: (i, 0)),
        scratch_shapes=[pltpu.VMEM((tm, D), BF16)],
    )
    return pl.pallas_call(
        _expert_kernel,
        grid_spec=grid_spec,
        out_shape=jax.ShapeDtypeStruct((P, D), F32),
        compiler_params=_cparams("arbitrary", "arbitrary"),
        name="moe_experts",
    )(tile_e, tile_nv, xs, w_gu, w_gu, b_gu, b_gu, w_dn, b_dn)


def _combine_kernel(dest_hbm, gate_ref, hp_ref, hs_ref, ys_hbm, g_ref, yp_ref, ysm_ref, buf_ref, idx_smem, sem_idx,
                    sem, *, n_p):
    i = pl.program_id(0)
    n = pl.num_programs(0)
    R = hp_ref.shape[0]
    slot = i % 2
    N = R * TOP_K

    def idx_copy(step, s):
        return pltpu.make_async_copy(dest_hbm.at[step], idx_smem.at[pl.ds(pl.multiple_of(s * N, N), N)],
                                     sem_idx.at[s])

    def start_rows(s):
        def body(r, c):
            for k in range(TOP_K):
                _row_copy(ys_hbm, buf_ref.at[s, k], idx_smem[s * N + r * TOP_K + k], r,
                          sem.at[s]).start(priority=k % 2)
            return c
        lax.fori_loop(0, R, body, 0, unroll=8)

    def wait_rows(s):
        def body(r, c):
            for k in range(TOP_K):
                _row_copy(ys_hbm, buf_ref.at[s, k], 0, r, sem.at[s]).wait()
            return c
        lax.fori_loop(0, R, body, 0, unroll=8)

    @pl.when(i == 0)
    def _():
        cp = idx_copy(0, 0)
        cp.start()
        cp.wait()
        start_rows(0)

        @pl.when(n > 1)
        def _():
            idx_copy(1, 1).start()

    @pl.when(i + 1 < n)
    def _():
        idx_copy(i + 1, 1 - slot).wait()
        start_rows(1 - slot)

    @pl.when(i + 2 < n)
    def _():
        idx_copy(i + 2, slot).start()

    wait_rows(slot)
    is_p = i < n_p
    h = jnp.where(is_p, hp_ref[...], hs_ref[...])
    gate = gate_ref[...]
    for k in range(TOP_K):
        h = h + gate[:, k:k + 1] * buf_ref[slot, k]
    y = _rms(h, g_ref[...])

    @pl.when(is_p)
    def _():
        yp_ref[...] = y

    @pl.when(jnp.logical_not(is_p))
    def _():
        ysm_ref[...] = y


def _combine(dest2, gate, h2p, h2s, ys, g, *, R):
    (Tp, D), Ts = h2p.shape, h2s.shape[0]
    n_p = Tp // R
    row = lambda i: (i, 0)
    return pl.pallas_call(
        functools.partial(_combine_kernel, n_p=n_p),
        grid=((Tp + Ts) // R,),
        in_specs=[pl.BlockSpec(memory_space=pl.ANY), pl.BlockSpec((R, LANE), row)] + _two_group_specs((R, D), n_p)
                 + [pl.BlockSpec(memory_space=pl.ANY), pl.BlockSpec((1, D), lambda i: (0, 0))],
        out_specs=_two_group_specs((R, D), n_p),
        out_shape=[jax.ShapeDtypeStruct((Tp, D), F32), jax.ShapeDtypeStruct((Ts, D), F32)],
        scratch_shapes=[pltpu.VMEM((2, TOP_K, R, D), F32), pltpu.SMEM((2 * R * TOP_K,), jnp.int32),
                        pltpu.SemaphoreType.DMA((2,)), pltpu.SemaphoreType.DMA((2,))],
        compiler_params=_cparams("arbitrary"),
        name="moe_combine",
    )(dest2, gate, h2p, h2s, ys, g)


def _rope_tables(pos, rope, rows):
    half = rope // 2
    inv = ROPE_BASE ** (-jnp.arange(half, dtype=F32) / half)
    ang = pos.astype(F32)[:, None] * inv[None, :]
    cos, sin = jnp.cos(ang), jnp.sin(ang)
    z = jnp.zeros((pos.shape[0], LANE - rope), F32)
    cos_t = jnp.concatenate([cos, cos, z], axis=1)
    sin_t = jnp.concatenate([-sin, sin, z], axis=1)
    rep = max(1, rows // pos.shape[0])
    return jnp.tile(cos_t, (rep, 1)), jnp.tile(sin_t, (rep, 1))


def _swap_halves(w):
    half = w.shape[-1] // 2
    return jnp.concatenate([w[..., half:], w[..., :half]], axis=-1)


def _mixer(x, pos, past, wts, side, *, tm):
    B, S, D = x.shape
    T = B * S
    heads, ql, kl, rope = wts["heads"], wts["ql"], wts["kl"], wts["rope"]
    x2 = x.reshape(T, D)
    cos_t, sin_t = _rope_tables(pos, rope, tm)
    u, cqn, lat, kr, krb = _small_proj(x2, wts["norm_mix_g"], wts["w_small"], wts["q_norm_g"], wts["kv_norm_g"],
                                       cos_t, sin_t, tm=tm, ql=ql, kl=kl, rope=rope)
    tm_big = min(1024, T)
    (hq, hf, hi, hg, ga, gb), side_out = _big_proj(u, wts["w_big"], (BF16, F32, BF16, BF16, BF16, BF16), side,
                                                   tm=tm_big, tn=256)
    qf = _q_proj(cqn, wts["wq_raw"], wts["wq_sw"], cos_t, sin_t, tm=tm, heads=heads)
    F = hq.shape[1]
    r3 = lambda a: a.reshape(B, S, a.shape[1])
    if past is None:
        kf, v = _kv_proj(lat, krb, wts["w_uk"], wts["w_uv"], tm=tm, heads=heads)
        o_a = _attention(r3(qf), r3(kf), r3(v), heads=heads, tq=min(512, S), tk=min(1024, S), hpb=2)
        o_b, st = _hgrn(r3(hq), r3(hf), r3(hi), r3(hg), wts["lb"], wts["hgrn_g"], None, heads=wts["hg_heads"],
                        L=min(8 * CHUNK, S), chunk=CHUNK)
    else:
        cache_lat, cache_kr, state = past
        o_a = _attention_sample(r3(qf), cache_lat, cache_kr, r3(lat), r3(kr), wts["w_uk_t"], wts["w_uv_h"],
                                heads=heads)
        o_b, st = _hgrn(r3(hq), r3(hf), r3(hi), r3(hg), wts["lb"], wts["hgrn_g"], state, heads=wts["hg_heads"], L=S,
                        chunk=S)
    h2 = _out_proj(o_a.reshape(T, D), o_b.reshape(T, F), ga, gb, x2, wts["w_o"], tm=256)
    return h2, lat.reshape(B, S, kl), kr.reshape(B, S, rope), st, side_out


def kernel(x_prompt, x_sample, cache_mla_latent, cache_mla_krope, state_hgrn, norm_mix_g, w_in, q_norm_g, w_uq,
           kv_norm_g, w_uk, w_uv, hgrn_lb_raw, hgrn_norm_g, w_o, norm_ffn_g, w_router, b_router, w_gate_up,
           b_gate_up, w_down, b_down, norm_final_g):
    depth = w_in.shape[0]
    assert depth == 1, "single-layer kernel"
    B, S, D = x_prompt.shape
    Bd, Sd, _ = x_sample.shape
    past_len = cache_mla_latent.shape[2]
    ql, heads, qk = w_uq.shape[1:]
    kl, _, nope = w_uk.shape[1:]
    vd = w_uv.shape[3]
    rope = qk - nope
    hg_heads, dk, dv = state_hgrn.shape[2:]
    n_exp, _, ff2 = w_gate_up.shape[1:]
    assert nope == LANE and vd == LANE and dk == LANE and dv == LANE and rope <= LANE
    hf_dim = hg_heads * dk
    scale = 1.0 / math.sqrt(nope + rope)
    l = 0

    wi = w_in[l]
    o_cq, o_kv, o_pe, o_hq = 0, ql, ql + kl, ql + kl + rope
    k_pe = wi[:, o_pe:o_pe + rope]
    zpad = jnp.zeros((D, LANE - rope), F32)
    w_small = jnp.concatenate([wi[:, o_cq:o_pe], k_pe, zpad, _swap_halves(k_pe), zpad], axis=1).astype(BF16)
    segs = [hf_dim, hf_dim, D, D, D, D]
    offs = [o_hq]
    for s_ in segs[:-1]:
        offs.append(offs[-1] + s_)
    w_big = [wi[:, o:o + s_].astype(BF16) for o, s_ in zip(offs, segs)]
    wq = w_uq[l] * scale
    zq = jnp.zeros((ql, heads, LANE - rope), F32)
    wq_raw = jnp.concatenate([wq[..., :nope], wq[..., nope:], zq], axis=-1).reshape(ql, heads * 2 * LANE).astype(BF16)
    wq_sw = jnp.concatenate([_swap_halves(wq[..., nope:]), zq], axis=-1).reshape(ql, heads * LANE).astype(BF16)
    lb_all = jnp.cumsum(jax.nn.softmax(hgrn_lb_raw.astype(F32), axis=0), axis=0)
    wts = dict(
        heads=heads, ql=ql, kl=kl, rope=rope, hg_heads=hg_heads,
        norm_mix_g=norm_mix_g[l][None], w_small=w_small, q_norm_g=q_norm_g[l][None], kv_norm_g=kv_norm_g[l][None],
        w_big=w_big, wq_raw=wq_raw, wq_sw=wq_sw,
        w_uk=w_uk[l].reshape(kl, heads * nope).astype(BF16), w_uv=w_uv[l].reshape(kl, heads * vd).astype(BF16),
        w_uk_t=jnp.transpose(w_uk[l], (1, 2, 0)).astype(BF16), w_uv_h=jnp.transpose(w_uv[l], (1, 0, 2)).astype(BF16),
        lb=lb_all[l][None], hgrn_g=hgrn_norm_g[l][None], w_o=w_o[l].astype(BF16),
    )

    Tp, Ts = B * S, Bd * Sd
    tm = 512
    assert S % tm == 0 or tm % S == 0
    assert Tp % tm == 0 and Ts % tm == 0 and tm % Sd == 0
    h2p, lat_p, kr_p, st_p, (w_gu_b, w_dn_b) = _mixer(x_prompt, jnp.arange(S), None, wts,
                                                       [w_gate_up[l], w_down[l]], tm=tm)
    past = (cache_mla_latent[l], cache_mla_krope[l], state_hgrn[l])
    h2s, lat_s, kr_s, st_s, _ = _mixer(x_sample, past_len + jnp.arange(Sd), past, wts, [], tm=tm)

    T_all = Tp + Ts
    u_all, idx, gate, cnt = _router(h2p, h2s, norm_ffn_g[l][None], w_router[l], b_router[l][None], tm=tm)

    te_rows = 512
    A = T_all * TOP_K
    counts = cnt[0].astype(jnp.int32)
    padded = (counts + te_rows - 1) // te_rows * te_rows
    pad_end = jnp.cumsum(padded)
    pad_start = pad_end - padded
    used_end = pad_start + counts
    e_sel, rank = idx[:, :TOP_K], idx[:, TOP_K:2 * TOP_K]
    onehot = e_sel[:, :, None] == jnp.arange(n_exp, dtype=jnp.int32)[None, None, :]
    dest = (rank + jnp.sum(jnp.where(onehot, pad_start[None, None, :], 0), axis=-1)).reshape(A)
    n_tiles = -(-A // te_rows) + n_exp
    P = n_tiles * te_rows
    tile_start = jnp.arange(n_tiles, dtype=jnp.int32) * te_rows
    tile_e = jnp.minimum(jnp.sum((tile_start[:, None] >= pad_end[None, :]).astype(jnp.int32), axis=1), n_exp - 1)
    tile_nv = jnp.clip(used_end[tile_e] - tile_start, 0, te_rows).astype(jnp.int32)
    tile_nv = jnp.where(tile_start < pad_end[-1], tile_nv, 0)
    last_e = tile_e[jnp.maximum(jnp.sum((tile_start < pad_end[-1]).astype(jnp.int32)) - 1, 0)]
    tile_e = jnp.where(tile_nv > 0, tile_e, last_e)

    dR = 512
    xs = _dispatch(used_end.astype(jnp.int32), pad_end.astype(jnp.int32),
                   dest.reshape(T_all // dR, dR * TOP_K), u_all, P=P, R=dR)
    ys = _experts(tile_e, tile_nv, xs, w_gu_b, b_gate_up[l][:, None, :], w_dn_b, b_down[l][:, None, :],
                  tm=te_rows, fc=1024)

    cR = 256
    y_p, y_s = _combine(dest.reshape(T_all // cR, cR * TOP_K), gate, h2p, h2s, ys, norm_final_g[None], R=cR)

    return (y_p.reshape(B, S, D), y_s.reshape(Bd, Sd, D), lat_p[None], kr_p[None], st_p[None],
            lat_s[None], kr_s[None], st_s[None])
```

```python
import functools
import math

import jax
import jax.numpy as jnp
from jax import lax
from jax.experimental import pallas as pl
from jax.experimental.pallas import tpu as pltpu

F32 = jnp.float32
BF16 = jnp.bfloat16

CHUNK = 64
NORM_EPS = 1e-6
ROPE_BASE = 10000.0
NEG_INF = -1e30
TOP_K = 4
SWIGLU_ALPHA = 1.702
SWIGLU_LIMIT = 7.0

LANE = 128
VMEM_LIMIT = 56 * 1024 * 1024


def _cparams(*sem):
    return pltpu.CompilerParams(dimension_semantics=sem, vmem_limit_bytes=VMEM_LIMIT)


def _dot(a, b):
    return jnp.dot(a, b, preferred_element_type=F32)


def _dot_nt(a, b):
    return lax.dot_general(a, b, (((1,), (1,)), ((), ())), preferred_element_type=F32)


def _rms(x, g):
    return x * lax.rsqrt(jnp.mean(x * x, axis=-1, keepdims=True) + NORM_EPS) * g


def _sigmoid(x):
    return 1.0 / (1.0 + jnp.exp(-x))


def _pack_bf16_pairs(x):
    half = x.shape[1] // 2
    bits = lax.bitcast_convert_type(x.astype(BF16).astype(F32), jnp.uint32)
    return (bits[:, :half] & jnp.uint32(0xFFFF0000)) | (bits[:, half:] >> 16)


def _unpack_bf16_pairs(p):
    hi = lax.bitcast_convert_type(p & jnp.uint32(0xFFFF0000), F32).astype(BF16)
    lo = lax.bitcast_convert_type(p << 16, F32).astype(BF16)
    return hi, lo


def _small_proj_kernel(x_ref, g_ref, w_ref, qg_ref, kvg_ref, cos_ref, sin_ref,
                       u_ref, cqn_ref, lat_ref, kr_ref, krb_ref, *, ql, kl, rope):
    u = _rms(x_ref[...], g_ref[...]).astype(BF16)
    u_ref[...] = u
    y = _dot(u, w_ref[...])
    cqn_ref[...] = _rms(y[:, :ql], qg_ref[...]).astype(BF16)
    lat_ref[...] = _rms(y[:, ql:ql + kl], kvg_ref[...])
    o = ql + kl
    kr = y[:, o:o + LANE] * cos_ref[...] + y[:, o + LANE:o + 2 * LANE] * sin_ref[...]
    kr_ref[...] = kr[:, :rope]
    krb_ref[...] = kr.astype(BF16)


def _small_proj(x2, g, w_small, qg, kvg, cos_t, sin_t, *, tm, ql, kl, rope):
    T, D = x2.shape
    n = w_small.shape[1]
    nblk = cos_t.shape[0] // tm
    row = lambda i: (i, 0)
    fix = lambda i: (0, 0)
    tab = lambda i: (i % nblk, 0)
    return pl.pallas_call(
        functools.partial(_small_proj_kernel, ql=ql, kl=kl, rope=rope),
        grid=(T // tm,),
        in_specs=[pl.BlockSpec((tm, D), row), pl.BlockSpec((1, D), fix), pl.BlockSpec((D, n), fix),
                  pl.BlockSpec((1, ql), fix), pl.BlockSpec((1, kl), fix),
                  pl.BlockSpec((tm, LANE), tab), pl.BlockSpec((tm, LANE), tab)],
        out_specs=[pl.BlockSpec((tm, D), row), pl.BlockSpec((tm, ql), row), pl.BlockSpec((tm, kl), row),
                   pl.BlockSpec((tm, rope), row), pl.BlockSpec((tm, LANE), row)],
        out_shape=[jax.ShapeDtypeStruct((T, D), BF16), jax.ShapeDtypeStruct((T, ql), BF16),
                   jax.ShapeDtypeStruct((T, kl), F32), jax.ShapeDtypeStruct((T, rope), F32),
                   jax.ShapeDtypeStruct((T, LANE), BF16)],
        compiler_params=_cparams("parallel"),
        name="small_proj",
    )(x2, g, w_small, qg, kvg, cos_t, sin_t)


def _big_proj_kernel(u_ref, *refs, nseg, nside):
    ws, sides = refs[:nseg], refs[nseg:nseg + nside]
    outs, side_outs = refs[nseg + nside:2 * nseg + nside], refs[2 * nseg + nside:]
    u = u_ref[...]
    for w_ref, o_ref in zip(ws, outs):
        o_ref[...] = _dot(u, w_ref[...]).astype(o_ref.dtype)
    for s_ref, o_ref in zip(sides, side_outs):
        o_ref[...] = s_ref[...].astype(o_ref.dtype)


def _big_proj(u, ws, out_dtypes, side, *, tm, tn):
    T, D = u.shape
    n = ws[0].shape[1]
    nseg = len(ws)
    nj = n // tn
    steps = (T // tm) * nj
    side2 = [a.reshape(-1, a.shape[-1]) for a in side]
    side_blocks = [(a.shape[0] // steps, a.shape[1]) for a in side2]
    assert all(a.shape[0] % steps == 0 and b[0] % 16 == 0 for a, b in zip(side2, side_blocks))
    slab = lambda i, j: (i * nj + j, 0)
    res = pl.pallas_call(
        functools.partial(_big_proj_kernel, nseg=nseg, nside=len(side)),
        grid=(T // tm, nj),
        in_specs=[pl.BlockSpec((tm, D), lambda i, j: (i, 0))]
                 + [pl.BlockSpec((D, tn), lambda i, j: (0, j)) for _ in ws]
                 + [pl.BlockSpec(b, slab) for b in side_blocks],
        out_specs=[pl.BlockSpec((tm, tn), lambda i, j: (i, j)) for _ in ws]
                  + [pl.BlockSpec(b, slab) for b in side_blocks],
        out_shape=[jax.ShapeDtypeStruct((T, n), dt) for dt in out_dtypes]
                  + [jax.ShapeDtypeStruct(a.shape, BF16) for a in side2],
        compiler_params=_cparams("arbitrary", "arbitrary"),
        name="big_proj",
    )(u, *ws, *side2)
    return res[:nseg], [o.reshape(a.shape) for o, a in zip(res[nseg:], side)]


def _q_proj_kernel(c_ref, wr_ref, ws_ref, cos_ref, sin_ref, q_ref, *, heads):
    c = c_ref[...]
    raw = _dot(c, wr_ref[...])
    sw = _dot(c, ws_ref[...])
    cos = cos_ref[...]
    sin = sin_ref[...]
    for h in range(heads):
        b = 2 * LANE * h
        q_ref[:, b:b + LANE] = raw[:, b:b + LANE].astype(BF16)
        r = raw[:, b + LANE:b + 2 * LANE] * cos + sw[:, LANE * h:LANE * (h + 1)] * sin
        q_ref[:, b + LANE:b + 2 * LANE] = r.astype(BF16)


def _q_proj(cqn, wq_raw, wq_sw, cos_t, sin_t, *, tm, heads):
    T, ql = cqn.shape
    nblk = cos_t.shape[0] // tm
    row = lambda i: (i, 0)
    fix = lambda i: (0, 0)
    tab = lambda i: (i % nblk, 0)
    n = wq_raw.shape[1]
    return pl.pallas_call(
        functools.partial(_q_proj_kernel, heads=heads),
        grid=(T // tm,),
        in_specs=[pl.BlockSpec((tm, ql), row), pl.BlockSpec(wq_raw.shape, fix), pl.BlockSpec(wq_sw.shape, fix),
                  pl.BlockSpec((tm, LANE), tab), pl.BlockSpec((tm, LANE), tab)],
        out_specs=pl.BlockSpec((tm, n), row),
        out_shape=jax.ShapeDtypeStruct((T, n), BF16),
        compiler_params=_cparams("parallel"),
        name="q_proj",
    )(cqn, wq_raw, wq_sw, cos_t, sin_t)


def _kv_proj_kernel(lat_ref, krb_ref, wk_ref, wv_ref, k_ref, v_ref, *, heads):
    lat = lat_ref[...].astype(BF16)
    kn = _dot(lat, wk_ref[...]).astype(BF16)
    v_ref[...] = _dot(lat, wv_ref[...]).astype(BF16)
    krb = krb_ref[...]
    for h in range(heads):
        b = 2 * LANE * h
        k_ref[:, b:b + LANE] = kn[:, LANE * h:LANE * (h + 1)]
        k_ref[:, b + LANE:b + 2 * LANE] = krb


def _kv_proj(lat, krb, wk, wv, *, tm, heads):
    T, kl = lat.shape
    row = lambda i: (i, 0)
    fix = lambda i: (0, 0)
    return pl.pallas_call(
        functools.partial(_kv_proj_kernel, heads=heads),
        grid=(T // tm,),
        in_specs=[pl.BlockSpec((tm, kl), row), pl.BlockSpec((tm, LANE), row),
                  pl.BlockSpec(wk.shape, fix), pl.BlockSpec(wv.shape, fix)],
        out_specs=[pl.BlockSpec((tm, 2 * LANE * heads), row), pl.BlockSpec((tm, wv.shape[1]), row)],
        out_shape=[jax.ShapeDtypeStruct((T, 2 * LANE * heads), BF16), jax.ShapeDtypeStruct((T, wv.shape[1]), BF16)],
        compiler_params=_cparams("parallel"),
        name="kv_proj",
    )(lat, krb, wk, wv)


def _attn_body(q_ref, k_ref, v_ref, o_ref, *, tq, tk, hpb):
    S = q_ref.shape[1]
    kd = 2 * LANE
    vd = v_ref.shape[2] // hpb

    def update(carry, s, v):
        m, l, acc = carry
        m_new = jnp.maximum(m, jnp.max(s, axis=-1, keepdims=True))
        alpha = jnp.exp(m - m_new)
        p = jnp.exp(s - m_new)
        l = alpha * l + jnp.sum(p, axis=-1, keepdims=True)
        acc = alpha * acc + _dot(p.astype(BF16), v)
        return m_new, l, acc

    for r0 in range(0, S, tq):
        seen = r0 + tq
        blocks = [(k0, min(tk, seen - k0)) for k0 in range(0, seen, tk)]
        for g in range(hpb):
            q = q_ref[0, r0:r0 + tq, g * kd:(g + 1) * kd]
            carry = (jnp.full((tq, 1), NEG_INF, F32), jnp.zeros((tq, 1), F32), jnp.zeros((tq, vd), F32))
            for k0, kl in blocks:
                k = k_ref[0, k0:k0 + kl, g * kd:(g + 1) * kd]
                v = v_ref[0, k0:k0 + kl, g * vd:(g + 1) * vd]
                s = _dot_nt(q, k)
                if k0 + kl > r0:
                    qc = (lax.broadcasted_iota(jnp.int32, (tq, kl), 0) + r0) // CHUNK
                    kc = (lax.broadcasted_iota(jnp.int32, (tq, kl), 1) + k0) // CHUNK
                    s = jnp.where(kc <= qc, s, NEG_INF)
                carry = update(carry, s, v)
            m, l, acc = carry
            o_ref[0, r0:r0 + tq, g * vd:(g + 1) * vd] = (acc / l).astype(BF16)


def _attn_kernel(q_ref, k_ref, v_ref, o_ref, *, tq, tk, hpb):
    _attn_body(q_ref, k_ref, v_ref, o_ref, tq=tq, tk=tk, hpb=hpb)


def _attention(qf, kf, v, *, heads, tq, tk, hpb):
    B, S, _ = qf.shape
    vd = v.shape[2] // heads
    seq = lambda b, h: (b, 0, h)
    return pl.pallas_call(
        functools.partial(_attn_kernel, tq=tq, tk=tk, hpb=hpb),
        grid=(B, heads // hpb),
        in_specs=[pl.BlockSpec((1, S, 2 * LANE * hpb), seq), pl.BlockSpec((1, S, 2 * LANE * hpb), seq),
                  pl.BlockSpec((1, S, vd * hpb), seq)],
        out_specs=pl.BlockSpec((1, S, vd * hpb), seq),
        out_shape=jax.ShapeDtypeStruct((B, S, heads * vd), BF16),
        compiler_params=_cparams("parallel", "parallel"),
        name="mla_prompt_attn",
    )(qf, kf, v)


def _attn_sample_kernel(q_ref, cl_ref, ck_ref, ln_ref, kn_ref, wuk_ref, wuv_ref, o_ref,
                        ql_ref, qr_ref, ol_ref, *, heads, rope):
    sd = q_ref.shape[1]
    for h in range(heads):
        b = 2 * LANE * h
        ql_ref[h * sd:(h + 1) * sd, :] = _dot(q_ref[0, :, b:b + LANE], wuk_ref[h]).astype(BF16)
        qr_ref[h * sd:(h + 1) * sd, :] = q_ref[0, :, b + LANE:b + 2 * LANE]
    ql = ql_ref[...]
    qr = qr_ref[...][:, :rope]
    cl = cl_ref[0].astype(BF16)
    ck = ck_ref[0].astype(BF16)
    ln = ln_ref[0].astype(BF16)
    kn = kn_ref[0].astype(BF16)
    s_past = _dot_nt(ql, cl) + _dot_nt(qr, ck)
    s_new = _dot_nt(ql, ln) + _dot_nt(qr, kn)
    m = jnp.maximum(jnp.max(s_past, axis=-1, keepdims=True), jnp.max(s_new, axis=-1, keepdims=True))
    p_past = jnp.exp(s_past - m)
    p_new = jnp.exp(s_new - m)
    l = jnp.sum(p_past, axis=-1, keepdims=True) + jnp.sum(p_new, axis=-1, keepdims=True)
    o_lat = (_dot(p_past.astype(BF16), cl) + _dot(p_new.astype(BF16), ln)) / l
    ol_ref[...] = o_lat.astype(BF16)
    vd = wuv_ref.shape[2]
    for h in range(heads):
        o_ref[0, :, h * vd:(h + 1) * vd] = _dot(ol_ref[h * sd:(h + 1) * sd, :], wuv_ref[h]).astype(BF16)


def _attention_sample(qf, cache_lat, cache_kr, lat_new, kr_new, wuk_t, wuv, *, heads):
    B, sd, _ = qf.shape
    P, kl = cache_lat.shape[1:]
    rope = cache_kr.shape[2]
    vd = wuv.shape[2]
    b3 = lambda b: (b, 0, 0)
    fix = lambda b: (0, 0, 0)
    return pl.pallas_call(
        functools.partial(_attn_sample_kernel, heads=heads, rope=rope),
        grid=(B,),
        in_specs=[pl.BlockSpec((1, sd, qf.shape[2]), b3), pl.BlockSpec((1, P, kl), b3), pl.BlockSpec((1, P, rope), b3),
                  pl.BlockSpec((1, sd, kl), b3), pl.BlockSpec((1, sd, rope), b3),
                  pl.BlockSpec(wuk_t.shape, fix), pl.BlockSpec(wuv.shape, fix)],
        out_specs=pl.BlockSpec((1, sd, heads * vd), b3),
        out_shape=jax.ShapeDtypeStruct((B, sd, heads * vd), BF16),
        scratch_shapes=[pltpu.VMEM((heads * sd, kl), BF16), pltpu.VMEM((heads * sd, LANE), BF16),
                        pltpu.VMEM((heads * sd, kl), BF16)],
        compiler_params=_cparams("parallel"),
        name="mla_sample_attn",
    )(qf, cache_lat, cache_kr, lat_new, kr_new, wuk_t, wuv)


def _hgrn_chunk(hq_ref, hf_ref, hi_ref, hg_ref, lb_ref, g_ref, o_ref, st_ref, rows, *, heads):
    L = rows.stop - rows.start
    dk = st_ref.shape[2]
    row = lax.broadcasted_iota(jnp.int32, (L, L), 0)
    col = lax.broadcasted_iota(jnp.int32, (L, L), 1)
    causal = col <= row
    tri = jnp.where(causal, 1.0, 0.0).astype(BF16)
    lb = lb_ref[...]
    f = lb + (1.0 - lb) * _sigmoid(hf_ref[0, rows, :])
    kk = 1.0 - f
    hi, mid, lo = _split3(jnp.log(f))
    cum = _dot(tri, hi) + _dot(tri, mid) + _dot(tri, lo)
    tot = cum[L - 1:L, :]
    q_dec = (hq_ref[0, rows, :].astype(F32) * jnp.exp(cum)).astype(BF16)
    k_dec = (kk * jnp.exp(-cum)).astype(BF16)
    k2 = (kk * jnp.exp(tot - cum)).astype(BF16)
    decay = jnp.exp(tot)
    vb = hi_ref[0, rows, :]
    v32 = vb.astype(F32)
    sls = [slice(h * dk, (h + 1) * dk) for h in range(heads)]
    atts = [jnp.where(causal, _dot_nt(q_dec[:, sl], k_dec[:, sl]), 0.0).astype(BF16) for sl in sls]
    sts = [st_ref[h] for h in range(heads)]
    outs = [_dot_nt(q_dec[:, sl], sts[h].astype(BF16)) + _dot(atts[h], vb[:, sl]) for h, sl in enumerate(sls)]
    for h, sl in enumerate(sls):
        st_ref[h] = decay[:, sl] * sts[h] + _dot(v32[:, sl].T.astype(BF16), k2[:, sl])
    g = g_ref[...]
    for h, sl in enumerate(sls):
        o = outs[h]
        on = o * lax.rsqrt(jnp.mean(o * o, axis=-1, keepdims=True) + NORM_EPS) * g
        hg = hg_ref[0, rows, sl].astype(F32)
        o_ref[0, rows, sl] = (on * (hg * _sigmoid(hg))).astype(BF16)


def _hgrn_kernel(*refs, heads, has_state, chunk):
    if has_state:
        hq_ref, hf_ref, hi_ref, hg_ref, lb_ref, g_ref, s0_ref, o_ref, sn_ref, st_ref = refs
    else:
        hq_ref, hf_ref, hi_ref, hg_ref, lb_ref, g_ref, o_ref, sn_ref, st_ref = refs
    c = pl.program_id(1)

    @pl.when(c == 0)
    def _():
        if has_state:
            for h in range(heads):
                st_ref[h] = s0_ref[0, h].T
        else:
            st_ref[...] = jnp.zeros_like(st_ref)

    for c0 in range(0, hq_ref.shape[1], chunk):
        _hgrn_chunk(hq_ref, hf_ref, hi_ref, hg_ref, lb_ref, g_ref, o_ref, st_ref, slice(c0, c0 + chunk), heads=heads)

    @pl.when(c == pl.num_programs(1) - 1)
    def _():
        for h in range(heads):
            sn_ref[0, h] = st_ref[h].T


def _hgrn(hq, hf, hi, hg, lb, g, s0, *, heads, L, chunk):
    B, S, F = hq.shape
    dk = F // heads
    blk = lambda b, c: (b, c, 0)
    fix2 = lambda b, c: (0, 0)
    st4 = lambda b, c: (b, 0, 0, 0)
    in_specs = [pl.BlockSpec((1, L, F), blk)] * 4 + [pl.BlockSpec((1, F), fix2), pl.BlockSpec((1, dk), fix2)]
    args = [hq, hf, hi, hg, lb, g]
    if s0 is not None:
        in_specs.append(pl.BlockSpec((1, heads, dk, dk), st4))
        args.append(s0)
    return pl.pallas_call(
        functools.partial(_hgrn_kernel, heads=heads, has_state=s0 is not None, chunk=chunk),
        grid=(B, S // L),
        in_specs=in_specs,
        out_specs=[pl.BlockSpec((1, L, F), blk), pl.BlockSpec((1, heads, dk, dk), st4)],
        out_shape=[jax.ShapeDtypeStruct((B, S, F), BF16), jax.ShapeDtypeStruct((B, heads, dk, dk), F32)],
        scratch_shapes=[pltpu.VMEM((heads, dk, dk), F32)],
        compiler_params=_cparams("parallel", "arbitrary"),
        name="hgrn",
    )(*args)


def _out_proj_kernel(oa_ref, ob_ref, ga_ref, gb_ref, x_ref, w_ref, o_ref, *, kc):
    acc = x_ref[...]
    for k in range(w_ref.shape[0] // kc):
        sl = slice(k * kc, (k + 1) * kc)
        ga = _sigmoid(ga_ref[:, sl].astype(F32))
        gb = _sigmoid(gb_ref[:, sl].astype(F32))
        m = (ga * oa_ref[:, sl].astype(F32) + gb * ob_ref[:, sl].astype(F32)).astype(BF16)
        acc = acc + _dot(m, w_ref[sl, :])
    o_ref[...] = acc


def _out_proj(oa, ob, ga, gb, x2, w_o, *, tm):
    T, D = x2.shape
    row = lambda i: (i, 0)
    return pl.pallas_call(
        functools.partial(_out_proj_kernel, kc=512),
        grid=(T // tm,),
        in_specs=[pl.BlockSpec((tm, D), row)] * 5 + [pl.BlockSpec((D, D), lambda i: (0, 0))],
        out_specs=pl.BlockSpec((tm, D), row),
        out_shape=jax.ShapeDtypeStruct((T, D), F32),
        compiler_params=_cparams("parallel"),
        name="out_proj",
    )(oa, ob, ga, gb, x2, w_o)


def _split3(x):
    hi = x.astype(BF16)
    r = x - hi.astype(F32)
    mid = r.astype(BF16)
    lo = (r - mid.astype(F32)).astype(BF16)
    return hi, mid, lo


def _router_kernel(hp_ref, hs_ref, g_ref, w3_ref, b_ref, u_ref, idx_ref, gate_ref, cnt_ref, run_ref, *, n_exp, n_p):
    step = pl.program_id(0)

    @pl.when(step == 0)
    def _():
        run_ref[...] = jnp.zeros_like(run_ref)

    h = jnp.where(step < n_p, hp_ref[...], hs_ref[...])
    u = _rms(h, g_ref[...])
    u_ref[...] = _pack_bf16_pairs(u)
    u_hi, u_mid, u_lo = _split3(u)
    e1, e2 = n_exp, 2 * n_exp
    p_hi = _dot(u_hi, w3_ref[...])
    p_mid = _dot(u_mid, w3_ref[:, :e2])
    p_lo = _dot(u_lo, w3_ref[:, :e1])
    logits = (p_hi[:, :e1] + (p_hi[:, e1:e2] + p_mid[:, :e1]) + (p_hi[:, e2:] + p_mid[:, e1:] + p_lo)) + b_ref[...]
    tm = logits.shape[0]
    lane_e = lax.broadcasted_iota(jnp.int32, (tm, n_exp), 1)
    lane_o = lax.broadcasted_iota(jnp.int32, (tm, LANE), 1)
    idx_out = jnp.zeros((tm, LANE), jnp.int32)
    val_out = jnp.zeros((tm, LANE), F32)
    vals, picks = [], []
    for k in range(TOP_K):
        m = jnp.max(logits, axis=-1, keepdims=True)
        i = jnp.min(jnp.where(logits == m, lane_e, n_exp), axis=-1, keepdims=True)
        pick = lane_e == i
        logits = jnp.where(pick, -jnp.inf, logits)
        vals.append(m)
        picks.append(pick)
        idx_out = jnp.where(lane_o == k, i, idx_out)
    es = [jnp.exp(v - vals[0]) for v in vals]
    den = es[0]
    for e in es[1:]:
        den = den + e
    for k in range(TOP_K):
        val_out = jnp.where(lane_o == k, es[k] / den, val_out)
    multi = jnp.zeros((tm, n_exp), F32)
    for pick in picks:
        multi = multi + jnp.where(pick, 1.0, 0.0)
    earlier = lax.broadcasted_iota(jnp.int32, (tm, tm), 1) < lax.broadcasted_iota(jnp.int32, (tm, tm), 0)
    before = _dot(jnp.where(earlier, 1.0, 0.0).astype(BF16), multi.astype(BF16)) + run_ref[...]
    for k in range(TOP_K):
        rank = jnp.sum(jnp.where(picks[k], before, 0.0), axis=-1, keepdims=True).astype(jnp.int32)
        idx_out = jnp.where(lane_o == TOP_K + k, rank, idx_out)
    run_ref[...] += jnp.sum(multi, axis=0, keepdims=True)
    cnt_ref[...] = run_ref[...]
    idx_ref[...] = idx_out
    gate_ref[...] = val_out


def _two_group_specs(shape, n_p):
    return [pl.BlockSpec(shape, lambda i: (jnp.minimum(i, n_p - 1), 0)),
            pl.BlockSpec(shape, lambda i: (jnp.maximum(i - n_p, 0), 0))]


def _router(h2p, h2s, g, w_r, b_r, *, tm):
    (Tp, D), Ts = h2p.shape, h2s.shape[0]
    T = Tp + Ts
    n_exp = w_r.shape[1]
    w3 = jnp.concatenate(_split3(w_r), axis=1)
    row = lambda i: (i, 0)
    fix = lambda i: (0, 0)
    return pl.pallas_call(
        functools.partial(_router_kernel, n_exp=n_exp, n_p=Tp // tm),
        grid=(T // tm,),
        in_specs=_two_group_specs((tm, D), Tp // tm)
                 + [pl.BlockSpec((1, D), fix), pl.BlockSpec((D, 3 * n_exp), fix), pl.BlockSpec((1, n_exp), fix)],
        out_specs=[pl.BlockSpec((tm, D // 2), row), pl.BlockSpec((tm, LANE), row), pl.BlockSpec((tm, LANE), row),
                   pl.BlockSpec((1, n_exp), fix)],
        out_shape=[jax.ShapeDtypeStruct((T, D // 2), jnp.uint32), jax.ShapeDtypeStruct((T, LANE), jnp.int32),
                   jax.ShapeDtypeStruct((T, LANE), F32), jax.ShapeDtypeStruct((1, n_exp), F32)],
        scratch_shapes=[pltpu.VMEM((1, n_exp), F32)],
        compiler_params=_cparams("arbitrary"),
        name="router",
    )(h2p, h2s, g, w3, b_r)


def _row_copy(src_ref, dst_ref, src_row, dst_row, sem):
    return pltpu.make_async_copy(src_ref.at[pl.ds(src_row, 1)], dst_ref.at[pl.ds(dst_row, 1)], sem)


def _dispatch_kernel(zs_ref, ze_ref, dest_hbm, u_ref, xs_hbm, idx_smem, zrow_ref, sem_idx, sem, sem_z, *, R, n_exp):
    i = pl.program_id(0)
    n = pl.num_programs(0)
    slot = i % 2
    N = R * TOP_K

    def idx_copy(step, s):
        return pltpu.make_async_copy(dest_hbm.at[step], idx_smem.at[pl.ds(pl.multiple_of(s * N, N), N)],
                                     sem_idx.at[s])

    def start_rows(s):
        def body(r, c):
            for k in range(TOP_K):
                _row_copy(u_ref, xs_hbm, r, idx_smem[s * N + r * TOP_K + k], sem).start(priority=k % 2)
            return c
        lax.fori_loop(0, R, body, 0, unroll=8)

    def wait_rows():
        def body(r, c):
            for k in range(TOP_K):
                _row_copy(u_ref, xs_hbm, 0, 0, sem).wait()
            return c
        lax.fori_loop(0, R, body, 0, unroll=8)

    def zero_rows(start):
        def per_expert(e, c):
            def body(r, c2):
                cp = _row_copy(zrow_ref, xs_hbm, 0, r, sem_z)
                cp.start() if start else cp.wait()
                return c2
            return lax.fori_loop(zs_ref[e], ze_ref[e], body, c)
        lax.fori_loop(0, n_exp, per_expert, 0)

    @pl.when(i == 0)
    def _():
        zrow_ref[...] = jnp.zeros_like(zrow_ref)
        idx_copy(0, 0).start()

    idx_copy(i, slot).wait()

    @pl.when(i + 1 < n)
    def _():
        idx_copy(i + 1, 1 - slot).start()

    start_rows(slot)

    @pl.when(i == n - 1)
    def _():
        zero_rows(True)

    wait_rows()

    @pl.when(i == n - 1)
    def _():
        zero_rows(False)


def _dispatch(zs, ze, dest2, u, *, P, R):
    n = dest2.shape[0]
    D = u.shape[1]
    grid_spec = pltpu.PrefetchScalarGridSpec(
        num_scalar_prefetch=2,
        grid=(n,),
        in_specs=[pl.BlockSpec(memory_space=pl.ANY), pl.BlockSpec((R, D), lambda i, zs, ze: (i, 0))],
        out_specs=pl.BlockSpec(memory_space=pl.ANY),
        scratch_shapes=[pltpu.SMEM((2 * R * TOP_K,), jnp.int32), pltpu.VMEM((8, D), u.dtype),
                        pltpu.SemaphoreType.DMA((2,)), pltpu.SemaphoreType.DMA, pltpu.SemaphoreType.DMA],
    )
    return pl.pallas_call(
        functools.partial(_dispatch_kernel, R=R, n_exp=zs.shape[0]),
        grid_spec=grid_spec,
        out_shape=jax.ShapeDtypeStruct((P, D), u.dtype),
        compiler_params=_cparams("arbitrary"),
        name="moe_dispatch",
    )(zs, ze, dest2, u)


def _expert_kernel(te_ref, nv_ref, x_ref, wg_ref, wu_ref, bg_ref, bu_ref, wd_ref, bd_ref, y_ref, xb_ref):
    i = pl.program_id(0)
    c = pl.program_id(1)
    half = x_ref.shape[1]

    live = nv_ref[i] > 0

    def chunk(first):
        if first:
            hi, lo = _unpack_bf16_pairs(x_ref[...])
            xb_ref[:, :half] = hi
            xb_ref[:, half:] = lo
            xb = jnp.concatenate([hi, lo], axis=1)
        else:
            xb = xb_ref[...]
        gate = jnp.minimum(_dot(xb, wg_ref[0]) + bg_ref[0], SWIGLU_LIMIT)
        up = jnp.clip(_dot(xb, wu_ref[0]) + bu_ref[0], -SWIGLU_LIMIT, SWIGLU_LIMIT)
        act = gate * _sigmoid(SWIGLU_ALPHA * gate) * (up + 1.0)
        down = _dot(act.astype(BF16), wd_ref[0])
        if first:
            y_ref[...] = down + bd_ref[0]
        else:
            y_ref[...] += down

    @pl.when(jnp.logical_and(live, c == 0))
    def _():
        chunk(True)

    @pl.when(jnp.logical_and(live, c > 0))
    def _():
        chunk(False)

    @pl.when(jnp.logical_and(jnp.logical_not(live), c == 0))
    def _():
        y_ref[...] = jnp.broadcast_to(bd_ref[0], y_ref.shape)


def _experts(tile_e, tile_nv, xs, w_gu, b_gu, w_dn, b_dn, *, tm, fc):
    P, half = xs.shape
    D = 2 * half
    E, _, ff2 = w_gu.shape
    ff = ff2 // 2
    nc = ff // fc
    n_tiles = P // tm

    def cc(i, c, nv):
        return jnp.where(nv[i] > 0, c, nc - 1)

    grid_spec = pltpu.PrefetchScalarGridSpec(
        num_scalar_prefetch=2,
        grid=(n_tiles, nc),
        in_specs=[
            pl.BlockSpec((tm, half), lambda i, c, te, nv: (jnp.where(nv[i] > 0, i, 0), 0)),
            pl.BlockSpec((1, D, fc), lambda i, c, te, nv: (te[i], 0, cc(i, c, nv))),
            pl.BlockSpec((1, D, fc), lambda i, c, te, nv: (te[i], 0, nc + cc(i, c, nv))),
            pl.BlockSpec((1, 1, fc), lambda i, c, te, nv: (te[i], 0, cc(i, c, nv))),
            pl.BlockSpec((1, 1, fc), lambda i, c, te, nv: (te[i], 0, nc + cc(i, c, nv))),
            pl.BlockSpec((1, fc, D), lambda i, c, te, nv: (te[i], cc(i, c, nv), 0)),
            pl.BlockSpec((1, 1, D), lambda i, c, te, nv: (te[i], 0, 0)),
        ],
        out_specs=pl.BlockSpec((tm, D), lambda i, c, te, nv: (i, 0)),
        scratch_shapes=[pltpu.VMEM((tm, D), BF16)],
    )
    return pl.pallas_call(
        _expert_kernel,
        grid_spec=grid_spec,
        out_shape=jax.ShapeDtypeStruct((P, D), F32),
        compiler_params=_cparams("arbitrary", "arbitrary"),
        name="moe_experts",
    )(tile_e, tile_nv, xs, w_gu, w_gu, b_gu, b_gu, w_dn, b_dn)


def _combine_kernel(dest_hbm, gate_ref, hp_ref, hs_ref, ys_hbm, g_ref, yp_ref, ysm_ref, buf_ref, idx_smem, sem_idx,
                    sem, *, n_p):
    i = pl.program_id(0)
    n = pl.num_programs(0)
    R = hp_ref.shape[0]
    slot = i % 2
    N = R * TOP_K

    def idx_copy(step, s):
        return pltpu.make_async_copy(dest_hbm.at[step], idx_smem.at[pl.ds(pl.multiple_of(s * N, N), N)],
                                     sem_idx.at[s])

    def start_rows(s):
        def body(r, c):
            for k in range(TOP_K):
                _row_copy(ys_hbm, buf_ref.at[s, k], idx_smem[s * N + r * TOP_K + k], r,
                          sem.at[s]).start(priority=k % 2)
            return c
        lax.fori_loop(0, R, body, 0, unroll=8)

    def wait_rows(s):
        def body(r, c):
            for k in range(TOP_K):
                _row_copy(ys_hbm, buf_ref.at[s, k], 0, r, sem.at[s]).wait()
            return c
        lax.fori_loop(0, R, body, 0, unroll=8)

    @pl.when(i == 0)
    def _():
        cp = idx_copy(0, 0)
        cp.start()
        cp.wait()
        start_rows(0)

        @pl.when(n > 1)
        def _():
            idx_copy(1, 1).start()

    more = i + 1 < n

    @pl.when(more)
    def _():
        idx_copy(i + 1, 1 - slot).wait()

    wait_rows(slot)
    g = g_ref[...]
    nxt = 1 - slot
    sub = 64

    def run(h_ref, y_ref, issue):
        def body(j, c):
            r0 = pl.multiple_of(j * sub, sub)
            rows = pl.ds(r0, sub)
            h = h_ref[rows, :]
            gate = gate_ref[rows, :]
            for k in range(TOP_K):
                h = h + gate[:, k:k + 1] * buf_ref[slot, k, rows, :]
            y_ref[rows, :] = _rms(h, g)
            if issue:
                for r in range(sub):
                    for k in range(TOP_K):
                        _row_copy(ys_hbm, buf_ref.at[nxt, k], idx_smem[nxt * N + (r0 + r) * TOP_K + k], r0 + r,
                                  sem.at[nxt]).start(priority=k % 2)
            return c
        lax.fori_loop(0, R // sub, body, 0)

    is_p = i < n_p
    for prompt in (True, False):
        for issue in (True, False):
            cond = jnp.logical_and(is_p if prompt else jnp.logical_not(is_p),
                                   more if issue else jnp.logical_not(more))
            pl.when(cond)(functools.partial(run, hp_ref if prompt else hs_ref, yp_ref if prompt else ysm_ref, issue))

    @pl.when(i + 2 < n)
    def _():
        idx_copy(i + 2, slot).start()


def _combine(dest2, gate, h2p, h2s, ys, g, *, R):
    (Tp, D), Ts = h2p.shape, h2s.shape[0]
    n_p = Tp // R
    row = lambda i: (i, 0)
    return pl.pallas_call(
        functools.partial(_combine_kernel, n_p=n_p),
        grid=((Tp + Ts) // R,),
        in_specs=[pl.BlockSpec(memory_space=pl.ANY), pl.BlockSpec((R, LANE), row)] + _two_group_specs((R, D), n_p)
                 + [pl.BlockSpec(memory_space=pl.ANY), pl.BlockSpec((1, D), lambda i: (0, 0))],
        out_specs=_two_group_specs((R, D), n_p),
        out_shape=[jax.ShapeDtypeStruct((Tp, D), F32), jax.ShapeDtypeStruct((Ts, D), F32)],
        scratch_shapes=[pltpu.VMEM((2, TOP_K, R, D), F32), pltpu.SMEM((2 * R * TOP_K,), jnp.int32),
                        pltpu.SemaphoreType.DMA((2,)), pltpu.SemaphoreType.DMA((2,))],
        compiler_params=_cparams("arbitrary"),
        name="moe_combine",
    )(dest2, gate, h2p, h2s, ys, g)


def _rope_tables(pos, rope, rows):
    half = rope // 2
    inv = ROPE_BASE ** (-jnp.arange(half, dtype=F32) / half)
    ang = pos.astype(F32)[:, None] * inv[None, :]
    cos, sin = jnp.cos(ang), jnp.sin(ang)
    z = jnp.zeros((pos.shape[0], LANE - rope), F32)
    cos_t = jnp.concatenate([cos, cos, z], axis=1)
    sin_t = jnp.concatenate([-sin, sin, z], axis=1)
    rep = max(1, rows // pos.shape[0])
    return jnp.tile(cos_t, (rep, 1)), jnp.tile(sin_t, (rep, 1))


def _swap_halves(w):
    half = w.shape[-1] // 2
    return jnp.concatenate([w[..., half:], w[..., :half]], axis=-1)


def _mixer(x, pos, past, wts, side, *, tm):
    B, S, D = x.shape
    T = B * S
    heads, ql, kl, rope = wts["heads"], wts["ql"], wts["kl"], wts["rope"]
    x2 = x.reshape(T, D)
    cos_t, sin_t = _rope_tables(pos, rope, tm)
    u, cqn, lat, kr, krb = _small_proj(x2, wts["norm_mix_g"], wts["w_small"], wts["q_norm_g"], wts["kv_norm_g"],
                                       cos_t, sin_t, tm=tm, ql=ql, kl=kl, rope=rope)
    tm_big = min(1024, T)
    (hq, hf, hi, hg, ga, gb), side_out = _big_proj(u, wts["w_big"], (BF16, F32, BF16, BF16, BF16, BF16), side,
                                                   tm=tm_big, tn=256)
    qf = _q_proj(cqn, wts["wq_raw"], wts["wq_sw"], cos_t, sin_t, tm=tm, heads=heads)
    F = hq.shape[1]
    r3 = lambda a: a.reshape(B, S, a.shape[1])
    if past is None:
        kf, v = _kv_proj(lat, krb, wts["w_uk"], wts["w_uv"], tm=tm, heads=heads)
        o_a = _attention(r3(qf), r3(kf), r3(v), heads=heads, tq=min(512, S), tk=min(1024, S), hpb=2)
        o_b, st = _hgrn(r3(hq), r3(hf), r3(hi), r3(hg), wts["lb"], wts["hgrn_g"], None, heads=wts["hg_heads"],
                        L=min(8 * CHUNK, S), chunk=CHUNK)
    else:
        cache_lat, cache_kr, state = past
        o_a = _attention_sample(r3(qf), cache_lat, cache_kr, r3(lat), r3(kr), wts["w_uk_t"], wts["w_uv_h"],
                                heads=heads)
        o_b, st = _hgrn(r3(hq), r3(hf), r3(hi), r3(hg), wts["lb"], wts["hgrn_g"], state, heads=wts["hg_heads"], L=S,
                        chunk=S)
    h2 = _out_proj(o_a.reshape(T, D), o_b.reshape(T, F), ga, gb, x2, wts["w_o"], tm=256)
    return h2, lat.reshape(B, S, kl), kr.reshape(B, S, rope), st, side_out


def kernel(x_prompt, x_sample, cache_mla_latent, cache_mla_krope, state_hgrn, norm_mix_g, w_in, q_norm_g, w_uq,
           kv_norm_g, w_uk, w_uv, hgrn_lb_raw, hgrn_norm_g, w_o, norm_ffn_g, w_router, b_router, w_gate_up,
           b_gate_up, w_down, b_down, norm_final_g):
    depth = w_in.shape[0]
    assert depth == 1, "single-layer kernel"
    B, S, D = x_prompt.shape
    Bd, Sd, _ = x_sample.shape
    past_len = cache_mla_latent.shape[2]
    ql, heads, qk = w_uq.shape[1:]
    kl, _, nope = w_uk.shape[1:]
    vd = w_uv.shape[3]
    rope = qk - nope
    hg_heads, dk, dv = state_hgrn.shape[2:]
    n_exp, _, ff2 = w_gate_up.shape[1:]
    assert nope == LANE and vd == LANE and dk == LANE and dv == LANE and rope <= LANE
    hf_dim = hg_heads * dk
    scale = 1.0 / math.sqrt(nope + rope)
    l = 0

    wi = w_in[l]
    o_cq, o_kv, o_pe, o_hq = 0, ql, ql + kl, ql + kl + rope
    k_pe = wi[:, o_pe:o_pe + rope]
    zpad = jnp.zeros((D, LANE - rope), F32)
    w_small = jnp.concatenate([wi[:, o_cq:o_pe], k_pe, zpad, _swap_halves(k_pe), zpad], axis=1).astype(BF16)
    segs = [hf_dim, hf_dim, D, D, D, D]
    offs = [o_hq]
    for s_ in segs[:-1]:
        offs.append(offs[-1] + s_)
    w_big = [wi[:, o:o + s_].astype(BF16) for o, s_ in zip(offs, segs)]
    wq = w_uq[l] * scale
    zq = jnp.zeros((ql, heads, LANE - rope), F32)
    wq_raw = jnp.concatenate([wq[..., :nope], wq[..., nope:], zq], axis=-1).reshape(ql, heads * 2 * LANE).astype(BF16)
    wq_sw = jnp.concatenate([_swap_halves(wq[..., nope:]), zq], axis=-1).reshape(ql, heads * LANE).astype(BF16)
    lb_all = jnp.cumsum(jax.nn.softmax(hgrn_lb_raw.astype(F32), axis=0), axis=0)
    wts = dict(
        heads=heads, ql=ql, kl=kl, rope=rope, hg_heads=hg_heads,
        norm_mix_g=norm_mix_g[l][None], w_small=w_small, q_norm_g=q_norm_g[l][None], kv_norm_g=kv_norm_g[l][None],
        w_big=w_big, wq_raw=wq_raw, wq_sw=wq_sw,
        w_uk=w_uk[l].reshape(kl, heads * nope).astype(BF16), w_uv=w_uv[l].reshape(kl, heads * vd).astype(BF16),
        w_uk_t=jnp.transpose(w_uk[l], (1, 2, 0)).astype(BF16), w_uv_h=jnp.transpose(w_uv[l], (1, 0, 2)).astype(BF16),
        lb=lb_all[l][None], hgrn_g=hgrn_norm_g[l][None], w_o=w_o[l].astype(BF16),
    )

    Tp, Ts = B * S, Bd * Sd
    tm = 512
    assert S % tm == 0 or tm % S == 0
    assert Tp % tm == 0 and Ts % tm == 0 and tm % Sd == 0
    h2p, lat_p, kr_p, st_p, (w_gu_b, w_dn_b) = _mixer(x_prompt, jnp.arange(S), None, wts,
                                                       [w_gate_up[l], w_down[l]], tm=tm)
    past = (cache_mla_latent[l], cache_mla_krope[l], state_hgrn[l])
    h2s, lat_s, kr_s, st_s, _ = _mixer(x_sample, past_len + jnp.arange(Sd), past, wts, [], tm=tm)

    T_all = Tp + Ts
    u_all, idx, gate, cnt = _router(h2p, h2s, norm_ffn_g[l][None], w_router[l], b_router[l][None], tm=tm)

    te_rows = 512
    A = T_all * TOP_K
    counts = cnt[0].astype(jnp.int32)
    padded = (counts + te_rows - 1) // te_rows * te_rows
    pad_end = jnp.cumsum(padded)
    pad_start = pad_end - padded
    used_end = pad_start + counts
    e_sel, rank = idx[:, :TOP_K], idx[:, TOP_K:2 * TOP_K]
    onehot = e_sel[:, :, None] == jnp.arange(n_exp, dtype=jnp.int32)[None, None, :]
    dest = (rank + jnp.sum(jnp.where(onehot, pad_start[None, None, :], 0), axis=-1)).reshape(A)
    n_tiles = -(-A // te_rows) + n_exp
    P = n_tiles * te_rows
    tile_start = jnp.arange(n_tiles, dtype=jnp.int32) * te_rows
    tile_e = jnp.minimum(jnp.sum((tile_start[:, None] >= pad_end[None, :]).astype(jnp.int32), axis=1), n_exp - 1)
    tile_nv = jnp.clip(used_end[tile_e] - tile_start, 0, te_rows).astype(jnp.int32)
    tile_nv = jnp.where(tile_start < pad_end[-1], tile_nv, 0)
    last_e = tile_e[jnp.maximum(jnp.sum((tile_start < pad_end[-1]).astype(jnp.int32)) - 1, 0)]
    tile_e = jnp.where(tile_nv > 0, tile_e, last_e)

    dR = 512
    xs = _dispatch(used_end.astype(jnp.int32), pad_end.astype(jnp.int32),
                   dest.reshape(T_all // dR, dR * TOP_K), u_all, P=P, R=dR)
    ys = _experts(tile_e, tile_nv, xs, w_gu_b, b_gate_up[l][:, None, :], w_dn_b, b_down[l][:, None, :],
                  tm=te_rows, fc=1024)

    cR = 256
    y_p, y_s = _combine(dest.reshape(T_all // cR, cR * TOP_K), gate, h2p, h2s, ys, norm_final_g[None], R=cR)

    return (y_p.reshape(B, S, D), y_s.reshape(Bd, Sd, D), lat_p[None], kr_p[None], st_p[None],
            lat_s[None], kr_s[None], st_s[None])
```

```python
import functools
import math

import jax
import jax.numpy as jnp
from jax import lax
from jax.experimental import pallas as pl
from jax.experimental.pallas import tpu as pltpu

F32 = jnp.float32
BF16 = jnp.bfloat16

CHUNK = 64
NORM_EPS = 1e-6
ROPE_BASE = 10000.0
NEG_INF = -1e30
TOP_K = 4
SWIGLU_ALPHA = 1.702
SWIGLU_LIMIT = 7.0

LANE = 128
VMEM_LIMIT = 56 * 1024 * 1024


def _cparams(*sem):
    return pltpu.CompilerParams(dimension_semantics=sem, vmem_limit_bytes=VMEM_LIMIT)


def _dot(a, b):
    return jnp.dot(a, b, preferred_element_type=F32)


def _dot_nt(a, b):
    return lax.dot_general(a, b, (((1,), (1,)), ((), ())), preferred_element_type=F32)


def _rms(x, g):
    return x * lax.rsqrt(jnp.mean(x * x, axis=-1, keepdims=True) + NORM_EPS) * g


def _sigmoid(x):
    return 1.0 / (1.0 + jnp.exp(-x))


def _pack_bf16_pairs(x):
    half = x.shape[1] // 2
    bits = lax.bitcast_convert_type(x.astype(BF16).astype(F32), jnp.uint32)
    return (bits[:, :half] & jnp.uint32(0xFFFF0000)) | (bits[:, half:] >> 16)


def _unpack_bf16_pairs(p):
    hi = lax.bitcast_convert_type(p & jnp.uint32(0xFFFF0000), F32).astype(BF16)
    lo = lax.bitcast_convert_type(p << 16, F32).astype(BF16)
    return hi, lo


def _small_proj_kernel(x_ref, g_ref, w_ref, qg_ref, kvg_ref, cos_ref, sin_ref,
                       u_ref, cqn_ref, lat_ref, kr_ref, krb_ref, *, ql, kl, rope):
    u = _rms(x_ref[...], g_ref[...]).astype(BF16)
    u_ref[...] = u
    y = _dot(u, w_ref[...])
    cqn_ref[...] = _rms(y[:, :ql], qg_ref[...]).astype(BF16)
    lat_ref[...] = _rms(y[:, ql:ql + kl], kvg_ref[...])
    o = ql + kl
    kr = y[:, o:o + LANE] * cos_ref[...] + y[:, o + LANE:o + 2 * LANE] * sin_ref[...]
    kr_ref[...] = kr[:, :rope]
    krb_ref[...] = kr.astype(BF16)


def _small_proj(x2, g, w_small, qg, kvg, cos_t, sin_t, *, tm, ql, kl, rope):
    T, D = x2.shape
    n = w_small.shape[1]
    nblk = cos_t.shape[0] // tm
    row = lambda i: (i, 0)
    fix = lambda i: (0, 0)
    tab = lambda i: (i % nblk, 0)
    return pl.pallas_call(
        functools.partial(_small_proj_kernel, ql=ql, kl=kl, rope=rope),
        grid=(T // tm,),
        in_specs=[pl.BlockSpec((tm, D), row), pl.BlockSpec((1, D), fix), pl.BlockSpec((D, n), fix),
                  pl.BlockSpec((1, ql), fix), pl.BlockSpec((1, kl), fix),
                  pl.BlockSpec((tm, LANE), tab), pl.BlockSpec((tm, LANE), tab)],
        out_specs=[pl.BlockSpec((tm, D), row), pl.BlockSpec((tm, ql), row), pl.BlockSpec((tm, kl), row),
                   pl.BlockSpec((tm, rope), row), pl.BlockSpec((tm, LANE), row)],
        out_shape=[jax.ShapeDtypeStruct((T, D), BF16), jax.ShapeDtypeStruct((T, ql), BF16),
                   jax.ShapeDtypeStruct((T, kl), F32), jax.ShapeDtypeStruct((T, rope), F32),
                   jax.ShapeDtypeStruct((T, LANE), BF16)],
        compiler_params=_cparams("parallel"),
        name="small_proj",
    )(x2, g, w_small, qg, kvg, cos_t, sin_t)


def _big_proj_kernel(u_ref, *refs, nseg, nside):
    ws, sides = refs[:nseg], refs[nseg:nseg + nside]
    outs, side_outs = refs[nseg + nside:2 * nseg + nside], refs[2 * nseg + nside:]
    u = u_ref[...]
    for w_ref, o_ref in zip(ws, outs):
        o_ref[...] = _dot(u, w_ref[...]).astype(o_ref.dtype)
    for s_ref, o_ref in zip(sides, side_outs):
        o_ref[...] = s_ref[...].astype(o_ref.dtype)


def _big_proj(u, ws, out_dtypes, side, *, tm, tn):
    T, D = u.shape
    n = ws[0].shape[1]
    nseg = len(ws)
    nj = n // tn
    steps = (T // tm) * nj
    side2 = [a.reshape(-1, a.shape[-1]) for a in side]
    side_blocks = [(a.shape[0] // steps, a.shape[1]) for a in side2]
    assert all(a.shape[0] % steps == 0 and b[0] % 16 == 0 for a, b in zip(side2, side_blocks))
    slab = lambda i, j: (i * nj + j, 0)
    res = pl.pallas_call(
        functools.partial(_big_proj_kernel, nseg=nseg, nside=len(side)),
        grid=(T // tm, nj),
        in_specs=[pl.BlockSpec((tm, D), lambda i, j: (i, 0))]
                 + [pl.BlockSpec((D, tn), lambda i, j: (0, j)) for _ in ws]
                 + [pl.BlockSpec(b, slab) for b in side_blocks],
        out_specs=[pl.BlockSpec((tm, tn), lambda i, j: (i, j)) for _ in ws]
                  + [pl.BlockSpec(b, slab) for b in side_blocks],
        out_shape=[jax.ShapeDtypeStruct((T, n), dt) for dt in out_dtypes]
                  + [jax.ShapeDtypeStruct(a.shape, BF16) for a in side2],
        compiler_params=_cparams("arbitrary", "arbitrary"),
        name="big_proj",
    )(u, *ws, *side2)
    return res[:nseg], [o.reshape(a.shape) for o, a in zip(res[nseg:], side)]


def _q_proj_kernel(c_ref, wr_ref, ws_ref, cos_ref, sin_ref, q_ref, *, heads):
    c = c_ref[...]
    raw = _dot(c, wr_ref[...])
    sw = _dot(c, ws_ref[...])
    cos = cos_ref[...]
    sin = sin_ref[...]
    for h in range(heads):
        b = 2 * LANE * h
        q_ref[:, b:b + LANE] = raw[:, b:b + LANE].astype(BF16)
        r = raw[:, b + LANE:b + 2 * LANE] * cos + sw[:, LANE * h:LANE * (h + 1)] * sin
        q_ref[:, b + LANE:b + 2 * LANE] = r.astype(BF16)


def _q_proj(cqn, wq_raw, wq_sw, cos_t, sin_t, *, tm, heads):
    T, ql = cqn.shape
    nblk = cos_t.shape[0] // tm
    row = lambda i: (i, 0)
    fix = lambda i: (0, 0)
    tab = lambda i: (i % nblk, 0)
    n = wq_raw.shape[1]
    return pl.pallas_call(
        functools.partial(_q_proj_kernel, heads=heads),
        grid=(T // tm,),
        in_specs=[pl.BlockSpec((tm, ql), row), pl.BlockSpec(wq_raw.shape, fix), pl.BlockSpec(wq_sw.shape, fix),
                  pl.BlockSpec((tm, LANE), tab), pl.BlockSpec((tm, LANE), tab)],
        out_specs=pl.BlockSpec((tm, n), row),
        out_shape=jax.ShapeDtypeStruct((T, n), BF16),
        compiler_params=_cparams("parallel"),
        name="q_proj",
    )(cqn, wq_raw, wq_sw, cos_t, sin_t)


def _kv_proj_kernel(lat_ref, krb_ref, wk_ref, wv_ref, k_ref, v_ref, *, heads):
    lat = lat_ref[...].astype(BF16)
    kn = _dot(lat, wk_ref[...]).astype(BF16)
    v_ref[...] = _dot(lat, wv_ref[...]).astype(BF16)
    krb = krb_ref[...]
    for h in range(heads):
        b = 2 * LANE * h
        k_ref[:, b:b + LANE] = kn[:, LANE * h:LANE * (h + 1)]
        k_ref[:, b + LANE:b + 2 * LANE] = krb


def _kv_proj(lat, krb, wk, wv, *, tm, heads):
    T, kl = lat.shape
    row = lambda i: (i, 0)
    fix = lambda i: (0, 0)
    return pl.pallas_call(
        functools.partial(_kv_proj_kernel, heads=heads),
        grid=(T // tm,),
        in_specs=[pl.BlockSpec((tm, kl), row), pl.BlockSpec((tm, LANE), row),
                  pl.BlockSpec(wk.shape, fix), pl.BlockSpec(wv.shape, fix)],
        out_specs=[pl.BlockSpec((tm, 2 * LANE * heads), row), pl.BlockSpec((tm, wv.shape[1]), row)],
        out_shape=[jax.ShapeDtypeStruct((T, 2 * LANE * heads), BF16), jax.ShapeDtypeStruct((T, wv.shape[1]), BF16)],
        compiler_params=_cparams("parallel"),
        name="kv_proj",
    )(lat, krb, wk, wv)


def _attn_body(q_ref, k_ref, v_ref, o_ref, *, tq, tk, hpb):
    S = q_ref.shape[1]
    kd = 2 * LANE
    vd = v_ref.shape[2] // hpb

    def update(carry, s, v):
        m, l, acc = carry
        m_new = jnp.maximum(m, jnp.max(s, axis=-1, keepdims=True))
        alpha = jnp.exp(m - m_new)
        p = jnp.exp(s - m_new)
        l = alpha * l + jnp.sum(p, axis=-1, keepdims=True)
        acc = alpha * acc + _dot(p.astype(BF16), v)
        return m_new, l, acc

    for r0 in range(0, S, tq):
        seen = r0 + tq
        blocks = [(k0, min(tk, seen - k0)) for k0 in range(0, seen, tk)]
        for g in range(hpb):
            q = q_ref[0, r0:r0 + tq, g * kd:(g + 1) * kd]
            carry = (jnp.full((tq, 1), NEG_INF, F32), jnp.zeros((tq, 1), F32), jnp.zeros((tq, vd), F32))
            for k0, kl in blocks:
                k = k_ref[0, k0:k0 + kl, g * kd:(g + 1) * kd]
                v = v_ref[0, k0:k0 + kl, g * vd:(g + 1) * vd]
                s = _dot_nt(q, k)
                if k0 + kl > r0:
                    qc = (lax.broadcasted_iota(jnp.int32, (tq, kl), 0) + r0) // CHUNK
                    kc = (lax.broadcasted_iota(jnp.int32, (tq, kl), 1) + k0) // CHUNK
                    s = jnp.where(kc <= qc, s, NEG_INF)
                carry = update(carry, s, v)
            m, l, acc = carry
            o_ref[0, r0:r0 + tq, g * vd:(g + 1) * vd] = (acc / l).astype(BF16)


def _attn_kernel(q_ref, k_ref, v_ref, o_ref, *, tq, tk, hpb):
    _attn_body(q_ref, k_ref, v_ref, o_ref, tq=tq, tk=tk, hpb=hpb)


def _attention(qf, kf, v, *, heads, tq, tk, hpb):
    B, S, _ = qf.shape
    vd = v.shape[2] // heads
    seq = lambda b, h: (b, 0, h)
    return pl.pallas_call(
        functools.partial(_attn_kernel, tq=tq, tk=tk, hpb=hpb),
        grid=(B, heads // hpb),
        in_specs=[pl.BlockSpec((1, S, 2 * LANE * hpb), seq), pl.BlockSpec((1, S, 2 * LANE * hpb), seq),
                  pl.BlockSpec((1, S, vd * hpb), seq)],
        out_specs=pl.BlockSpec((1, S, vd * hpb), seq),
        out_shape=jax.ShapeDtypeStruct((B, S, heads * vd), BF16),
        compiler_params=_cparams("parallel", "parallel"),
        name="mla_prompt_attn",
    )(qf, kf, v)


def _attn_sample_kernel(q_ref, cl_ref, ck_ref, ln_ref, kn_ref, wuk_ref, wuv_ref, o_ref,
                        ql_ref, qr_ref, ol_ref, *, heads, rope):
    sd = q_ref.shape[1]
    for h in range(heads):
        b = 2 * LANE * h
        ql_ref[h * sd:(h + 1) * sd, :] = _dot(q_ref[0, :, b:b + LANE], wuk_ref[h]).astype(BF16)
        qr_ref[h * sd:(h + 1) * sd, :] = q_ref[0, :, b + LANE:b + 2 * LANE]
    ql = ql_ref[...]
    qr = qr_ref[...][:, :rope]
    cl = cl_ref[0].astype(BF16)
    ck = ck_ref[0].astype(BF16)
    ln = ln_ref[0].astype(BF16)
    kn = kn_ref[0].astype(BF16)
    s_past = _dot_nt(ql, cl) + _dot_nt(qr, ck)
    s_new = _dot_nt(ql, ln) + _dot_nt(qr, kn)
    m = jnp.maximum(jnp.max(s_past, axis=-1, keepdims=True), jnp.max(s_new, axis=-1, keepdims=True))
    p_past = jnp.exp(s_past - m)
    p_new = jnp.exp(s_new - m)
    l = jnp.sum(p_past, axis=-1, keepdims=True) + jnp.sum(p_new, axis=-1, keepdims=True)
    o_lat = (_dot(p_past.astype(BF16), cl) + _dot(p_new.astype(BF16), ln)) / l
    ol_ref[...] = o_lat.astype(BF16)
    vd = wuv_ref.shape[2]
    for h in range(heads):
        o_ref[0, :, h * vd:(h + 1) * vd] = _dot(ol_ref[h * sd:(h + 1) * sd, :], wuv_ref[h]).astype(BF16)


def _attention_sample(qf, cache_lat, cache_kr, lat_new, kr_new, wuk_t, wuv, *, heads):
    B, sd, _ = qf.shape
    P, kl = cache_lat.shape[1:]
    rope = cache_kr.shape[2]
    vd = wuv.shape[2]
    b3 = lambda b: (b, 0, 0)
    fix = lambda b: (0, 0, 0)
    return pl.pallas_call(
        functools.partial(_attn_sample_kernel, heads=heads, rope=rope),
        grid=(B,),
        in_specs=[pl.BlockSpec((1, sd, qf.shape[2]), b3), pl.BlockSpec((1, P, kl), b3), pl.BlockSpec((1, P, rope), b3),
                  pl.BlockSpec((1, sd, kl), b3), pl.BlockSpec((1, sd, rope), b3),
                  pl.BlockSpec(wuk_t.shape, fix), pl.BlockSpec(wuv.shape, fix)],
        out_specs=pl.BlockSpec((1, sd, heads * vd), b3),
        out_shape=jax.ShapeDtypeStruct((B, sd, heads * vd), BF16),
        scratch_shapes=[pltpu.VMEM((heads * sd, kl), BF16), pltpu.VMEM((heads * sd, LANE), BF16),
                        pltpu.VMEM((heads * sd, kl), BF16)],
        compiler_params=_cparams("parallel"),
        name="mla_sample_attn",
    )(qf, cache_lat, cache_kr, lat_new, kr_new, wuk_t, wuv)


def _hgrn_chunk(hq_ref, hf_ref, hi_ref, hg_ref, lb_ref, g_ref, o_ref, st_ref, rows, *, heads):
    L = rows.stop - rows.start
    dk = st_ref.shape[2]
    row = lax.broadcasted_iota(jnp.int32, (L, L), 0)
    col = lax.broadcasted_iota(jnp.int32, (L, L), 1)
    causal = col <= row
    tri = jnp.where(causal, 1.0, 0.0).astype(BF16)
    lb = lb_ref[...]
    f = lb + (1.0 - lb) * _sigmoid(hf_ref[0, rows, :])
    kk = 1.0 - f
    hi, mid, lo = _split3(jnp.log(f))
    cum = _dot(tri, hi) + _dot(tri, mid) + _dot(tri, lo)
    tot = cum[L - 1:L, :]
    q_dec = (hq_ref[0, rows, :].astype(F32) * jnp.exp(cum)).astype(BF16)
    k_dec = (kk * jnp.exp(-cum)).astype(BF16)
    k2 = (kk * jnp.exp(tot - cum)).astype(BF16)
    decay = jnp.exp(tot)
    vb = hi_ref[0, rows, :]
    v32 = vb.astype(F32)
    sls = [slice(h * dk, (h + 1) * dk) for h in range(heads)]
    atts = [jnp.where(causal, _dot_nt(q_dec[:, sl], k_dec[:, sl]), 0.0).astype(BF16) for sl in sls]
    sts = [st_ref[h] for h in range(heads)]
    outs = [_dot_nt(q_dec[:, sl], sts[h].astype(BF16)) + _dot(atts[h], vb[:, sl]) for h, sl in enumerate(sls)]
    for h, sl in enumerate(sls):
        st_ref[h] = decay[:, sl] * sts[h] + _dot(v32[:, sl].T.astype(BF16), k2[:, sl])
    g = g_ref[...]
    for h, sl in enumerate(sls):
        o = outs[h]
        on = o * lax.rsqrt(jnp.mean(o * o, axis=-1, keepdims=True) + NORM_EPS) * g
        hg = hg_ref[0, rows, sl].astype(F32)
        o_ref[0, rows, sl] = (on * (hg * _sigmoid(hg))).astype(BF16)


def _hgrn_kernel(*refs, heads, has_state, chunk):
    if has_state:
        hq_ref, hf_ref, hi_ref, hg_ref, lb_ref, g_ref, s0_ref, o_ref, sn_ref, st_ref = refs
    else:
        hq_ref, hf_ref, hi_ref, hg_ref, lb_ref, g_ref, o_ref, sn_ref, st_ref = refs
    c = pl.program_id(1)

    @pl.when(c == 0)
    def _():
        if has_state:
            for h in range(heads):
                st_ref[h] = s0_ref[0, h].T
        else:
            st_ref[...] = jnp.zeros_like(st_ref)

    for c0 in range(0, hq_ref.shape[1], chunk):
        _hgrn_chunk(hq_ref, hf_ref, hi_ref, hg_ref, lb_ref, g_ref, o_ref, st_ref, slice(c0, c0 + chunk), heads=heads)

    @pl.when(c == pl.num_programs(1) - 1)
    def _():
        for h in range(heads):
            sn_ref[0, h] = st_ref[h].T


def _hgrn(hq, hf, hi, hg, lb, g, s0, *, heads, L, chunk):
    B, S, F = hq.shape
    dk = F // heads
    blk = lambda b, c: (b, c, 0)
    fix2 = lambda b, c: (0, 0)
    st4 = lambda b, c: (b, 0, 0, 0)
    in_specs = [pl.BlockSpec((1, L, F), blk)] * 4 + [pl.BlockSpec((1, F), fix2), pl.BlockSpec((1, dk), fix2)]
    args = [hq, hf, hi, hg, lb, g]
    if s0 is not None:
        in_specs.append(pl.BlockSpec((1, heads, dk, dk), st4))
        args.append(s0)
    return pl.pallas_call(
        functools.partial(_hgrn_kernel, heads=heads, has_state=s0 is not None, chunk=chunk),
        grid=(B, S // L),
        in_specs=in_specs,
        out_specs=[pl.BlockSpec((1, L, F), blk), pl.BlockSpec((1, heads, dk, dk), st4)],
        out_shape=[jax.ShapeDtypeStruct((B, S, F), BF16), jax.ShapeDtypeStruct((B, heads, dk, dk), F32)],
        scratch_shapes=[pltpu.VMEM((heads, dk, dk), F32)],
        compiler_params=_cparams("parallel", "arbitrary"),
        name="hgrn",
    )(*args)


def _out_proj_kernel(oa_ref, ob_ref, ga_ref, gb_ref, x_ref, w_ref, o_ref, *, kc):
    acc = x_ref[...]
    for k in range(w_ref.shape[0] // kc):
        sl = slice(k * kc, (k + 1) * kc)
        ga = _sigmoid(ga_ref[:, sl].astype(F32))
        gb = _sigmoid(gb_ref[:, sl].astype(F32))
        m = (ga * oa_ref[:, sl].astype(F32) + gb * ob_ref[:, sl].astype(F32)).astype(BF16)
        acc = acc + _dot(m, w_ref[sl, :])
    o_ref[...] = acc


def _out_proj(oa, ob, ga, gb, x2, w_o, *, tm):
    T, D = x2.shape
    row = lambda i: (i, 0)
    return pl.pallas_call(
        functools.partial(_out_proj_kernel, kc=512),
        grid=(T // tm,),
        in_specs=[pl.BlockSpec((tm, D), row)] * 5 + [pl.BlockSpec((D, D), lambda i: (0, 0))],
        out_specs=pl.BlockSpec((tm, D), row),
        out_shape=jax.ShapeDtypeStruct((T, D), F32),
        compiler_params=_cparams("parallel"),
        name="out_proj",
    )(oa, ob, ga, gb, x2, w_o)


def _split3(x):
    hi = x.astype(BF16)
    r = x - hi.astype(F32)
    mid = r.astype(BF16)
    lo = (r - mid.astype(F32)).astype(BF16)
    return hi, mid, lo


def _router_kernel(hp_ref, hs_ref, g_ref, w3_ref, b_ref, u_ref, idx_ref, gate_ref, cnt_ref, run_ref, *, n_exp, n_p):
    step = pl.program_id(0)

    @pl.when(step == 0)
    def _():
        run_ref[...] = jnp.zeros_like(run_ref)

    h = jnp.where(step < n_p, hp_ref[...], hs_ref[...])
    u = _rms(h, g_ref[...])
    u_ref[...] = _pack_bf16_pairs(u)
    u_hi, u_mid, u_lo = _split3(u)
    e1, e2 = n_exp, 2 * n_exp
    p_hi = _dot(u_hi, w3_ref[...])
    p_mid = _dot(u_mid, w3_ref[:, :e2])
    p_lo = _dot(u_lo, w3_ref[:, :e1])
    logits = (p_hi[:, :e1] + (p_hi[:, e1:e2] + p_mid[:, :e1]) + (p_hi[:, e2:] + p_mid[:, e1:] + p_lo)) + b_ref[...]
    tm = logits.shape[0]
    lane_e = lax.broadcasted_iota(jnp.int32, (tm, n_exp), 1)
    lane_o = lax.broadcasted_iota(jnp.int32, (tm, LANE), 1)
    idx_out = jnp.zeros((tm, LANE), jnp.int32)
    val_out = jnp.zeros((tm, LANE), F32)
    vals, picks = [], []
    for k in range(TOP_K):
        m = jnp.max(logits, axis=-1, keepdims=True)
        i = jnp.min(jnp.where(logits == m, lane_e, n_exp), axis=-1, keepdims=True)
        pick = lane_e == i
        logits = jnp.where(pick, -jnp.inf, logits)
        vals.append(m)
        picks.append(pick)
        idx_out = jnp.where(lane_o == k, i, idx_out)
    es = [jnp.exp(v - vals[0]) for v in vals]
    den = es[0]
    for e in es[1:]:
        den = den + e
    for k in range(TOP_K):
        val_out = jnp.where(lane_o == k, es[k] / den, val_out)
    multi = jnp.zeros((tm, n_exp), F32)
    for pick in picks:
        multi = multi + jnp.where(pick, 1.0, 0.0)
    earlier = lax.broadcasted_iota(jnp.int32, (tm, tm), 1) < lax.broadcasted_iota(jnp.int32, (tm, tm), 0)
    before = _dot(jnp.where(earlier, 1.0, 0.0).astype(BF16), multi.astype(BF16)) + run_ref[...]
    for k in range(TOP_K):
        rank = jnp.sum(jnp.where(picks[k], before, 0.0), axis=-1, keepdims=True).astype(jnp.int32)
        idx_out = jnp.where(lane_o == TOP_K + k, rank, idx_out)
    run_ref[...] += jnp.sum(multi, axis=0, keepdims=True)
    cnt_ref[...] = run_ref[...]
    idx_ref[...] = idx_out
    gate_ref[...] = val_out


def _two_group_specs(shape, n_p):
    return [pl.BlockSpec(shape, lambda i: (jnp.minimum(i, n_p - 1), 0)),
            pl.BlockSpec(shape, lambda i: (jnp.maximum(i - n_p, 0), 0))]


def _router(h2p, h2s, g, w_r, b_r, *, tm):
    (Tp, D), Ts = h2p.shape, h2s.shape[0]
    T = Tp + Ts
    n_exp = w_r.shape[1]
    w3 = jnp.concatenate(_split3(w_r), axis=1)
    row = lambda i: (i, 0)
    fix = lambda i: (0, 0)
    return pl.pallas_call(
        functools.partial(_router_kernel, n_exp=n_exp, n_p=Tp // tm),
        grid=(T // tm,),
        in_specs=_two_group_specs((tm, D), Tp // tm)
                 + [pl.BlockSpec((1, D), fix), pl.BlockSpec((D, 3 * n_exp), fix), pl.BlockSpec((1, n_exp), fix)],
        out_specs=[pl.BlockSpec((tm, D // 2), row), pl.BlockSpec((tm, LANE), row), pl.BlockSpec((tm, LANE), row),
                   pl.BlockSpec((1, n_exp), fix)],
        out_shape=[jax.ShapeDtypeStruct((T, D // 2), jnp.uint32), jax.ShapeDtypeStruct((T, LANE), jnp.int32),
                   jax.ShapeDtypeStruct((T, LANE), F32), jax.ShapeDtypeStruct((1, n_exp), F32)],
        scratch_shapes=[pltpu.VMEM((1, n_exp), F32)],
        compiler_params=_cparams("arbitrary"),
        name="router",
    )(h2p, h2s, g, w3, b_r)


def _row_copy(src_ref, dst_ref, src_row, dst_row, sem):
    return pltpu.make_async_copy(src_ref.at[pl.ds(src_row, 1)], dst_ref.at[pl.ds(dst_row, 1)], sem)


def _dispatch_kernel(zs_ref, ze_ref, dest_hbm, u_ref, xs_hbm, idx_smem, zrow_ref, sem_idx, sem, sem_z, *, R, n_exp):
    i = pl.program_id(0)
    n = pl.num_programs(0)
    slot = i % 2
    N = R * TOP_K

    def idx_copy(step, s):
        return pltpu.make_async_copy(dest_hbm.at[step], idx_smem.at[pl.ds(pl.multiple_of(s * N, N), N)],
                                     sem_idx.at[s])

    def start_rows(s):
        def body(r, c):
            for k in range(TOP_K):
                _row_copy(u_ref, xs_hbm, r, idx_smem[s * N + r * TOP_K + k], sem).start(priority=k % 2)
            return c
        lax.fori_loop(0, R, body, 0, unroll=8)

    def wait_rows():
        def body(r, c):
            for k in range(TOP_K):
                _row_copy(u_ref, xs_hbm, 0, 0, sem).wait()
            return c
        lax.fori_loop(0, R, body, 0, unroll=8)

    def zero_rows(start):
        def per_expert(e, c):
            def body(r, c2):
                cp = _row_copy(zrow_ref, xs_hbm, 0, r, sem_z)
                cp.start() if start else cp.wait()
                return c2
            return lax.fori_loop(zs_ref[e], ze_ref[e], body, c)
        lax.fori_loop(0, n_exp, per_expert, 0)

    @pl.when(i == 0)
    def _():
        zrow_ref[...] = jnp.zeros_like(zrow_ref)
        idx_copy(0, 0).start()

    idx_copy(i, slot).wait()

    @pl.when(i + 1 < n)
    def _():
        idx_copy(i + 1, 1 - slot).start()

    start_rows(slot)

    @pl.when(i == n - 1)
    def _():
        zero_rows(True)

    wait_rows()

    @pl.when(i == n - 1)
    def _():
        zero_rows(False)


def _dispatch(zs, ze, dest2, u, *, P, R):
    n = dest2.shape[0]
    D = u.shape[1]
    grid_spec = pltpu.PrefetchScalarGridSpec(
        num_scalar_prefetch=2,
        grid=(n,),
        in_specs=[pl.BlockSpec(memory_space=pl.ANY), pl.BlockSpec((R, D), lambda i, zs, ze: (i, 0))],
        out_specs=pl.BlockSpec(memory_space=pl.ANY),
        scratch_shapes=[pltpu.SMEM((2 * R * TOP_K,), jnp.int32), pltpu.VMEM((8, D), u.dtype),
                        pltpu.SemaphoreType.DMA((2,)), pltpu.SemaphoreType.DMA, pltpu.SemaphoreType.DMA],
    )
    return pl.pallas_call(
        functools.partial(_dispatch_kernel, R=R, n_exp=zs.shape[0]),
        grid_spec=grid_spec,
        out_shape=jax.ShapeDtypeStruct((P, D), u.dtype),
        compiler_params=_cparams("arbitrary"),
        name="moe_dispatch",
    )(zs, ze, dest2, u)


def _expert_kernel(te_ref, nv_ref, x_ref, wg_ref, wu_ref, bg_ref, bu_ref, wd_ref, bd_ref, y_ref, xb_ref):
    i = pl.program_id(0)
    c = pl.program_id(1)
    half = x_ref.shape[1]

    live = nv_ref[i] > 0

    def chunk(first):
        if first:
            hi, lo = _unpack_bf16_pairs(x_ref[...])
            xb_ref[:, :half] = hi
            xb_ref[:, half:] = lo
            xb = jnp.concatenate([hi, lo], axis=1)
        else:
            xb = xb_ref[...]
        gate = jnp.minimum(_dot(xb, wg_ref[0]) + bg_ref[0], SWIGLU_LIMIT)
        up = jnp.clip(_dot(xb, wu_ref[0]) + bu_ref[0], -SWIGLU_LIMIT, SWIGLU_LIMIT)
        act = gate * _sigmoid(SWIGLU_ALPHA * gate) * (up + 1.0)
        down = _dot(act.astype(BF16), wd_ref[0])
        if first:
            y_ref[...] = down + bd_ref[0]
        else:
            y_ref[...] += down

    @pl.when(jnp.logical_and(live, c == 0))
    def _():
        chunk(True)

    @pl.when(jnp.logical_and(live, c > 0))
    def _():
        chunk(False)

    @pl.when(jnp.logical_and(jnp.logical_not(live), c == 0))
    def _():
        y_ref[...] = jnp.broadcast_to(bd_ref[0], y_ref.shape)


def _experts(tile_e, tile_nv, xs, w_gu, b_gu, w_dn, b_dn, *, tm, fc):
    P, half = xs.shape
    D = 2 * half
    E, _, ff2 = w_gu.shape
    ff = ff2 // 2
    nc = ff // fc
    n_tiles = P // tm

    def cc(i, c, nv):
        return jnp.where(nv[i] > 0, c, nc - 1)

    grid_spec = pltpu.PrefetchScalarGridSpec(
        num_scalar_prefetch=2,
        grid=(n_tiles, nc),
        in_specs=[
            pl.BlockSpec((tm, half), lambda i, c, te, nv: (jnp.where(nv[i] > 0, i, 0), 0)),
            pl.BlockSpec((1, D, fc), lambda i, c, te, nv: (te[i], 0, cc(i, c, nv))),
            pl.BlockSpec((1, D, fc), lambda i, c, te, nv: (te[i], 0, nc + cc(i, c, nv))),
            pl.BlockSpec((1, 1, fc), lambda i, c, te, nv: (te[i], 0, cc(i, c, nv))),
            pl.BlockSpec((1, 1, fc), lambda i, c, te, nv: (te[i], 0, nc + cc(i, c, nv))),
            pl.BlockSpec((1, fc, D), lambda i, c, te, nv: (te[i], cc(i, c, nv), 0)),
            pl.BlockSpec((1, 1, D), lambda i, c, te, nv: (te[i], 0, 0)),
        ],
        out_specs=pl.BlockSpec((tm, D), lambda i, c, te, nv: (i, 0)),
        scratch_shapes=[pltpu.VMEM((tm, D), BF16)],
    )
    return pl.pallas_call(
        _expert_kernel,
        grid_spec=grid_spec,
        out_shape=jax.ShapeDtypeStruct((P, D), F32),
        compiler_params=_cparams("arbitrary", "arbitrary"),
        name="moe_experts",
    )(tile_e, tile_nv, xs, w_gu, w_gu, b_gu, b_gu, w_dn, b_dn)


def _combine_kernel(dest_hbm, gate_ref, hp_ref, hs_ref, ys_hbm, g_ref, yp_ref, ysm_ref, buf_ref, idx_smem, sem_idx,
                    sem, *, n_p):
    i = pl.program_id(0)
    n = pl.num_programs(0)
    R = hp_ref.shape[0]
    slot = i % 2
    N = R * TOP_K

    def idx_copy(step, s):
        return pltpu.make_async_copy(dest_hbm.at[step], idx_smem.at[pl.ds(pl.multiple_of(s * N, N), N)],
                                     sem_idx.at[s])

    def start_rows(s):
        def body(r, c):
            for k in range(TOP_K):
                _row_copy(ys_hbm, buf_ref.at[s, k], idx_smem[s * N + r * TOP_K + k], r,
                          sem.at[s]).start(priority=k % 2)
            return c
        lax.fori_loop(0, R, body, 0, unroll=8)

    def wait_rows(s):
        def body(r, c):
            for k in range(TOP_K):
                _row_copy(ys_hbm, buf_ref.at[s, k], 0, r, sem.at[s]).wait()
            return c
        lax.fori_loop(0, R, body, 0, unroll=8)

    @pl.when(i == 0)
    def _():
        cp = idx_copy(0, 0)
        cp.start()
        cp.wait()
        start_rows(0)

        @pl.when(n > 1)
        def _():
            idx_copy(1, 1).start()

    @pl.when(i + 1 < n)
    def _():
        idx_copy(i + 1, 1 - slot).wait()
        start_rows(1 - slot)

    @pl.when(i + 2 < n)
    def _():
        idx_copy(i + 2, slot).start()

    wait_rows(slot)
    is_p = i < n_p
    h = jnp.where(is_p, hp_ref[...], hs_ref[...])
    gate = gate_ref[...]
    for k in range(TOP_K):
        h = h + gate[:, k:k + 1] * buf_ref[slot, k]
    y = _rms(h, g_ref[...])

    @pl.when(is_p)
    def _():
        yp_ref[...] = y

    @pl.when(jnp.logical_not(is_p))
    def _():
        ysm_ref[...] = y


def _combine(dest2, gate, h2p, h2s, ys, g, *, R):
    (Tp, D), Ts = h2p.shape, h2s.shape[0]
    n_p = Tp // R
    row = lambda i: (i, 0)
    return pl.pallas_call(
        functools.partial(_combine_kernel, n_p=n_p),
        grid=((Tp + Ts) // R,),
        in_specs=[pl.BlockSpec(memory_space=pl.ANY), pl.BlockSpec((R, LANE), row)] + _two_group_specs((R, D), n_p)
                 + [pl.BlockSpec(memory_space=pl.ANY), pl.BlockSpec((1, D), lambda i: (0, 0))],
        out_specs=_two_group_specs((R, D), n_p),
        out_shape=[jax.ShapeDtypeStruct((Tp, D), F32), jax.ShapeDtypeStruct((Ts, D), F32)],
        scratch_shapes=[pltpu.VMEM((2, TOP_K, R, D), F32), pltpu.SMEM((2 * R * TOP_K,), jnp.int32),
                        pltpu.SemaphoreType.DMA((2,)), pltpu.SemaphoreType.DMA((2,))],
        compiler_params=_cparams("arbitrary"),
        name="moe_combine",
    )(dest2, gate, h2p, h2s, ys, g)


def _rope_tables(pos, rope, rows):
    half = rope // 2
    inv = ROPE_BASE ** (-jnp.arange(half, dtype=F32) / half)
    ang = pos.astype(F32)[:, None] * inv[None, :]
    cos, sin = jnp.cos(ang), jnp.sin(ang)
    z = jnp.zeros((pos.shape[0], LANE - rope), F32)
    cos_t = jnp.concatenate([cos, cos, z], axis=1)
    sin_t = jnp.concatenate([-sin, sin, z], axis=1)
    rep = max(1, rows // pos.shape[0])
    return jnp.tile(cos_t, (rep, 1)), jnp.tile(sin_t, (rep, 1))


def _swap_halves(w):
    half = w.shape[-1] // 2
    return jnp.concatenate([w[..., half:], w[..., :half]], axis=-1)


def _mixer(x, pos, past, wts, side, *, tm):
    B, S, D = x.shape
    T = B * S
    heads, ql, kl, rope = wts["heads"], wts["ql"], wts["kl"], wts["rope"]
    x2 = x.reshape(T, D)
    cos_t, sin_t = _rope_tables(pos, rope, tm)
    u, cqn, lat, kr, krb = _small_proj(x2, wts["norm_mix_g"], wts["w_small"], wts["q_norm_g"], wts["kv_norm_g"],
                                       cos_t, sin_t, tm=tm, ql=ql, kl=kl, rope=rope)
    tm_big = min(1024, T)
    (hq, hf, hi, hg, ga, gb), side_out = _big_proj(u, wts["w_big"], (BF16, F32, BF16, BF16, BF16, BF16), side,
                                                   tm=tm_big, tn=256)
    qf = _q_proj(cqn, wts["wq_raw"], wts["wq_sw"], cos_t, sin_t, tm=tm, heads=heads)
    F = hq.shape[1]
    r3 = lambda a: a.reshape(B, S, a.shape[1])
    if past is None:
        kf, v = _kv_proj(lat, krb, wts["w_uk"], wts["w_uv"], tm=tm, heads=heads)
        o_a = _attention(r3(qf), r3(kf), r3(v), heads=heads, tq=min(512, S), tk=min(1024, S), hpb=2)
        o_b, st = _hgrn(r3(hq), r3(hf), r3(hi), r3(hg), wts["lb"], wts["hgrn_g"], None, heads=wts["hg_heads"],
                        L=min(8 * CHUNK, S), chunk=CHUNK)
    else:
        cache_lat, cache_kr, state = past
        o_a = _attention_sample(r3(qf), cache_lat, cache_kr, r3(lat), r3(kr), wts["w_uk_t"], wts["w_uv_h"],
                                heads=heads)
        o_b, st = _hgrn(r3(hq), r3(hf), r3(hi), r3(hg), wts["lb"], wts["hgrn_g"], state, heads=wts["hg_heads"], L=S,
                        chunk=S)
    h2 = _out_proj(o_a.reshape(T, D), o_b.reshape(T, F), ga, gb, x2, wts["w_o"], tm=512)
    return h2, lat.reshape(B, S, kl), kr.reshape(B, S, rope), st, side_out


def kernel(x_prompt, x_sample, cache_mla_latent, cache_mla_krope, state_hgrn, norm_mix_g, w_in, q_norm_g, w_uq,
           kv_norm_g, w_uk, w_uv, hgrn_lb_raw, hgrn_norm_g, w_o, norm_ffn_g, w_router, b_router, w_gate_up,
           b_gate_up, w_down, b_down, norm_final_g):
    depth = w_in.shape[0]
    assert depth == 1, "single-layer kernel"
    B, S, D = x_prompt.shape
    Bd, Sd, _ = x_sample.shape
    past_len = cache_mla_latent.shape[2]
    ql, heads, qk = w_uq.shape[1:]
    kl, _, nope = w_uk.shape[1:]
    vd = w_uv.shape[3]
    rope = qk - nope
    hg_heads, dk, dv = state_hgrn.shape[2:]
    n_exp, _, ff2 = w_gate_up.shape[1:]
    assert nope == LANE and vd == LANE and dk == LANE and dv == LANE and rope <= LANE
    hf_dim = hg_heads * dk
    scale = 1.0 / math.sqrt(nope + rope)
    l = 0

    wi = w_in[l]
    o_cq, o_kv, o_pe, o_hq = 0, ql, ql + kl, ql + kl + rope
    k_pe = wi[:, o_pe:o_pe + rope]
    zpad = jnp.zeros((D, LANE - rope), F32)
    w_small = jnp.concatenate([wi[:, o_cq:o_pe], k_pe, zpad, _swap_halves(k_pe), zpad], axis=1).astype(BF16)
    segs = [hf_dim, hf_dim, D, D, D, D]
    offs = [o_hq]
    for s_ in segs[:-1]:
        offs.append(offs[-1] + s_)
    w_big = [wi[:, o:o + s_].astype(BF16) for o, s_ in zip(offs, segs)]
    wq = w_uq[l] * scale
    zq = jnp.zeros((ql, heads, LANE - rope), F32)
    wq_raw = jnp.concatenate([wq[..., :nope], wq[..., nope:], zq], axis=-1).reshape(ql, heads * 2 * LANE).astype(BF16)
    wq_sw = jnp.concatenate([_swap_halves(wq[..., nope:]), zq], axis=-1).reshape(ql, heads * LANE).astype(BF16)
    lb_all = jnp.cumsum(jax.nn.softmax(hgrn_lb_raw.astype(F32), axis=0), axis=0)
    wts = dict(
        heads=heads, ql=ql, kl=kl, rope=rope, hg_heads=hg_heads,
        norm_mix_g=norm_mix_g[l][None], w_small=w_small, q_norm_g=q_norm_g[l][None], kv_norm_g=kv_norm_g[l][None],
        w_big=w_big, wq_raw=wq_raw, wq_sw=wq_sw,
        w_uk=w_uk[l].reshape(kl, heads * nope).astype(BF16), w_uv=w_uv[l].reshape(kl, heads * vd).astype(BF16),
        w_uk_t=jnp.transpose(w_uk[l], (1, 2, 0)).astype(BF16), w_uv_h=jnp.transpose(w_uv[l], (1, 0, 2)).astype(BF16),
        lb=lb_all[l][None], hgrn_g=hgrn_norm_g[l][None], w_o=w_o[l].astype(BF16),
    )

    Tp, Ts = B * S, Bd * Sd
    tm = 512
    assert S % tm == 0 or tm % S == 0
    assert Tp % tm == 0 and Ts % tm == 0 and tm % Sd == 0
    h2p, lat_p, kr_p, st_p, (w_gu_b, w_dn_b) = _mixer(x_prompt, jnp.arange(S), None, wts,
                                                       [w_gate_up[l], w_down[l]], tm=tm)
    past = (cache_mla_latent[l], cache_mla_krope[l], state_hgrn[l])
    h2s, lat_s, kr_s, st_s, _ = _mixer(x_sample, past_len + jnp.arange(Sd), past, wts, [], tm=tm)

    T_all = Tp + Ts
    u_all, idx, gate, cnt = _router(h2p, h2s, norm_ffn_g[l][None], w_router[l], b_router[l][None], tm=tm)

    te_rows = 512
    A = T_all * TOP_K
    counts = cnt[0].astype(jnp.int32)
    padded = (counts + te_rows - 1) // te_rows * te_rows
    pad_end = jnp.cumsum(padded)
    pad_start = pad_end - padded
    used_end = pad_start + counts
    e_sel, rank = idx[:, :TOP_K], idx[:, TOP_K:2 * TOP_K]
    onehot = e_sel[:, :, None] == jnp.arange(n_exp, dtype=jnp.int32)[None, None, :]
    dest = (rank + jnp.sum(jnp.where(onehot, pad_start[None, None, :], 0), axis=-1)).reshape(A)
    n_tiles = -(-A // te_rows) + n_exp
    P = n_tiles * te_rows
    tile_start = jnp.arange(n_tiles, dtype=jnp.int32) * te_rows
    tile_e = jnp.minimum(jnp.sum((tile_start[:, None] >= pad_end[None, :]).astype(jnp.int32), axis=1), n_exp - 1)
    tile_nv = jnp.clip(used_end[tile_e] - tile_start, 0, te_rows).astype(jnp.int32)
    tile_nv = jnp.where(tile_start < pad_end[-1], tile_nv, 0)
    last_e = tile_e[jnp.maximum(jnp.sum((tile_start < pad_end[-1]).astype(jnp.int32)) - 1, 0)]
    tile_e = jnp.where(tile_nv > 0, tile_e, last_e)

    dR = 512
    xs = _dispatch(used_end.astype(jnp.int32), pad_end.astype(jnp.int32),
                   dest.reshape(T_all // dR, dR * TOP_K), u_all, P=P, R=dR)
    ys = _experts(tile_e, tile_nv, xs, w_gu_b, b_gate_up[l][:, None, :], w_dn_b, b_down[l][:, None, :],
                  tm=te_rows, fc=1024)

    cR = 256
    y_p, y_s = _combine(dest.reshape(T_all // cR, cR * TOP_K), gate, h2p, h2s, ys, norm_final_g[None], R=cR)

    return (y_p.reshape(B, S, D), y_s.reshape(Bd, Sd, D), lat_p[None], kr_p[None], st_p[None],
            lat_s[None], kr_s[None], st_s[None])
```
